```python
import math
import jax
import jax.numpy as jnp
from jax import lax
import numpy as np

D_MODEL = 1024
BATCH = 8
SEQ = 4096
DEPTH = 2
DEC_BATCH = 32
DEC_SEQ = 16
PAST_LEN = 4096

CHUNK = 64
N_META = 16
Q_BLOCK = 128
EPS = 1e-6
ROPE_BASE = 10000.0

MLA_HEADS = 8
MLA_NOPE = 64
MLA_ROPE = 32
MLA_V = 64
MLA_Q_LORA = 384
MLA_KV_LORA = 256
MLA_WIDTH = MLA_HEADS * MLA_V

M_HEADS = 4
M_DH = 128
M_WIDTH = M_HEADS * M_DH

G_HEADS = 4
G_DK = 128
G_DV = 128
G_WIDTH = G_HEADS * G_DV
G_QKV = G_HEADS * (2 * G_DK + G_DV)
CONV_W = 4

D_MIX = MLA_WIDTH + M_WIDTH + G_WIDTH

IN_SPLITS = (
    MLA_Q_LORA, MLA_KV_LORA, MLA_ROPE, MLA_WIDTH,
    M_WIDTH, M_WIDTH, M_WIDTH, M_HEADS, M_HEADS, M_WIDTH, M_WIDTH,
    G_QKV, G_HEADS, G_HEADS, G_WIDTH,
)
IN_COLS = sum(IN_SPLITS)

kernel_name = "hybrid_mla_mlstm_gdn_stream_step"


def rms_norm(x, w):
    xf = x.astype(jnp.float32)
    y = xf * lax.rsqrt(jnp.mean(xf * xf, axis=-1, keepdims=True) + EPS)
    return (y * w.astype(jnp.float32)).astype(x.dtype)


def l2_norm(x):
    xf = x.astype(jnp.float32)
    return (xf * lax.rsqrt(jnp.sum(xf * xf, axis=-1, keepdims=True) + EPS)).astype(x.dtype)


def apply_rope(x, pos):
    half = x.shape[-1] // 2
    freq = ROPE_BASE ** (-jnp.arange(half, dtype=jnp.float32) / half)
    ang = pos.astype(jnp.float32)[:, None] * freq[None, :]
    ang = ang.reshape((ang.shape[0],) + (1,) * (x.ndim - 3) + (half,))
    cos = jnp.cos(ang).astype(x.dtype)
    sin = jnp.sin(ang).astype(x.dtype)
    x1, x2 = x[..., :half], x[..., half:]
    return jnp.concatenate([x1 * cos - x2 * sin, x2 * cos + x1 * sin], axis=-1)


def to_chunks(a, blk):
    b, l = a.shape[:2]
    a = a.reshape((b, l // blk, blk) + a.shape[2:])
    return jnp.swapaxes(jnp.moveaxis(a, 1, 0), 2, 3).astype(jnp.float32)


def from_chunks(o):
    n, b, h, t, d = o.shape
    return jnp.transpose(o, (1, 0, 3, 2, 4)).reshape(b, n * t, h, d)


def mla_attention(q_nope, q_rope, c_kv, k_rope, w_uk, w_uv, n_prefix, absorbed):
    L = q_nope.shape[1]
    scale = 1.0 / math.sqrt(MLA_NOPE + MLA_ROPE)
    if absorbed:
        q_lat = jnp.einsum('blhd,rhd->blhr', q_nope, w_uk)
    else:
        k_nope = jnp.einsum('bkr,rhd->bkhd', c_kv, w_uk)
        v = jnp.einsum('bkr,rhd->bkhd', c_kv, w_uv)
    outs = []
    for s in range(0, L, Q_BLOCK):
        e = min(s + Q_BLOCK, L)
        n_own = min(L, -(-e // CHUNK) * CHUNK)
        n_keys = n_prefix + n_own
        q_chunk = jnp.arange(s, e) // CHUNK
        k_chunk = jnp.arange(n_own) // CHUNK
        mask = jnp.concatenate([jnp.ones((e - s, n_prefix), dtype=bool),
                                k_chunk[None, :] <= q_chunk[:, None]], axis=1)
        sc = jnp.einsum('blhd,bkd->bhlk', q_rope[:, s:e], k_rope[:, :n_keys])
        if absorbed:
            sc = sc + jnp.einsum('blhr,bkr->bhlk', q_lat[:, s:e], c_kv[:, :n_keys])
        else:
            sc = sc + jnp.einsum('blhd,bkhd->bhlk', q_nope[:, s:e], k_nope[:, :n_keys])
        p = jax.nn.softmax(jnp.where(mask, sc.astype(jnp.float32) * scale, -jnp.inf), axis=-1).astype(c_kv.dtype)
        if absorbed:
            o = jnp.einsum('blhr,rhd->blhd', jnp.einsum('bhlk,bkr->blhr', p, c_kv[:, :n_keys]), w_uv)
        else:
            o = jnp.einsum('bhlk,bkhd->blhd', p, v[:, :n_keys])
        outs.append(o)
    return jnp.concatenate(outs, axis=1)


def mlstm_chunkwise(q, k, v, i_pre, log_f, C0, n0, m0, blk):
    dt = q.dtype
    xs = tuple(to_chunks(a, blk) for a in (q, k, v, i_pre, log_f))
    causal = jnp.tril(jnp.ones((blk, blk), dtype=bool))

    def step(carry, xs_c):
        C, n, m = carry
        qc, kc, vc, ic, fc = xs_c
        b = jnp.cumsum(fc, axis=-1)
        dmat = jnp.where(causal, b[..., :, None] - b[..., None, :] + ic[..., None, :], -jnp.inf)
        inter = b + m[..., None]
        m_t = jnp.maximum(inter, jnp.max(dmat, axis=-1))
        w_inter = jnp.exp(inter - m_t)
        qk = jnp.einsum('bhtd,bhsd->bhts', qc, kc) * jnp.exp(dmat - m_t[..., None])
        num = w_inter[..., None] * jnp.einsum('bhed,bhtd->bhte', C, qc) + jnp.einsum('bhts,bhse->bhte', qk, vc)
        den = w_inter * jnp.einsum('bhd,bhtd->bht', n, qc) + jnp.sum(qk, axis=-1)
        h = num / jnp.maximum(jnp.abs(den), jnp.exp(-m_t))[..., None]
        m_new = m_t[..., -1]
        g_state = jnp.exp(inter[..., -1] - m_new)
        g_tok = jnp.exp(b[..., -1:] - b + ic - m_new[..., None])
        C_new = g_state[..., None, None] * C + jnp.einsum('bhs,bhse,bhsd->bhed', g_tok, vc, kc)
        n_new = g_state[..., None] * n + jnp.einsum('bhs,bhsd->bhd', g_tok, kc)
        return (C_new, n_new, m_new), h

    carry0 = (C0.astype(jnp.float32), n0.astype(jnp.float32), m0.astype(jnp.float32))
    (C, n, m), h = lax.scan(step, carry0, xs)
    return from_chunks(h).astype(dt), C.astype(dt), n.astype(dt), m.astype(dt)


def gdn_chunkwise(q, k, v, log_a, beta, S0, blk):
    dt = q.dtype
    xs = tuple(to_chunks(a, blk) for a in (q, k, v, log_a, beta))
    incl = jnp.tril(jnp.ones((blk, blk), dtype=bool))
    strict = jnp.tril(jnp.ones((blk, blk), dtype=bool), -1)
    eye = jnp.eye(blk, dtype=jnp.float32)

    def step(S, xs_c):
        qc, kc, vc, gc, bc = xs_c
        G = jnp.cumsum(gc, axis=-1)
        gam = jnp.exp(jnp.where(incl, G[..., :, None] - G[..., None, :], -jnp.inf))
        A = jnp.where(strict, bc[..., :, None] * jnp.einsum('bhtd,bhsd->bhts', kc, kc) * gam, 0.0)
        rhs = jnp.concatenate([bc[..., None] * vc, (bc * jnp.exp(G))[..., None] * kc], axis=-1)
        sol = lax.linalg.triangular_solve(A + eye, rhs, left_side=True, lower=True, unit_diagonal=True)
        u, w = sol[..., :G_DV], sol[..., G_DV:]
        delta = u - jnp.einsum('bhtk,bhkv->bhtv', w, S)
        o = (jnp.einsum('bhtk,bhkv->bhtv', qc * jnp.exp(G)[..., None], S)
             + jnp.einsum('bhts,bhsv->bhtv', jnp.einsum('bhtd,bhsd->bhts', qc, kc) * gam, delta))
        S = (jnp.exp(G[..., -1])[..., None, None] * S
             + jnp.einsum('bhsk,bhsv->bhkv', kc * jnp.exp(G[..., -1:] - G)[..., None], delta))
        return S, o

    S, o = lax.scan(step, S0.astype(jnp.float32), xs)
    return from_chunks(o).astype(dt), S.astype(dt)


def mixer_layer(x, pos0, prefix_c, prefix_kr, m_C, m_n, m_m, g_S, g_buf, absorbed,
                norm_w, w_in, q_norm, w_uq, kv_norm, w_uk, w_uv, m_gate_b, m_norm,
                g_conv_w, g_a_log, g_dt_bias, g_norm, w_out):
    B, L, _ = x.shape
    blk = min(CHUNK, L)
    proj = rms_norm(x, norm_w) @ w_in
    (c_q, c_kv, k_r, z_a, m_q, m_k, m_v, m_i, m_f, m_o, z_m,
     g_qkv, g_a, g_b, z_g) = jnp.split(proj, np.cumsum(IN_SPLITS)[:-1].tolist(), axis=-1)

    pos = pos0 + jnp.arange(L)
    q = (rms_norm(c_q, q_norm) @ w_uq).reshape(B, L, MLA_HEADS, MLA_NOPE + MLA_ROPE)
    q_nope, q_rope = q[..., :MLA_NOPE], apply_rope(q[..., MLA_NOPE:], pos)
    c_new = rms_norm(c_kv, kv_norm)
    kr_new = apply_rope(k_r, pos)
    if prefix_c is None:
        c_all, kr_all, n_prefix = c_new, kr_new, 0
    else:
        c_all = jnp.concatenate([prefix_c.astype(c_new.dtype), c_new], axis=1)
        kr_all = jnp.concatenate([prefix_kr.astype(kr_new.dtype), kr_new], axis=1)
        n_prefix = prefix_c.shape[1]
    o_a = mla_attention(q_nope, q_rope, c_all, kr_all, w_uk, w_uv, n_prefix, absorbed).reshape(B, L, MLA_WIDTH)

    hd = (B, L, M_HEADS, M_DH)
    h_m, m_C, m_n, m_m = mlstm_chunkwise(
        m_q.reshape(hd), m_k.reshape(hd) * (M_DH ** -0.5), m_v.reshape(hd),
        m_i + m_gate_b[0], jax.nn.log_sigmoid(m_f + m_gate_b[1]), m_C, m_n, m_m, blk)
    h_m = jax.nn.sigmoid(m_o).reshape(hd) * h_m
    o_m = rms_norm(h_m, m_norm.reshape(M_HEADS, M_DH)).reshape(B, L, M_WIDTH)

    xc = jnp.concatenate([g_buf.astype(g_qkv.dtype), g_qkv], axis=1)
    conv = xc[:, 0:L] * g_conv_w[0]
    for j in range(1, CONV_W):
        conv = conv + xc[:, j:j + L] * g_conv_w[j]
    g_buf = xc[:, L:]
    conv = jax.nn.silu(conv)
    g_q, g_k, g_v = jnp.split(conv, [G_HEADS * G_DK, 2 * G_HEADS * G_DK], axis=-1)
    g_q = l2_norm(g_q.reshape(B, L, G_HEADS, G_DK)) * (G_DK ** -0.5)
    g_k = l2_norm(g_k.reshape(B, L, G_HEADS, G_DK))
    log_decay = -jnp.exp(g_a_log) * jax.nn.softplus(g_a + g_dt_bias)
    h_g, g_S = gdn_chunkwise(g_q, g_k, g_v.reshape(B, L, G_HEADS, G_DV), log_decay,
                             jax.nn.sigmoid(g_b), g_S, blk)
    o_g = rms_norm(h_g, g_norm).reshape(B, L, G_WIDTH)

    mixed = jnp.concatenate([o_a * jax.nn.silu(z_a), o_m * jax.nn.silu(z_m), o_g * jax.nn.silu(z_g)], axis=-1)
    return x + mixed @ w_out, c_new, kr_new, m_C, m_n, m_m, g_S, g_buf


def stack_layers(rows, i):
    return jnp.stack([r[i] for r in rows])


def setup_inputs(seed: int = 0) -> dict:
    key = jax.random.key(seed)
    ks = jax.random.split(key, 26)
    f32 = jnp.float32

    def nrm(k, shape, scale=1.0):
        return scale * jax.random.normal(k, shape, f32)

    def gain(k, shape):
        return 1.0 + 0.02 * jax.random.normal(k, shape, f32)

    n_cache = N_META + PAST_LEN
    dt_init = jnp.exp(jax.random.uniform(ks[20], (DEPTH, G_HEADS), f32, math.log(1e-3), math.log(1e-1)))
    gate_b = jnp.stack([nrm(ks[17], (DEPTH, M_HEADS), 0.1),
                        jnp.linspace(3.0, 6.0, M_HEADS, dtype=f32)[None, :] + nrm(ks[18], (DEPTH, M_HEADS), 0.1)],
                       axis=1)
    return {
        'x_prompt': nrm(ks[0], (BATCH, SEQ, D_MODEL)),
        'x_sample': nrm(ks[1], (DEC_BATCH, DEC_SEQ, D_MODEL)),
        'cache_mla_latent': nrm(ks[2], (DEPTH, DEC_BATCH, n_cache, MLA_KV_LORA)),
        'cache_mla_krope': nrm(ks[3], (DEPTH, DEC_BATCH, n_cache, MLA_ROPE)),
        'state_mlstm_C': nrm(ks[4], (DEPTH, DEC_BATCH, M_HEADS, M_DH, M_DH), 0.5),
        'state_mlstm_n': nrm(ks[5], (DEPTH, DEC_BATCH, M_HEADS, M_DH), 0.5),
        'state_mlstm_m': nrm(ks[6], (DEPTH, DEC_BATCH, M_HEADS)),
        'state_gdn_S': nrm(ks[7], (DEPTH, DEC_BATCH, G_HEADS, G_DK, G_DV), 0.5),
        'state_gdn_conv': nrm(ks[8], (DEPTH, DEC_BATCH, CONV_W - 1, G_QKV)),
        'meta_tokens': nrm(ks[9], (N_META, D_MODEL)),
        'norm_w': gain(ks[10], (DEPTH, D_MODEL)),
        'w_in': nrm(ks[11], (DEPTH, D_MODEL, IN_COLS), D_MODEL ** -0.5),
        'mla_q_norm': gain(ks[12], (DEPTH, MLA_Q_LORA)),
        'mla_w_uq': nrm(ks[13], (DEPTH, MLA_Q_LORA, MLA_HEADS * (MLA_NOPE + MLA_ROPE)), MLA_Q_LORA ** -0.5),
        'mla_kv_norm': gain(ks[14], (DEPTH, MLA_KV_LORA)),
        'mla_w_uk': nrm(ks[15], (DEPTH, MLA_KV_LORA, MLA_HEADS, MLA_NOPE), MLA_KV_LORA ** -0.5),
        'mla_w_uv': nrm(ks[16], (DEPTH, MLA_KV_LORA, MLA_HEADS, MLA_V), MLA_KV_LORA ** -0.5),
        'mlstm_gate_b': gate_b,
        'mlstm_norm': gain(ks[19], (DEPTH, M_WIDTH)),
        'gdn_conv_w': nrm(ks[21], (DEPTH, CONV_W, G_QKV), CONV_W ** -0.5),
        'gdn_a_log': jnp.log(jax.random.uniform(ks[22], (DEPTH, G_HEADS), f32, 1.0, 16.0)),
        'gdn_dt_bias': dt_init + jnp.log(-jnp.expm1(-dt_init)),
        'gdn_norm': gain(ks[23], (DEPTH, G_DV)),
        'w_out': nrm(ks[24], (DEPTH, D_MIX, D_MODEL), D_MIX ** -0.5),
        'final_norm': gain(ks[25], (D_MODEL,)),
    }


def reference(x_prompt, x_sample, cache_mla_latent, cache_mla_krope, state_mlstm_C, state_mlstm_n,
              state_mlstm_m, state_gdn_S, state_gdn_conv, meta_tokens, norm_w, w_in, mla_q_norm,
              mla_w_uq, mla_kv_norm, mla_w_uk, mla_w_uv, mlstm_gate_b, mlstm_norm, gdn_conv_w,
              gdn_a_log, gdn_dt_bias, gdn_norm, w_out, final_norm):
    B = x_prompt.shape[0]
    dt = x_prompt.dtype
    sample_pos0 = cache_mla_latent.shape[2]
    h_meta = jnp.broadcast_to(meta_tokens.astype(dt)[None], (B, N_META, D_MODEL))
    h_p, h_s = x_prompt, x_sample
    z_C = jnp.zeros((B, M_HEADS, M_DH, M_DH), dt)
    z_n = jnp.zeros((B, M_HEADS, M_DH), dt)
    z_m = jnp.zeros((B, M_HEADS), dt)
    z_S = jnp.zeros((B, G_HEADS, G_DK, G_DV), dt)
    z_buf = jnp.zeros((B, CONV_W - 1, G_QKV), dt)
    p_rows, s_rows = [], []
    for l in range(DEPTH):
        w = tuple(p[l] for p in (norm_w, w_in, mla_q_norm, mla_w_uq, mla_kv_norm, mla_w_uk, mla_w_uv,
                                 mlstm_gate_b, mlstm_norm, gdn_conv_w, gdn_a_log, gdn_dt_bias, gdn_norm, w_out))
        h_meta, c_m, kr_m, mC, mn, mm, gS, gbuf = mixer_layer(
            h_meta, 0, None, None, z_C, z_n, z_m, z_S, z_buf, False, *w)
        h_p, c_p, kr_p, mC, mn, mm, gS, gbuf = mixer_layer(
            h_p, N_META, c_m, kr_m, mC, mn, mm, gS, gbuf, False, *w)
        p_rows.append((jnp.concatenate([c_m, c_p], axis=1), jnp.concatenate([kr_m, kr_p], axis=1),
                       mC, mn, mm, gS, gbuf))
        h_s, c_s, kr_s, sC, sn, sm, sS, sbuf = mixer_layer(
            h_s, sample_pos0, cache_mla_latent[l], cache_mla_krope[l], state_mlstm_C[l], state_mlstm_n[l],
            state_mlstm_m[l], state_gdn_S[l], state_gdn_conv[l], True, *w)
        s_rows.append((c_s, kr_s, sC, sn, sm, sS, sbuf))
    y_prompt = rms_norm(h_p, final_norm)
    y_sample = rms_norm(h_s, final_norm)
    p_mla_latent = stack_layers(p_rows, 0)
    p_mla_krope = stack_layers(p_rows, 1)
    p_mlstm_C = stack_layers(p_rows, 2)
    p_mlstm_n = stack_layers(p_rows, 3)
    p_mlstm_m = stack_layers(p_rows, 4)
    p_gdn_S = stack_layers(p_rows, 5)
    p_gdn_conv = stack_layers(p_rows, 6)
    s_mla_latent = stack_layers(s_rows, 0)
    s_mla_krope = stack_layers(s_rows, 1)
    s_mlstm_C = stack_layers(s_rows, 2)
    s_mlstm_n = stack_layers(s_rows, 3)
    s_mlstm_m = stack_layers(s_rows, 4)
    s_gdn_S = stack_layers(s_rows, 5)
    s_gdn_conv = stack_layers(s_rows, 6)
    return (y_prompt, y_sample, p_mla_latent, p_mla_krope, p_mlstm_C, p_mlstm_n, p_mlstm_m, p_gdn_S, p_gdn_conv,
            s_mla_latent, s_mla_krope, s_mlstm_C, s_mlstm_n, s_mlstm_m, s_gdn_S, s_gdn_conv)
```

```python
import functools
import math

import jax
import jax.numpy as jnp
import numpy as np
from jax import lax
from jax.experimental import pallas as pl
from jax.experimental.pallas import tpu as pltpu

F32 = jnp.float32
BF16 = jnp.bfloat16

EPS = 1e-6
ROPE_BASE = 10000.0
CHUNK = 64
N_HEADS_MLA = 8
MLA_NOPE, MLA_ROPE, MLA_V = 64, 32, 64
MLA_Q_LORA, MLA_KV_LORA = 384, 256
R_HEADS, R_DH = 4, 128
R_WIDTH = R_HEADS * R_DH
CONV_W = 4
LANES = 128
HEAD_PAD = 128

OFF_GQKV = 0
OFF_MQKV = 1536
OFF_MOZ = 3072
OFF_ZA = 4096
OFF_ZG = 4608
OFF_CKV = 5120
OFF_CQ = 5376
OFF_SMALL = 5760
NP = 5888
SM_KR, SM_MI, SM_MF, SM_GA, SM_GB = 0, 32, 36, 40, 44

VMEM_LIMIT = 56 * 1024 * 1024


def _cparams(sem):
    return pltpu.CompilerParams(dimension_semantics=sem, vmem_limit_bytes=VMEM_LIMIT)


def _bf(x):
    return x.astype(BF16)


def _dot(a, b):
    return jnp.dot(a, b, preferred_element_type=F32)


def _dot_nt(a, b):
    return lax.dot_general(a, b, (((1,), (1,)), ((), ())), preferred_element_type=F32)


def _dot_tn(a, b):
    return lax.dot_general(a, b, (((0,), (0,)), ((), ())), preferred_element_type=F32)


def _split(x):
    hi = x.astype(BF16)
    lo = (x - hi.astype(F32)).astype(BF16)
    return hi, lo


def _dot3(a, b):
    ah, al = _split(a)
    bh, bl = _split(b)
    return _dot(ah, bh) + (_dot(ah, bl) + _dot(al, bh))


def _rms(x, w):
    return x * lax.rsqrt(jnp.mean(x * x, axis=-1, keepdims=True) + EPS) * w


def _silu(x):
    return x * jax.nn.sigmoid(x)


def _rope128(x, tc, ts1, ts2):
    return x * tc + pltpu.roll(x, 16, 1) * ts1 + pltpu.roll(x, LANES - 16, 1) * ts2


def _inproj_kernel(x_ref, nw_ref, w_ref, o_ref):
    xn = _bf(_rms(x_ref[...], nw_ref[...]))
    n0 = 0
    while n0 < NP:
        n1 = min(n0 + 512, NP)
        o_ref[:, n0:n1] = _dot(xn, w_ref[:, n0:n1])
        n0 = n1


def _inproj(x2d, norm_w, w_bf):
    rows, d = x2d.shape
    tm = min(rows, 512)
    return pl.pallas_call(
        _inproj_kernel,
        grid=(rows // tm,),
        in_specs=[
            pl.BlockSpec((tm, d), lambda i: (i, 0)),
            pl.BlockSpec((1, d), lambda i: (0, 0)),
            pl.BlockSpec((d, NP), lambda i: (0, 0), pipeline_mode=pl.Buffered(1)),
        ],
        out_specs=pl.BlockSpec((tm, NP), lambda i: (i, 0)),
        out_shape=jax.ShapeDtypeStruct((rows, NP), F32),
        compiler_params=_cparams(("arbitrary",)),
        name="inproj",
    )(x2d, norm_w.reshape(1, d), w_bf)


def _mla_prep_kernel(cq_ref, ckv_ref, sm_ref, tc_ref, ts1_ref, ts2_ref, qn_ref, wuq_ref, kvn_ref,
                     wuk_ref, wuv_ref, q_out, k_out, v_out, cn_out, kr_out):
    tc, ts1, ts2 = tc_ref[...], ts1_ref[...], ts2_ref[...]
    q = _dot(_bf(_rms(cq_ref[0], qn_ref[...])), wuq_ref[...])
    cn = _rms(ckv_ref[0], kvn_ref[...])
    cn_out[0] = cn
    cnb = _bf(cn)
    kn = _dot(cnb, wuk_ref[...])
    v_out[0] = _bf(_dot(cnb, wuv_ref[...]))
    sm = sm_ref[0]
    lane = lax.broadcasted_iota(jnp.int32, sm.shape, 1)
    kr = pltpu.roll(jnp.where(lane < MLA_ROPE, sm, 0.0), MLA_NOPE, 1)
    kr = _rope128(kr, tc, ts1, ts2)
    kr_out[0] = pltpu.roll(kr, LANES - MLA_NOPE, 1)[:, :MLA_ROPE]
    for h in range(N_HEADS_MLA):
        sl = slice(h * HEAD_PAD, (h + 1) * HEAD_PAD)
        q_out[0, :, sl] = _bf(_rope128(q[:, sl], tc, ts1, ts2))
        k_out[0, :, sl] = _bf(kn[:, sl] + kr)


def _mla_prep(proj3, tabs, lw, tr):
    nb, seq, _ = proj3.shape
    hq = N_HEADS_MLA * HEAD_PAD
    hv = N_HEADS_MLA * MLA_V
    const = lambda i, b: (0, 0)
    return pl.pallas_call(
        _mla_prep_kernel,
        grid=(seq // tr, nb),
        in_specs=[
            pl.BlockSpec((1, tr, MLA_Q_LORA), lambda i, b: (b, i, OFF_CQ // MLA_Q_LORA)),
            pl.BlockSpec((1, tr, MLA_KV_LORA), lambda i, b: (b, i, OFF_CKV // MLA_KV_LORA)),
            pl.BlockSpec((1, tr, LANES), lambda i, b: (b, i, OFF_SMALL // LANES)),
            pl.BlockSpec((tr, LANES), lambda i, b: (i, 0)),
            pl.BlockSpec((tr, LANES), lambda i, b: (i, 0)),
            pl.BlockSpec((tr, LANES), lambda i, b: (i, 0)),
            pl.BlockSpec((1, MLA_Q_LORA), const),
            pl.BlockSpec((MLA_Q_LORA, hq), const),
            pl.BlockSpec((1, MLA_KV_LORA), const),
            pl.BlockSpec((MLA_KV_LORA, hq), const),
            pl.BlockSpec((MLA_KV_LORA, hv), const),
        ],
        out_specs=[
            pl.BlockSpec((1, tr, hq), lambda i, b: (b, i, 0)),
            pl.BlockSpec((1, tr, hq), lambda i, b: (b, i, 0)),
            pl.BlockSpec((1, tr, hv), lambda i, b: (b, i, 0)),
            pl.BlockSpec((1, tr, MLA_KV_LORA), lambda i, b: (b, i, 0)),
            pl.BlockSpec((1, tr, MLA_ROPE), lambda i, b: (b, i, 0)),
        ],
        out_shape=[
            jax.ShapeDtypeStruct((nb, seq, hq), BF16),
            jax.ShapeDtypeStruct((nb, seq, hq), BF16),
            jax.ShapeDtypeStruct((nb, seq, hv), BF16),
            jax.ShapeDtypeStruct((nb, seq, MLA_KV_LORA), F32),
            jax.ShapeDtypeStruct((nb, seq, MLA_ROPE), F32),
        ],
        compiler_params=_cparams(("arbitrary", "arbitrary")),
        name="mla_prep",
    )(proj3, proj3, proj3, *tabs, lw["q_norm"], lw["w_uq"], lw["kv_norm"], lw["w_uk"], lw["w_uv"])


def _softmax_step(carry, s, vt):
    m, l, acc = carry
    m_new = jnp.maximum(m, jnp.max(s, axis=1, keepdims=True))
    alpha = jnp.exp(m - m_new)
    p = jnp.exp(s - m_new)
    l = alpha * l + jnp.sum(p, axis=1, keepdims=True)
    acc = alpha * acc + _dot(_bf(p), vt)
    return m_new, l, acc


def _attn_kernel(*refs, tq, tk, has_prefix, scale):
    if has_prefix:
        q_ref, k_ref, v_ref, kp_ref, vp_ref, o_ref = refs
    else:
        q_ref, k_ref, v_ref, o_ref = refs
    i = pl.program_id(1)
    q0 = i * tq
    nfull = q0 // tk
    lane = lax.broadcasted_iota(jnp.int32, (tq, LANES), 1)
    q_chunk = (q0 + lax.broadcasted_iota(jnp.int32, (tq, tk), 0)) // CHUNK
    k_chunk = (nfull * tk + lax.broadcasted_iota(jnp.int32, (tq, tk), 1)) // CHUNK
    visible = k_chunk <= q_chunk
    for hp in range(N_HEADS_MLA // 2):
        vsl = slice(hp * LANES, (hp + 1) * LANES)
        res = []
        for a in range(2):
            hsl = slice((2 * hp + a) * HEAD_PAD, (2 * hp + a + 1) * HEAD_PAD)
            qh = q_ref[0, :, hsl]
            carry = (jnp.full((tq, 1), -jnp.inf, F32), jnp.zeros((tq, 1), F32),
                     jnp.zeros((tq, LANES), F32))
            if has_prefix:
                s = _dot_nt(qh, kp_ref[0, :, hsl]) * scale
                carry = _softmax_step(carry, s, vp_ref[0, :, vsl])

            def body(j, c, qh=qh, hsl=hsl, vsl=vsl):
                r0 = pl.multiple_of(j * tk, tk)
                s = _dot_nt(qh, k_ref[0, pl.ds(r0, tk), hsl]) * scale
                return _softmax_step(c, s, v_ref[0, pl.ds(r0, tk), vsl])

            carry = lax.fori_loop(0, nfull, body, carry)
            r0 = pl.multiple_of(nfull * tk, tk)
            s = _dot_nt(qh, k_ref[0, pl.ds(r0, tk), hsl]) * scale
            s = jnp.where(visible, s, -jnp.inf)
            m, l, acc = _softmax_step(carry, s, v_ref[0, pl.ds(r0, tk), vsl])
            res.append(acc / l)
        o_ref[0, :, vsl] = jnp.where(lane < MLA_V, res[0], res[1])


def _attention(q, k, v, prefix, tq, tk):
    nb, seq, hq = q.shape
    hv = v.shape[-1]
    scale = 1.0 / math.sqrt(MLA_NOPE + MLA_ROPE)
    in_specs = [
        pl.BlockSpec((1, tq, hq), lambda b, i: (b, i, 0)),
        pl.BlockSpec((1, seq, hq), lambda b, i: (b, 0, 0)),
        pl.BlockSpec((1, seq, hv), lambda b, i: (b, 0, 0)),
    ]
    args = [q, k, v]
    if prefix is not None:
        kp, vp = prefix
        npre = kp.shape[1]
        in_specs += [pl.BlockSpec((1, npre, hq), lambda b, i: (0, 0, 0)),
                     pl.BlockSpec((1, npre, hv), lambda b, i: (0, 0, 0))]
        args += [kp, vp]
    return pl.pallas_call(
        functools.partial(_attn_kernel, tq=tq, tk=tk, has_prefix=prefix is not None, scale=scale),
        grid=(nb, seq // tq),
        in_specs=in_specs,
        out_specs=pl.BlockSpec((1, tq, hv), lambda b, i: (b, i, 0)),
        out_shape=jax.ShapeDtypeStruct((nb, seq, hv), F32),
        compiler_params=_cparams(("arbitrary", "arbitrary")),
        name="attn",
    )(*args)


def _sample_attn_kernel(cq_ref, ckv_ref, sm_ref, cc_ref, kc_ref, tc_ref, ts1_ref, ts2_ref, qn_ref,
                        wuq_ref, kvn_ref, wuk_ref, wuv_ref, o_ref, cn_out, kr_out, *, n_cache, tk, scale):
    tc, ts1, ts2 = tc_ref[...], ts1_ref[...], ts2_ref[...]
    t = cq_ref.shape[1]
    q = _dot(_bf(_rms(cq_ref[0], qn_ref[...])), wuq_ref[...])
    cn = _rms(ckv_ref[0], kvn_ref[...])
    cn_out[0] = cn
    sm = sm_ref[0]
    lane = lax.broadcasted_iota(jnp.int32, sm.shape, 1)
    kr = pltpu.roll(jnp.where(lane < MLA_ROPE, sm, 0.0), MLA_NOPE, 1)
    kr = _rope128(kr, tc, ts1, ts2)
    krn = pltpu.roll(kr, LANES - MLA_NOPE, 1)[:, :MLA_ROPE]
    kr_out[0] = krn
    ql, qr = [], []
    for h in range(N_HEADS_MLA):
        sl = slice(h * HEAD_PAD, (h + 1) * HEAD_PAD)
        qh = _rope128(q[:, sl], tc, ts1, ts2)
        ql.append(_dot_nt(_bf(qh), wuk_ref[:, sl]))
        qr.append(pltpu.roll(qh, LANES - MLA_NOPE, 1)[:, :MLA_ROPE])
    qlat = _bf(jnp.concatenate(ql, axis=0))
    qrope = _bf(jnp.concatenate(qr, axis=0))

    def step(carry, ct, kt):
        ctb = _bf(ct)
        s = (_dot_nt(qlat, ctb) + _dot_nt(qrope, _bf(kt))) * scale
        return _softmax_step(carry, s, ctb)

    rows = N_HEADS_MLA * t
    carry = (jnp.full((rows, 1), -jnp.inf, F32), jnp.zeros((rows, 1), F32),
             jnp.zeros((rows, MLA_KV_LORA), F32))
    nfull = n_cache // tk

    def body(j, c):
        r0 = pl.multiple_of(j * tk, tk)
        return step(c, cc_ref[0, pl.ds(r0, tk), :], kc_ref[0, pl.ds(r0, tk), :])

    carry = lax.fori_loop(0, nfull, body, carry)
    if nfull * tk < n_cache:
        carry = step(carry, cc_ref[0, nfull * tk:n_cache, :], kc_ref[0, nfull * tk:n_cache, :])
    m, l, acc = step(carry, cn, krn)
    olat = _bf(acc / l)
    out_lane = lax.broadcasted_iota(jnp.int32, (t, N_HEADS_MLA * MLA_V), 1) // MLA_V
    o = jnp.zeros((t, N_HEADS_MLA * MLA_V), F32)
    for h in range(N_HEADS_MLA):
        o = jnp.where(out_lane == h, _dot(olat[h * t:(h + 1) * t], wuv_ref[...]), o)
    o_ref[0] = o


def _sample_attention(proj3, cache_c, cache_kr, tabs, lw):
    nb, t, _ = proj3.shape
    n_cache = cache_c.shape[1]
    hq = N_HEADS_MLA * HEAD_PAD
    hv = N_HEADS_MLA * MLA_V
    scale = 1.0 / math.sqrt(MLA_NOPE + MLA_ROPE)
    const = lambda b: (0, 0)
    return pl.pallas_call(
        functools.partial(_sample_attn_kernel, n_cache=n_cache, tk=min(512, n_cache), scale=scale),
        grid=(nb,),
        in_specs=[
            pl.BlockSpec((1, t, MLA_Q_LORA), lambda b: (b, 0, OFF_CQ // MLA_Q_LORA)),
            pl.BlockSpec((1, t, MLA_KV_LORA), lambda b: (b, 0, OFF_CKV // MLA_KV_LORA)),
            pl.BlockSpec((1, t, LANES), lambda b: (b, 0, OFF_SMALL // LANES)),
            pl.BlockSpec((1, n_cache, MLA_KV_LORA), lambda b: (b, 0, 0)),
            pl.BlockSpec((1, n_cache, MLA_ROPE), lambda b: (b, 0, 0)),
            pl.BlockSpec((t, LANES), const),
            pl.BlockSpec((t, LANES), const),
            pl.BlockSpec((t, LANES), const),
            pl.BlockSpec((1, MLA_Q_LORA), const),
            pl.BlockSpec((MLA_Q_LORA, hq), const),
            pl.BlockSpec((1, MLA_KV_LORA), const),
            pl.BlockSpec((MLA_KV_LORA, hq), const),
            pl.BlockSpec((MLA_KV_LORA, hv), const),
        ],
        out_specs=[
            pl.BlockSpec((1, t, hv), lambda b: (b, 0, 0)),
            pl.BlockSpec((1, t, MLA_KV_LORA), lambda b: (b, 0, 0)),
            pl.BlockSpec((1, t, MLA_ROPE), lambda b: (b, 0, 0)),
        ],
        out_shape=[
            jax.ShapeDtypeStruct((nb, t, hv), F32),
            jax.ShapeDtypeStruct((nb, t, MLA_KV_LORA), F32),
            jax.ShapeDtypeStruct((nb, t, MLA_ROPE), F32),
        ],
        compiler_params=_cparams(("arbitrary",)),
        name="sample_attn",
    )(proj3, proj3, proj3, cache_c, cache_kr, *tabs, lw["q_norm"], lw["w_uq"], lw["kv_norm"],
      lw["w_uk"], lw["w_uv"])


def _chunk_masks(t):
    row = lax.broadcasted_iota(jnp.int32, (t, t), 0)
    col = lax.broadcasted_iota(jnp.int32, (t, t), 1)
    return row == col, col <= row, row <= col, col < row


def _to_row(col_vec, eye):
    return jnp.sum(jnp.where(eye, col_vec, 0.0), axis=0, keepdims=True)


def _cumsum_col_row(col_vec, eye, tril, triu):
    row_vec = _to_row(col_vec, eye)
    c = jnp.sum(jnp.where(tril, row_vec, 0.0), axis=1, keepdims=True)
    r = jnp.sum(jnp.where(triu, col_vec, 0.0), axis=0, keepdims=True)
    return c, r


def _mlstm_kernel(qkv_ref, oz_ref, sm_ref, gb_ref, nrm_ref, c0_ref, n0_ref, m0_ref,
                  out_ref, c_ref, n_ref, m_ref, *, bb, t, shared_init):
    @pl.when(pl.program_id(1) == 0)
    def _():
        for b in range(bb):
            src = 0 if shared_init else b
            c_ref[b] = c0_ref[src]
            n_ref[b] = n0_ref[src]
            m_ref[0, b:b + 1, :] = m0_ref[0, src:src + 1, :]

    eye, tril, triu, _ = _chunk_masks(t)
    for b in range(bb):
        for h in range(R_HEADS):
            hs = slice(h * R_DH, (h + 1) * R_DH)
            q = qkv_ref[b, :, hs]
            k = qkv_ref[b, :, R_WIDTH + h * R_DH:R_WIDTH + (h + 1) * R_DH] * (R_DH ** -0.5)
            v = qkv_ref[b, :, 2 * R_WIDTH + h * R_DH:2 * R_WIDTH + (h + 1) * R_DH]
            i_col = sm_ref[b, :, SM_MI + h:SM_MI + h + 1] + gb_ref[0:1, h:h + 1]
            f_col = jax.nn.log_sigmoid(sm_ref[b, :, SM_MF + h:SM_MF + h + 1] + gb_ref[1:2, h:h + 1])
            i_row = _to_row(i_col, eye)
            b_col, b_row = _cumsum_col_row(f_col, eye, tril, triu)
            dmat = jnp.where(tril, b_col - b_row + i_row, -jnp.inf)
            m_prev = m_ref[0, b:b + 1, h:h + 1]
            inter = b_col + m_prev
            m_t = jnp.maximum(inter, jnp.max(dmat, axis=1, keepdims=True))
            w_inter = jnp.exp(inter - m_t)
            qb, kb, vb = _bf(q), _bf(k), _bf(v)
            qk = _dot_nt(qb, kb) * jnp.exp(dmat - m_t)
            cst = c_ref[b, h]
            nvec = n_ref[b, h:h + 1, :]
            num = w_inter * _dot_nt(qb, _bf(cst)) + _dot(_bf(qk), vb)
            den = (w_inter * jnp.sum(q * nvec, axis=1, keepdims=True)
                   + jnp.sum(qk, axis=1, keepdims=True))
            hh = num / jnp.maximum(jnp.abs(den), jnp.exp(-m_t))
            m_new = m_t[t - 1:t, :]
            b_last = b_col[t - 1:t, :]
            g_state = jnp.exp(b_last + m_prev - m_new)
            g_tok = jnp.exp(b_last - b_col + i_col - m_new)
            c_ref[b, h] = g_state * cst + _dot_tn(_bf(g_tok * v), kb)
            n_ref[b, h:h + 1, :] = g_state * nvec + jnp.sum(g_tok * k, axis=0, keepdims=True)
            m_ref[0, b:b + 1, h:h + 1] = m_new
            hm = jax.nn.sigmoid(oz_ref[b, :, hs]) * hh
            om = _rms(hm, nrm_ref[0:1, hs])
            out_ref[b, :, hs] = om * _silu(oz_ref[b, :, R_WIDTH + h * R_DH:R_WIDTH + (h + 1) * R_DH])


def _mlstm(proj3, lw, state, bb, t, shared_init):
    nb, seq, _ = proj3.shape
    c0, n0, m0 = state
    sb = 1 if shared_init else bb
    m0 = m0.reshape(-1, sb, R_HEADS)
    st = (lambda i, c: (0, 0, 0, 0)) if shared_init else (lambda i, c: (i, 0, 0, 0))
    st3 = (lambda i, c: (0, 0, 0)) if shared_init else (lambda i, c: (i, 0, 0))
    out, c1, n1, m1 = pl.pallas_call(
        functools.partial(_mlstm_kernel, bb=bb, t=t, shared_init=shared_init),
        grid=(nb // bb, seq // t),
        in_specs=[
            pl.BlockSpec((bb, t, 3 * R_WIDTH), lambda i, c: (i, c, OFF_MQKV // (3 * R_WIDTH))),
            pl.BlockSpec((bb, t, 2 * R_WIDTH), lambda i, c: (i, c, OFF_MOZ // (2 * R_WIDTH))),
            pl.BlockSpec((bb, t, LANES), lambda i, c: (i, c, OFF_SMALL // LANES)),
            pl.BlockSpec((2, R_HEADS), lambda i, c: (0, 0)),
            pl.BlockSpec((1, R_WIDTH), lambda i, c: (0, 0)),
            pl.BlockSpec((sb, R_HEADS, R_DH, R_DH), st),
            pl.BlockSpec((sb, R_HEADS, R_DH), st3),
            pl.BlockSpec((1, sb, R_HEADS), st3),
        ],
        out_specs=[
            pl.BlockSpec((bb, t, R_WIDTH), lambda i, c: (i, c, 0)),
            pl.BlockSpec((bb, R_HEADS, R_DH, R_DH), lambda i, c: (i, 0, 0, 0)),
            pl.BlockSpec((bb, R_HEADS, R_DH), lambda i, c: (i, 0, 0)),
            pl.BlockSpec((1, bb, R_HEADS), lambda i, c: (i, 0, 0)),
        ],
        out_shape=[
            jax.ShapeDtypeStruct((nb, seq, R_WIDTH), F32),
            jax.ShapeDtypeStruct((nb, R_HEADS, R_DH, R_DH), F32),
            jax.ShapeDtypeStruct((nb, R_HEADS, R_DH), F32),
            jax.ShapeDtypeStruct((nb // bb, bb, R_HEADS), F32),
        ],
        compiler_params=_cparams(("arbitrary", "arbitrary")),
        name="mlstm",
    )(proj3, proj3, proj3, lw["m_gate_b"], lw["m_norm"], c0, n0, m0)
    return out, (c1, n1, m1.reshape(nb, R_HEADS))


def _neumann(a, t):
    n_acc = -a
    apow = a
    deg = 2
    while deg < t:
        apow = _dot3(apow, apow)
        n_acc = n_acc + apow + _dot3(n_acc, apow)
        deg *= 2
    return n_acc


def _gdn_kernel(x_ref, z_ref, sm_ref, cw_ref, alog_ref, dtb_ref, gn_ref, s0_ref, buf0_ref,
                out_ref, s_ref, buf_ref, xprev_ref, *, bb, t, shared_init):
    c = pl.program_id(1)

    @pl.when(c == 0)
    def _():
        for b in range(bb):
            src = 0 if shared_init else b
            s_ref[b] = s0_ref[src]
            xprev_ref[b, 0:CONV_W - 1, :] = buf0_ref[src]

    eye, tril, triu, strict = _chunk_masks(t)
    rid8 = lax.broadcasted_iota(jnp.int32, (8, 3 * R_WIDTH), 0)
    for b in range(bb):
        x = x_ref[b]
        p8 = xprev_ref[b]
        conv = x * cw_ref[CONV_W - 1:CONV_W, :]
        for j in range(1, CONV_W):
            xs = pltpu.roll(x, j, 0)
            pj = pltpu.roll(p8, (j + 5) % 8, 0)
            head = jnp.where(rid8 < j, pj, xs[0:8])
            xsj = jnp.concatenate([head, xs[8:]], axis=0)
            conv = conv + xsj * cw_ref[CONV_W - 1 - j:CONV_W - j, :]
        xprev_ref[b] = pltpu.roll(x[t - 8:t], CONV_W - 1, 0)
        buf_ref[b] = xprev_ref[b, 0:CONV_W - 1, :]
        act = _silu(conv)
        for h in range(R_HEADS):
            hs = slice(h * R_DH, (h + 1) * R_DH)
            gq = act[:, hs]
            gk = act[:, R_WIDTH + h * R_DH:R_WIDTH + (h + 1) * R_DH]
            gv = act[:, 2 * R_WIDTH + h * R_DH:2 * R_WIDTH + (h + 1) * R_DH]
            qn = gq * lax.rsqrt(jnp.sum(gq * gq, axis=1, keepdims=True) + EPS) * (R_DH ** -0.5)
            kn = gk * lax.rsqrt(jnp.sum(gk * gk, axis=1, keepdims=True) + EPS)
            beta = jax.nn.sigmoid(sm_ref[b, :, SM_GB + h:SM_GB + h + 1])
            g_col = (-jnp.exp(alog_ref[0:1, h:h + 1])
                     * jax.nn.softplus(sm_ref[b, :, SM_GA + h:SM_GA + h + 1] + dtb_ref[0:1, h:h + 1]))
            gc_col, gc_row = _cumsum_col_row(g_col, eye, tril, triu)
            gam = jnp.exp(jnp.where(tril, gc_col - gc_row, -jnp.inf))
            knb, qnb = _bf(kn), _bf(qn)
            a = jnp.where(strict, beta * _dot_nt(knb, knb) * gam, 0.0)
            n_inv = _neumann(a, t)
            eg = jnp.exp(gc_col)
            rhs = jnp.concatenate([beta * gv, (beta * eg) * kn], axis=1)
            sol = rhs + _dot3(n_inv, rhs)
            u, w = sol[:, :R_DH], sol[:, R_DH:]
            st = s_ref[b, h]
            stb = _bf(st)
            delta = u - _dot(_bf(w), stb)
            deltab = _bf(delta)
            o = _dot(_bf(qn * eg), stb) + _dot(_bf(_dot_nt(qnb, knb) * gam), deltab)
            g_last = gc_col[t - 1:t, :]
            s_ref[b, h] = jnp.exp(g_last) * st + _dot_tn(_bf(kn * jnp.exp(g_last - gc_col)), deltab)
            out_ref[b, :, hs] = _rms(o, gn_ref[...]) * _silu(z_ref[b, :, hs])


def _gdn(proj3, lw, state, bb, t, shared_init):
    nb, seq, _ = proj3.shape
    s0, buf0 = state
    sb = 1 if shared_init else bb
    st = (lambda i, c: (0, 0, 0, 0)) if shared_init else (lambda i, c: (i, 0, 0, 0))
    st3 = (lambda i, c: (0, 0, 0)) if shared_init else (lambda i, c: (i, 0, 0))
    out, s1, buf1 = pl.pallas_call(
        functools.partial(_gdn_kernel, bb=bb, t=t, shared_init=shared_init),
        grid=(nb // bb, seq // t),
        in_specs=[
            pl.BlockSpec((bb, t, 3 * R_WIDTH), lambda i, c: (i, c, OFF_GQKV // (3 * R_WIDTH))),
            pl.BlockSpec((bb, t, R_WIDTH), lambda i, c: (i, c, OFF_ZG // R_WIDTH)),
            pl.BlockSpec((bb, t, LANES), lambda i, c: (i, c, OFF_SMALL // LANES)),
            pl.BlockSpec((CONV_W, 3 * R_WIDTH), lambda i, c: (0, 0)),
            pl.BlockSpec((1, R_HEADS), lambda i, c: (0, 0)),
            pl.BlockSpec((1, R_HEADS), lambda i, c: (0, 0)),
            pl.BlockSpec((1, R_DH), lambda i, c: (0, 0)),
            pl.BlockSpec((sb, R_HEADS, R_DH, R_DH), st),
            pl.BlockSpec((sb, CONV_W - 1, 3 * R_WIDTH), st3),
        ],
        out_specs=[
            pl.BlockSpec((bb, t, R_WIDTH), lambda i, c: (i, c, 0)),
            pl.BlockSpec((bb, R_HEADS, R_DH, R_DH), lambda i, c: (i, 0, 0, 0)),
            pl.BlockSpec((bb, CONV_W - 1, 3 * R_WIDTH), lambda i, c: (i, 0, 0)),
        ],
        out_shape=[
            jax.ShapeDtypeStruct((nb, seq, R_WIDTH), F32),
            jax.ShapeDtypeStruct((nb, R_HEADS, R_DH, R_DH), F32),
            jax.ShapeDtypeStruct((nb, CONV_W - 1, 3 * R_WIDTH), F32),
        ],
        scratch_shapes=[pltpu.VMEM((bb, 8, 3 * R_WIDTH), F32)],
        compiler_params=_cparams(("arbitrary", "arbitrary")),
        name="gdn",
    )(proj3, proj3, proj3, lw["g_conv_w"], lw["g_a_log"], lw["g_dt_bias"], lw["g_norm"], s0, buf0)
    return out, (s1, buf1)


def _outproj_kernel(*refs, final):
    if final:
        oa_ref, za_ref, mm_ref, mg_ref, x_ref, w_ref, fn_ref, y_ref = refs
    else:
        oa_ref, za_ref, mm_ref, mg_ref, x_ref, w_ref, y_ref = refs
    ma = oa_ref[...] * _silu(za_ref[...])
    acc = (_dot(_bf(ma), w_ref[0:R_WIDTH, :])
           + _dot(_bf(mm_ref[...]), w_ref[R_WIDTH:2 * R_WIDTH, :])
           + _dot(_bf(mg_ref[...]), w_ref[2 * R_WIDTH:3 * R_WIDTH, :]))
    hnew = x_ref[...] + acc
    y_ref[...] = _rms(hnew, fn_ref[...]) if final else hnew


def _outproj(oa, proj2, mm, mg, x2d, w_bf, final_norm):
    rows, d = x2d.shape
    tm = min(rows, 512)
    final = final_norm is not None
    in_specs = [
        pl.BlockSpec((tm, R_WIDTH), lambda i: (i, 0)),
        pl.BlockSpec((tm, R_WIDTH), lambda i: (i, OFF_ZA // R_WIDTH)),
        pl.BlockSpec((tm, R_WIDTH), lambda i: (i, 0)),
        pl.BlockSpec((tm, R_WIDTH), lambda i: (i, 0)),
        pl.BlockSpec((tm, d), lambda i: (i, 0)),
        pl.BlockSpec((3 * R_WIDTH, d), lambda i: (0, 0)),
    ]
    args = [oa, proj2, mm, mg, x2d, w_bf]
    if final:
        in_specs.append(pl.BlockSpec((1, d), lambda i: (0, 0)))
        args.append(final_norm.reshape(1, d))
    return pl.pallas_call(
        functools.partial(_outproj_kernel, final=final),
        grid=(rows // tm,),
        in_specs=in_specs,
        out_specs=pl.BlockSpec((tm, d), lambda i: (i, 0)),
        out_shape=jax.ShapeDtypeStruct((rows, d), F32),
        compiler_params=_cparams(("arbitrary",)),
        name="outproj",
    )(*args)


def _permute_w_in(w):
    d = w.shape[0]
    c_q, c_kv, k_r, z_a = w[:, 0:384], w[:, 384:640], w[:, 640:672], w[:, 672:1184]
    m_qkv, m_i, m_f = w[:, 1184:2720], w[:, 2720:2724], w[:, 2724:2728]
    m_oz = w[:, 2728:3752]
    g_qkv, g_a, g_b, z_g = w[:, 3752:5288], w[:, 5288:5292], w[:, 5292:5296], w[:, 5296:5808]
    small = jnp.concatenate([k_r, m_i, m_f, g_a, g_b, jnp.zeros((d, LANES - 48), w.dtype)], axis=1)
    return jnp.concatenate([g_qkv, m_qkv, m_oz, z_a, z_g, c_kv, c_q, small], axis=1).astype(BF16)


def _rope_tables(pos0, n):
    half = MLA_ROPE // 2
    freq = ROPE_BASE ** (-jnp.arange(half, dtype=F32) / half)
    ang = (pos0 + jnp.arange(n)).astype(F32)[:, None] * freq[None, :]
    cos, sin = jnp.cos(ang), jnp.sin(ang)
    one_lo = jnp.ones((n, MLA_NOPE), F32)
    one_hi = jnp.ones((n, LANES - MLA_NOPE - MLA_ROPE), F32)
    zero_lo = jnp.zeros((n, MLA_NOPE), F32)
    zero_hi = jnp.zeros((n, LANES - MLA_NOPE - MLA_ROPE), F32)
    zero_h = jnp.zeros((n, half), F32)
    tc = jnp.concatenate([one_lo, cos, cos, one_hi], axis=1)
    ts1 = jnp.concatenate([zero_lo, zero_h, sin, zero_hi], axis=1)
    ts2 = jnp.concatenate([zero_lo, -sin, zero_h, zero_hi], axis=1)
    return tc, ts1, ts2


def _layer_weights(l, norm_w, w_in, mla_q_norm, mla_w_uq, mla_kv_norm, mla_w_uk, mla_w_uv, mlstm_gate_b,
                   mlstm_norm, gdn_conv_w, gdn_a_log, gdn_dt_bias, gdn_norm, w_out):
    pad = HEAD_PAD - (MLA_NOPE + MLA_ROPE)
    w_uq = mla_w_uq[l].reshape(MLA_Q_LORA, N_HEADS_MLA, MLA_NOPE + MLA_ROPE)
    w_uq = jnp.pad(w_uq, ((0, 0), (0, 0), (0, pad))).reshape(MLA_Q_LORA, N_HEADS_MLA * HEAD_PAD)
    w_uk = jnp.pad(mla_w_uk[l], ((0, 0), (0, 0), (0, HEAD_PAD - MLA_NOPE)))
    return {
        "norm_w": norm_w[l],
        "w_in": _permute_w_in(w_in[l]),
        "q_norm": mla_q_norm[l].reshape(1, -1),
        "w_uq": w_uq.astype(BF16),
        "kv_norm": mla_kv_norm[l].reshape(1, -1),
        "w_uk": w_uk.reshape(MLA_KV_LORA, N_HEADS_MLA * HEAD_PAD).astype(BF16),
        "w_uv": mla_w_uv[l].reshape(MLA_KV_LORA, N_HEADS_MLA * MLA_V).astype(BF16),
        "m_gate_b": mlstm_gate_b[l],
        "m_norm": mlstm_norm[l].reshape(1, -1),
        "g_conv_w": gdn_conv_w[l],
        "g_a_log": gdn_a_log[l].reshape(1, -1),
        "g_dt_bias": gdn_dt_bias[l].reshape(1, -1),
        "g_norm": gdn_norm[l].reshape(1, -1),
        "w_out": w_out[l].astype(BF16),
    }


def _recurrent_groups(proj3, lw, m_state, g_state, bb, t, shared_init):
    mm, m_state = _mlstm(proj3, lw, m_state, bb, t, shared_init)
    mg, g_state = _gdn(proj3, lw, g_state, bb, t, shared_init)
    return mm, mg, m_state, g_state


def kernel(x_prompt, x_sample, cache_mla_latent, cache_mla_krope, state_mlstm_C, state_mlstm_n, state_mlstm_m, state_gdn_S, state_gdn_conv, meta_tokens, norm_w, w_in, mla_q_norm, mla_w_uq, mla_kv_norm, mla_w_uk, mla_w_uv, mlstm_gate_b, mlstm_norm, gdn_conv_w, gdn_a_log, gdn_dt_bias, gdn_norm, w_out, final_norm):
    nb, seq, d = x_prompt.shape
    ns, dseq, _ = x_sample.shape
    n_meta = meta_tokens.shape[0]
    n_cache = cache_mla_latent.shape[2]
    depth = norm_w.shape[0]
    assert seq % 256 == 0 and n_meta % 8 == 0 and dseq % 8 == 0 and n_meta <= CHUNK and dseq <= CHUNK

    tabs_m = _rope_tables(0, n_meta)
    tabs_p = _rope_tables(n_meta, seq)
    tabs_s = _rope_tables(n_cache, dseq)

    h_m = meta_tokens.astype(F32)
    h_p = x_prompt.reshape(nb * seq, d)
    h_s = x_sample.reshape(ns * dseq, d)
    zero_m = (jnp.zeros((1, R_HEADS, R_DH, R_DH), F32), jnp.zeros((1, R_HEADS, R_DH), F32),
              jnp.zeros((1, R_HEADS), F32))
    zero_g = (jnp.zeros((1, R_HEADS, R_DH, R_DH), F32), jnp.zeros((1, CONV_W - 1, 3 * R_WIDTH), F32))
    bb_p = 2 if nb % 2 == 0 else 1
    bb_s = 4 if ns % 4 == 0 else 1
    p_rows, s_rows = [], []
    for l in range(depth):
        lw = _layer_weights(l, norm_w, w_in, mla_q_norm, mla_w_uq, mla_kv_norm, mla_w_uk, mla_w_uv,
                            mlstm_gate_b, mlstm_norm, gdn_conv_w, gdn_a_log, gdn_dt_bias, gdn_norm, w_out)
        last = l == depth - 1

        proj_m = _inproj(h_m, lw["norm_w"], lw["w_in"])
        proj_m3 = proj_m.reshape(1, n_meta, NP)
        q_m, k_m, v_m, c_m, kr_m = _mla_prep(proj_m3, tabs_m, lw, n_meta)
        oa_m = _attention(q_m, k_m, v_m, None, n_meta, n_meta)
        mm_m, mg_m, mst, gst = _recurrent_groups(proj_m3, lw, zero_m, zero_g, 1, n_meta, True)
        if not last:
            h_m = _outproj(oa_m.reshape(n_meta, -1), proj_m, mm_m.reshape(n_meta, -1),
                           mg_m.reshape(n_meta, -1), h_m, lw["w_out"], None)

        proj_p = _inproj(h_p, lw["norm_w"], lw["w_in"])
        proj_p3 = proj_p.reshape(nb, seq, NP)
        q_p, k_p, v_p, c_p, kr_p = _mla_prep(proj_p3, tabs_p, lw, 256)
        oa_p = _attention(q_p, k_p, v_p, (k_m, v_m), 128, 256)
        mm_p, mg_p, mst, gst = _recurrent_groups(proj_p3, lw, mst, gst, bb_p, CHUNK, True)
        h_p = _outproj(oa_p.reshape(nb * seq, -1), proj_p, mm_p.reshape(nb * seq, -1),
                       mg_p.reshape(nb * seq, -1), h_p, lw["w_out"], final_norm if last else None)
        p_rows.append((
            jnp.concatenate([jnp.broadcast_to(c_m, (nb,) + c_m.shape[1:]), c_p], axis=1),
            jnp.concatenate([jnp.broadcast_to(kr_m, (nb,) + kr_m.shape[1:]), kr_p], axis=1),
            mst[0], mst[1], mst[2], gst[0], gst[1]))

        proj_s = _inproj(h_s, lw["norm_w"], lw["w_in"])
        proj_s3 = proj_s.reshape(ns, dseq, NP)
        oa_s, c_s, kr_s = _sample_attention(proj_s3, cache_mla_latent[l], cache_mla_krope[l], tabs_s, lw)
        mm_s, mg_s, sm_st, sg_st = _recurrent_groups(
            proj_s3, lw, (state_mlstm_C[l], state_mlstm_n[l], state_mlstm_m[l]),
            (state_gdn_S[l], state_gdn_conv[l]), bb_s, dseq, False)
        h_s = _outproj(oa_s.reshape(ns * dseq, -1), proj_s, mm_s.reshape(ns * dseq, -1),
                       mg_s.reshape(ns * dseq, -1), h_s, lw["w_out"], final_norm if last else None)
        s_rows.append((c_s, kr_s, sm_st[0], sm_st[1], sm_st[2], sg_st[0], sg_st[1]))

    y_prompt = h_p.reshape(nb, seq, d)
    y_sample = h_s.reshape(ns, dseq, d)
    stack = lambda rows, i: jnp.stack([r[i] for r in rows])
    return ((y_prompt, y_sample) + tuple(stack(p_rows, i) for i in range(7))
            + tuple(stack(s_rows, i) for i in range(7)))
```

```python
import functools
import math

import jax
import jax.numpy as jnp
import numpy as np
from jax import lax
from jax.experimental import pallas as pl
from jax.experimental.pallas import tpu as pltpu

F32 = jnp.float32
BF16 = jnp.bfloat16

EPS = 1e-6
ROPE_BASE = 10000.0
CHUNK = 64
N_HEADS_MLA = 8
MLA_NOPE, MLA_ROPE, MLA_V = 64, 32, 64
MLA_Q_LORA, MLA_KV_LORA = 384, 256
R_HEADS, R_DH = 4, 128
R_WIDTH = R_HEADS * R_DH
CONV_W = 4
LANES = 128
HEAD_PAD = 128
VT_ROWS = 80
SOFTMAX_SCALE = 1.0 / math.sqrt(MLA_NOPE + MLA_ROPE)
LOG2E = math.log2(math.e)
QK_LOOKAHEAD = 3

OFF_GQKV = 0
OFF_MQKV = 1536
OFF_MOZ = 3072
OFF_ZA = 4096
OFF_ZG = 4608
OFF_CKV = 5120
OFF_CQ = 5376
OFF_SMALL = 5760
NP = 5888
SM_KR, SM_MI, SM_MF, SM_GA, SM_GB = 0, 32, 36, 40, 44

VMEM_LIMIT = 56 * 1024 * 1024


def _cparams(sem):
    return pltpu.CompilerParams(dimension_semantics=sem, vmem_limit_bytes=VMEM_LIMIT)


def _bf(x):
    return x.astype(BF16)


def _dot(a, b):
    return jnp.dot(a, b, preferred_element_type=F32)


def _dot_nt(a, b):
    return lax.dot_general(a, b, (((1,), (1,)), ((), ())), preferred_element_type=F32)


def _dot_tn(a, b):
    return lax.dot_general(a, b, (((0,), (0,)), ((), ())), preferred_element_type=F32)


def _split(x):
    hi = x.astype(BF16)
    lo = (x - hi.astype(F32)).astype(BF16)
    return hi, lo


def _dot3(a, b):
    ah, al = _split(a)
    bh, bl = _split(b)
    return _dot(ah, bh) + (_dot(ah, bl) + _dot(al, bh))


def _rms(x, w):
    return x * lax.rsqrt(jnp.mean(x * x, axis=-1, keepdims=True) + EPS) * w


def _silu(x):
    return x * jax.nn.sigmoid(x)


def _rope128(x, tc, ts1, ts2):
    return x * tc + pltpu.roll(x, 16, 1) * ts1 + pltpu.roll(x, LANES - 16, 1) * ts2


def _inproj_kernel(x_ref, nw_ref, w_ref, o_ref):
    xn = _bf(_rms(x_ref[...], nw_ref[...]))
    n0 = 0
    while n0 < NP:
        n1 = min(n0 + 512, NP)
        o_ref[:, n0:n1] = _dot(xn, w_ref[:, n0:n1])
        n0 = n1


def _inproj(x2d, norm_w, w_bf):
    rows, d = x2d.shape
    tm = min(rows, 512)
    return pl.pallas_call(
        _inproj_kernel,
        grid=(rows // tm,),
        in_specs=[
            pl.BlockSpec((tm, d), lambda i: (i, 0)),
            pl.BlockSpec((1, d), lambda i: (0, 0)),
            pl.BlockSpec((d, NP), lambda i: (0, 0), pipeline_mode=pl.Buffered(1)),
        ],
        out_specs=pl.BlockSpec((tm, NP), lambda i: (i, 0)),
        out_shape=jax.ShapeDtypeStruct((rows, NP), F32),
        compiler_params=_cparams(("arbitrary",)),
        name="inproj",
    )(x2d, norm_w.reshape(1, d), w_bf)


def _mla_prep_kernel(cq_ref, ckv_ref, sm_ref, tc_ref, ts1_ref, ts2_ref, qn_ref, wuq_ref, kvn_ref,
                     wuk_ref, wuvt_ref, vone_ref, q_out, k_out, vt_out, cn_out, kr_out):
    tc, ts1, ts2 = tc_ref[...], ts1_ref[...], ts2_ref[...]
    q = _dot(_bf(_rms(cq_ref[0], qn_ref[...])), wuq_ref[...])
    cn = _rms(ckv_ref[0], kvn_ref[...])
    cn_out[0] = cn
    cnb = _bf(cn)
    kn = _dot(cnb, wuk_ref[...])
    vt_out[0] = _bf(_dot_nt(wuvt_ref[...], cnb) + vone_ref[...])
    sm = sm_ref[0]
    lane = lax.broadcasted_iota(jnp.int32, sm.shape, 1)
    kr = pltpu.roll(jnp.where(lane < MLA_ROPE, sm, 0.0), MLA_NOPE, 1)
    kr = _rope128(kr, tc, ts1, ts2)
    kr_out[0] = pltpu.roll(kr, LANES - MLA_NOPE, 1)[:, :MLA_ROPE]
    for h in range(N_HEADS_MLA):
        sl = slice(h * HEAD_PAD, (h + 1) * HEAD_PAD)
        q_out[0, :, sl] = _bf(_rope128(q[:, sl], tc, ts1, ts2) * (SOFTMAX_SCALE * LOG2E))
        k_out[0, :, sl] = _bf(kn[:, sl] + kr)


def _mla_prep(proj3, tabs, lw, tr):
    nb, seq, _ = proj3.shape
    hq = N_HEADS_MLA * HEAD_PAD
    hvt = N_HEADS_MLA * VT_ROWS
    const = lambda i, b: (0, 0)
    return pl.pallas_call(
        _mla_prep_kernel,
        grid=(seq // tr, nb),
        in_specs=[
            pl.BlockSpec((1, tr, MLA_Q_LORA), lambda i, b: (b, i, OFF_CQ // MLA_Q_LORA)),
            pl.BlockSpec((1, tr, MLA_KV_LORA), lambda i, b: (b, i, OFF_CKV // MLA_KV_LORA)),
            pl.BlockSpec((1, tr, LANES), lambda i, b: (b, i, OFF_SMALL // LANES)),
            pl.BlockSpec((tr, LANES), lambda i, b: (i, 0)),
            pl.BlockSpec((tr, LANES), lambda i, b: (i, 0)),
            pl.BlockSpec((tr, LANES), lambda i, b: (i, 0)),
            pl.BlockSpec((1, MLA_Q_LORA), const),
            pl.BlockSpec((MLA_Q_LORA, hq), const),
            pl.BlockSpec((1, MLA_KV_LORA), const),
            pl.BlockSpec((MLA_KV_LORA, hq), const),
            pl.BlockSpec((hvt, MLA_KV_LORA), const),
            pl.BlockSpec((hvt, 1), const),
        ],
        out_specs=[
            pl.BlockSpec((1, tr, hq), lambda i, b: (b, i, 0)),
            pl.BlockSpec((1, tr, hq), lambda i, b: (b, i, 0)),
            pl.BlockSpec((1, hvt, tr), lambda i, b: (b, 0, i)),
            pl.BlockSpec((1, tr, MLA_KV_LORA), lambda i, b: (b, i, 0)),
            pl.BlockSpec((1, tr, MLA_ROPE), lambda i, b: (b, i, 0)),
        ],
        out_shape=[
            jax.ShapeDtypeStruct((nb, seq, hq), BF16),
            jax.ShapeDtypeStruct((nb, seq, hq), BF16),
            jax.ShapeDtypeStruct((nb, hvt, seq), BF16),
            jax.ShapeDtypeStruct((nb, seq, MLA_KV_LORA), F32),
            jax.ShapeDtypeStruct((nb, seq, MLA_ROPE), F32),
        ],
        compiler_params=_cparams(("arbitrary", "arbitrary")),
        name="mla_prep",
    )(proj3, proj3, proj3, *tabs, lw["q_norm"], lw["w_uq"], lw["kv_norm"], lw["w_uk"], lw["w_uvt"],
      lw["v_one"])


def _softmax_step(carry, s, vt):
    m, l, acc = carry
    m_new = jnp.maximum(m, jnp.max(s, axis=1, keepdims=True))
    alpha = jnp.exp(m - m_new)
    p = jnp.exp(s - m_new)
    l = alpha * l + jnp.sum(p, axis=1, keepdims=True)
    acc = alpha * acc + _dot(_bf(p), vt)
    return m_new, l, acc


def _attn_kernel(*refs, tq, tk, has_prefix, single_tile):
    if has_prefix:
        q_ref, k_ref, vt_ref, kp_ref, vtp_ref, o_ref, m_s, acc_s, st_s = refs
    else:
        q_ref, k_ref, vt_ref, o_ref, m_s, acc_s, st_s = refs
    i = pl.program_id(1)
    q0 = i * tq
    nfull = q0 // tk
    hsl = [slice(h * HEAD_PAD, (h + 1) * HEAD_PAD) for h in range(N_HEADS_MLA)]
    rsl = [slice(h * VT_ROWS, (h + 1) * VT_ROWS) for h in range(N_HEADS_MLA)]

    def scores(h, kt):
        return _dot_nt(kt, q_ref[0, :, hsl[h]])

    def softmax_pv(h, st, vt, mask):
        if mask is not None:
            st = jnp.where(mask, st, -jnp.inf)
        m_old = m_s[h]
        m_new = jnp.maximum(m_old, jnp.max(st, axis=0, keepdims=True))
        p = jnp.exp2(st - m_new)
        acc_s[h] = jnp.exp2(m_old - m_new) * acc_s[h] + _dot(vt, _bf(p))
        m_s[h] = m_new

    def tile(r0, mask=None, r0_next=None):
        st = [None] * N_HEADS_MLA
        for h in range(N_HEADS_MLA):
            ahead = h + QK_LOOKAHEAD
            if ahead < N_HEADS_MLA:
                st[ahead] = scores(ahead, k_ref[0, pl.ds(r0, tk), hsl[ahead]])
            cur = st_s[h] if h < QK_LOOKAHEAD else st[h]
            if ahead >= N_HEADS_MLA and r0_next is not None:
                hn = ahead - N_HEADS_MLA
                st_s[hn] = scores(hn, k_ref[0, pl.ds(r0_next, tk), hsl[hn]])
            softmax_pv(h, cur, vt_ref[0, rsl[h], pl.ds(r0, tk)], mask)

    for h in range(N_HEADS_MLA):
        m_s[h] = jnp.full((1, tq), -jnp.inf, F32)
        acc_s[h] = jnp.zeros((VT_ROWS, tq), F32)
    if has_prefix:
        for h in range(N_HEADS_MLA):
            softmax_pv(h, scores(h, kp_ref[0, :, hsl[h]]), vtp_ref[0, rsl[h], :], None)
    for h in range(QK_LOOKAHEAD):
        st_s[h] = scores(h, k_ref[0, 0:tk, hsl[h]])

    def body(j, c):
        tile(pl.multiple_of(j * tk, tk), None, pl.multiple_of((j + 1) * tk, tk))
        return c

    if single_tile:
        r0 = 0
    else:
        lax.fori_loop(0, nfull, body, 0)
        r0 = pl.multiple_of(nfull * tk, tk)
    k_chunk = (nfull * tk + lax.broadcasted_iota(jnp.int32, (tk, tq), 0)) // CHUNK
    q_chunk = (q0 + lax.broadcasted_iota(jnp.int32, (tk, tq), 1)) // CHUNK
    tile(r0, k_chunk <= q_chunk)
    for hp in range(N_HEADS_MLA // 2):
        pair = []
        for h in (2 * hp, 2 * hp + 1):
            acc = acc_s[h]
            pair.append(acc[0:MLA_V] / acc[MLA_V:MLA_V + 1])
        o_ref[0, :, hp * LANES:(hp + 1) * LANES] = jnp.concatenate(pair, axis=0).T


def _attention(q, k, vt, prefix, tq, tk):
    nb, seq, hq = q.shape
    hvt = vt.shape[1]
    in_specs = [
        pl.BlockSpec((1, tq, hq), lambda b, i: (b, i, 0)),
        pl.BlockSpec((1, seq, hq), lambda b, i: (b, 0, 0)),
        pl.BlockSpec((1, hvt, seq), lambda b, i: (b, 0, 0)),
    ]
    args = [q, k, vt]
    if prefix is not None:
        kp, vtp = prefix
        npre = kp.shape[1]
        in_specs += [pl.BlockSpec((1, npre, hq), lambda b, i: (0, 0, 0)),
                     pl.BlockSpec((1, hvt, npre), lambda b, i: (0, 0, 0))]
        args += [kp, vtp]
    return pl.pallas_call(
        functools.partial(_attn_kernel, tq=tq, tk=tk, has_prefix=prefix is not None,
                          single_tile=seq == tk),
        grid=(nb, seq // tq),
        in_specs=in_specs,
        out_specs=pl.BlockSpec((1, tq, N_HEADS_MLA * MLA_V), lambda b, i: (b, i, 0)),
        out_shape=jax.ShapeDtypeStruct((nb, seq, N_HEADS_MLA * MLA_V), F32),
        scratch_shapes=[pltpu.VMEM((N_HEADS_MLA, 1, tq), F32),
                        pltpu.VMEM((N_HEADS_MLA, VT_ROWS, tq), F32),
                        pltpu.VMEM((QK_LOOKAHEAD, tk, tq), F32)],
        compiler_params=_cparams(("arbitrary", "arbitrary")),
        name="attn",
    )(*args)


def _sample_attn_kernel(cq_ref, ckv_ref, sm_ref, cc_ref, kc_ref, tc_ref, ts1_ref, ts2_ref, qn_ref,
                        wuq_ref, kvn_ref, wuk_ref, wuv_ref, o_ref, cn_out, kr_out, *, n_cache, tk, scale):
    tc, ts1, ts2 = tc_ref[...], ts1_ref[...], ts2_ref[...]
    t = cq_ref.shape[1]
    q = _dot(_bf(_rms(cq_ref[0], qn_ref[...])), wuq_ref[...])
    cn = _rms(ckv_ref[0], kvn_ref[...])
    cn_out[0] = cn
    sm = sm_ref[0]
    lane = lax.broadcasted_iota(jnp.int32, sm.shape, 1)
    kr = pltpu.roll(jnp.where(lane < MLA_ROPE, sm, 0.0), MLA_NOPE, 1)
    kr = _rope128(kr, tc, ts1, ts2)
    krn = pltpu.roll(kr, LANES - MLA_NOPE, 1)[:, :MLA_ROPE]
    kr_out[0] = krn
    ql, qr = [], []
    for h in range(N_HEADS_MLA):
        sl = slice(h * HEAD_PAD, (h + 1) * HEAD_PAD)
        qh = _rope128(q[:, sl], tc, ts1, ts2)
        ql.append(_dot_nt(_bf(qh), wuk_ref[:, sl]))
        qr.append(pltpu.roll(qh, LANES - MLA_NOPE, 1)[:, :MLA_ROPE])
    qlat = _bf(jnp.concatenate(ql, axis=0))
    qrope = _bf(jnp.concatenate(qr, axis=0))

    def step(carry, ct, kt):
        ctb = _bf(ct)
        s = (_dot_nt(qlat, ctb) + _dot_nt(qrope, _bf(kt))) * scale
        return _softmax_step(carry, s, ctb)

    rows = N_HEADS_MLA * t
    carry = (jnp.full((rows, 1), -jnp.inf, F32), jnp.zeros((rows, 1), F32),
             jnp.zeros((rows, MLA_KV_LORA), F32))
    nfull = n_cache // tk

    def body(j, c):
        r0 = pl.multiple_of(j * tk, tk)
        return step(c, cc_ref[0, pl.ds(r0, tk), :], kc_ref[0, pl.ds(r0, tk), :])

    carry = lax.fori_loop(0, nfull, body, carry)
    if nfull * tk < n_cache:
        carry = step(carry, cc_ref[0, nfull * tk:n_cache, :], kc_ref[0, nfull * tk:n_cache, :])
    m, l, acc = step(carry, cn, krn)
    olat = _bf(acc / l)
    out_lane = lax.broadcasted_iota(jnp.int32, (t, N_HEADS_MLA * MLA_V), 1) // MLA_V
    o = jnp.zeros((t, N_HEADS_MLA * MLA_V), F32)
    for h in range(N_HEADS_MLA):
        o = jnp.where(out_lane == h, _dot(olat[h * t:(h + 1) * t], wuv_ref[...]), o)
    o_ref[0] = o


def _sample_attention(proj3, cache_c, cache_kr, tabs, lw):
    nb, t, _ = proj3.shape
    n_cache = cache_c.shape[1]
    hq = N_HEADS_MLA * HEAD_PAD
    hv = N_HEADS_MLA * MLA_V
    scale = 1.0 / math.sqrt(MLA_NOPE + MLA_ROPE)
    const = lambda b: (0, 0)
    return pl.pallas_call(
        functools.partial(_sample_attn_kernel, n_cache=n_cache, tk=min(512, n_cache), scale=scale),
        grid=(nb,),
        in_specs=[
            pl.BlockSpec((1, t, MLA_Q_LORA), lambda b: (b, 0, OFF_CQ // MLA_Q_LORA)),
            pl.BlockSpec((1, t, MLA_KV_LORA), lambda b: (b, 0, OFF_CKV // MLA_KV_LORA)),
            pl.BlockSpec((1, t, LANES), lambda b: (b, 0, OFF_SMALL // LANES)),
            pl.BlockSpec((1, n_cache, MLA_KV_LORA), lambda b: (b, 0, 0)),
            pl.BlockSpec((1, n_cache, MLA_ROPE), lambda b: (b, 0, 0)),
            pl.BlockSpec((t, LANES), const),
            pl.BlockSpec((t, LANES), const),
            pl.BlockSpec((t, LANES), const),
            pl.BlockSpec((1, MLA_Q_LORA), const),
            pl.BlockSpec((MLA_Q_LORA, hq), const),
            pl.BlockSpec((1, MLA_KV_LORA), const),
            pl.BlockSpec((MLA_KV_LORA, hq), const),
            pl.BlockSpec((MLA_KV_LORA, hv), const),
        ],
        out_specs=[
            pl.BlockSpec((1, t, hv), lambda b: (b, 0, 0)),
            pl.BlockSpec((1, t, MLA_KV_LORA), lambda b: (b, 0, 0)),
            pl.BlockSpec((1, t, MLA_ROPE), lambda b: (b, 0, 0)),
        ],
        out_shape=[
            jax.ShapeDtypeStruct((nb, t, hv), F32),
            jax.ShapeDtypeStruct((nb, t, MLA_KV_LORA), F32),
            jax.ShapeDtypeStruct((nb, t, MLA_ROPE), F32),
        ],
        compiler_params=_cparams(("arbitrary",)),
        name="sample_attn",
    )(proj3, proj3, proj3, cache_c, cache_kr, *tabs, lw["q_norm"], lw["w_uq"], lw["kv_norm"],
      lw["w_uk"], lw["w_uv"])


def _chunk_masks(t):
    row = lax.broadcasted_iota(jnp.int32, (t, t), 0)
    col = lax.broadcasted_iota(jnp.int32, (t, t), 1)
    return row == col, col <= row, row <= col, col < row


def _to_row(col_vec, eye):
    return jnp.sum(jnp.where(eye, col_vec, 0.0), axis=0, keepdims=True)


def _cumsum_col_row(col_vec, eye, tril, triu):
    row_vec = _to_row(col_vec, eye)
    c = jnp.sum(jnp.where(tril, row_vec, 0.0), axis=1, keepdims=True)
    r = jnp.sum(jnp.where(triu, col_vec, 0.0), axis=0, keepdims=True)
    return c, r


def _mlstm_kernel(qkv_ref, oz_ref, sm_ref, gb_ref, nrm_ref, c0_ref, n0_ref, m0_ref,
                  out_ref, c_ref, n_ref, m_ref, *, bb, t, shared_init):
    @pl.when(pl.program_id(1) == 0)
    def _():
        for b in range(bb):
            src = 0 if shared_init else b
            c_ref[b] = c0_ref[src]
            n_ref[b] = n0_ref[src]
            m_ref[0, b:b + 1, :] = m0_ref[0, src:src + 1, :]

    eye, tril, triu, _ = _chunk_masks(t)
    chains = [(b, h) for b in range(bb) for h in range(R_HEADS)]
    i_col = [sm_ref[b, :, SM_MI + h:SM_MI + h + 1] + gb_ref[0:1, h:h + 1] for b, h in chains]
    f_col = [jax.nn.log_sigmoid(sm_ref[b, :, SM_MF + h:SM_MF + h + 1] + gb_ref[1:2, h:h + 1])
             for b, h in chains]
    i_row = [_to_row(x, eye) for x in i_col]
    f_row = [_to_row(x, eye) for x in f_col]
    b_col = [jnp.sum(jnp.where(tril, x, 0.0), axis=1, keepdims=True) for x in f_row]
    b_row = [jnp.sum(jnp.where(triu, x, 0.0), axis=0, keepdims=True) for x in f_col]
    dmat = [jnp.where(tril, bc - br + ir, -jnp.inf) for bc, br, ir in zip(b_col, b_row, i_row)]
    dmax = [jnp.max(x, axis=1, keepdims=True) for x in dmat]
    pre = []
    for (b, h), ic, bc, dm, dx in zip(chains, i_col, b_col, dmat, dmax):
        hs = slice(h * R_DH, (h + 1) * R_DH)
        q = qkv_ref[b, :, hs]
        k = qkv_ref[b, :, R_WIDTH + h * R_DH:R_WIDTH + (h + 1) * R_DH] * (R_DH ** -0.5)
        v = qkv_ref[b, :, 2 * R_WIDTH + h * R_DH:2 * R_WIDTH + (h + 1) * R_DH]
        m_prev = m_ref[0, b:b + 1, h:h + 1]
        inter = bc + m_prev
        m_t = jnp.maximum(inter, dx)
        m_new = m_t[t - 1:t, :]
        b_last = bc[t - 1:t, :]
        pre.append(dict(
            q=q, k=k, v=v, qb=_bf(q), kb=_bf(k), vb=_bf(v), m_t=m_t, m_new=m_new,
            w_inter=jnp.exp(inter - m_t), e=jnp.exp(dm - m_t),
            g_state=jnp.exp(b_last + m_prev - m_new),
            g_tok=jnp.exp(b_last - bc + ic - m_new)))
    qk_raw = [_dot_nt(p["qb"], p["kb"]) for p in pre]
    qc = [_dot_nt(p["qb"], _bf(c_ref[b, h])) for p, (b, h) in zip(pre, chains)]
    qk = [r * p["e"] for r, p in zip(qk_raw, pre)]
    pv = [_dot(_bf(x), p["vb"]) for x, p in zip(qk, pre)]
    upd = [_dot_tn(_bf(p["g_tok"] * p["v"]), p["kb"]) for p in pre]
    nvec = [n_ref[b, h:h + 1, :] for b, h in chains]
    qn = [jnp.sum(p["q"] * nv, axis=1, keepdims=True) for p, nv in zip(pre, nvec)]
    qks = [jnp.sum(x, axis=1, keepdims=True) for x in qk]
    hm = []
    for (b, h), p, qn_c, qks_c, qc_c, pv_c in zip(chains, pre, qn, qks, qc, pv):
        num = p["w_inter"] * qc_c + pv_c
        den = p["w_inter"] * qn_c + qks_c
        hh = num / jnp.maximum(jnp.abs(den), jnp.exp(-p["m_t"]))
        hm.append(jax.nn.sigmoid(oz_ref[b, :, h * R_DH:(h + 1) * R_DH]) * hh)
    ms = [jnp.mean(x * x, axis=-1, keepdims=True) for x in hm]
    for (b, h), p, nv, upd_c, hm_c, ms_c in zip(chains, pre, nvec, upd, hm, ms):
        hs = slice(h * R_DH, (h + 1) * R_DH)
        c_ref[b, h] = p["g_state"] * c_ref[b, h] + upd_c
        n_ref[b, h:h + 1, :] = p["g_state"] * nv + jnp.sum(p["g_tok"] * p["k"], axis=0, keepdims=True)
        m_ref[0, b:b + 1, h:h + 1] = p["m_new"]
        om = hm_c * lax.rsqrt(ms_c + EPS) * nrm_ref[0:1, hs]
        out_ref[b, :, hs] = om * _silu(oz_ref[b, :, R_WIDTH + h * R_DH:R_WIDTH + (h + 1) * R_DH])


def _mlstm(proj3, lw, state, bb, t, shared_init):
    nb, seq, _ = proj3.shape
    c0, n0, m0 = state
    sb = 1 if shared_init else bb
    m0 = m0.reshape(-1, sb, R_HEADS)
    st = (lambda i, c: (0, 0, 0, 0)) if shared_init else (lambda i, c: (i, 0, 0, 0))
    st3 = (lambda i, c: (0, 0, 0)) if shared_init else (lambda i, c: (i, 0, 0))
    out, c1, n1, m1 = pl.pallas_call(
        functools.partial(_mlstm_kernel, bb=bb, t=t, shared_init=shared_init),
        grid=(nb // bb, seq // t),
        in_specs=[
            pl.BlockSpec((bb, t, 3 * R_WIDTH), lambda i, c: (i, c, OFF_MQKV // (3 * R_WIDTH))),
            pl.BlockSpec((bb, t, 2 * R_WIDTH), lambda i, c: (i, c, OFF_MOZ // (2 * R_WIDTH))),
            pl.BlockSpec((bb, t, LANES), lambda i, c: (i, c, OFF_SMALL // LANES)),
            pl.BlockSpec((2, R_HEADS), lambda i, c: (0, 0)),
            pl.BlockSpec((1, R_WIDTH), lambda i, c: (0, 0)),
            pl.BlockSpec((sb, R_HEADS, R_DH, R_DH), st),
            pl.BlockSpec((sb, R_HEADS, R_DH), st3),
            pl.BlockSpec((1, sb, R_HEADS), st3),
        ],
        out_specs=[
            pl.BlockSpec((bb, t, R_WIDTH), lambda i, c: (i, c, 0)),
            pl.BlockSpec((bb, R_HEADS, R_DH, R_DH), lambda i, c: (i, 0, 0, 0)),
            pl.BlockSpec((bb, R_HEADS, R_DH), lambda i, c: (i, 0, 0)),
            pl.BlockSpec((1, bb, R_HEADS), lambda i, c: (i, 0, 0)),
        ],
        out_shape=[
            jax.ShapeDtypeStruct((nb, seq, R_WIDTH), F32),
            jax.ShapeDtypeStruct((nb, R_HEADS, R_DH, R_DH), F32),
            jax.ShapeDtypeStruct((nb, R_HEADS, R_DH), F32),
            jax.ShapeDtypeStruct((nb // bb, bb, R_HEADS), F32),
        ],
        compiler_params=_cparams(("arbitrary", "arbitrary")),
        name="mlstm",
    )(proj3, proj3, proj3, lw["m_gate_b"], lw["m_norm"], c0, n0, m0)
    return out, (c1, n1, m1.reshape(nb, R_HEADS))


def _neumann_all(a_list, t):
    levels = int(math.log2(t)) - 1
    n_acc = [-a for a in a_list]
    pw = [_dot(_bf(a), _bf(a)) for a in a_list]
    for _ in range(levels - 1):
        r = [_dot(_bf(jnp.concatenate([p, n], axis=0)), _bf(p)) for p, n in zip(pw, n_acc)]
        n_acc = [n + p + x[t:] for n, p, x in zip(n_acc, pw, r)]
        pw = [x[:t] for x in r]
    r = [_dot(_bf(n), _bf(p)) for p, n in zip(pw, n_acc)]
    return [n + p + x for n, p, x in zip(n_acc, pw, r)]


def _gdn_kernel(x_ref, z_ref, sm_ref, cw_ref, alog_ref, dtb_ref, gn_ref, s0_ref, buf0_ref,
                out_ref, s_ref, buf_ref, xprev_ref, *, bb, t, shared_init):
    c = pl.program_id(1)

    @pl.when(c == 0)
    def _():
        for b in range(bb):
            src = 0 if shared_init else b
            s_ref[b] = s0_ref[src]
            xprev_ref[b, 0:CONV_W - 1, :] = buf0_ref[src]

    eye, tril, triu, strict = _chunk_masks(t)
    rid8 = lax.broadcasted_iota(jnp.int32, (8, 3 * R_WIDTH), 0)
    chains = [(b, h) for b in range(bb) for h in range(R_HEADS)]
    g_col = [-jnp.exp(alog_ref[0:1, h:h + 1])
             * jax.nn.softplus(sm_ref[b, :, SM_GA + h:SM_GA + h + 1] + dtb_ref[0:1, h:h + 1])
             for b, h in chains]
    g_row = [_to_row(x, eye) for x in g_col]
    gc_col = [jnp.sum(jnp.where(tril, x, 0.0), axis=1, keepdims=True) for x in g_row]
    gc_row = [jnp.sum(jnp.where(triu, x, 0.0), axis=0, keepdims=True) for x in g_col]
    acts = []
    for b in range(bb):
        x = x_ref[b]
        p8 = xprev_ref[b]
        conv = x * cw_ref[CONV_W - 1:CONV_W, :]
        for j in range(1, CONV_W):
            xs = pltpu.roll(x, j, 0)
            pj = pltpu.roll(p8, (j + 5) % 8, 0)
            head = jnp.where(rid8 < j, pj, xs[0:8])
            xsj = jnp.concatenate([head, xs[8:]], axis=0)
            conv = conv + xsj * cw_ref[CONV_W - 1 - j:CONV_W - j, :]
        xprev_ref[b] = pltpu.roll(x[t - 8:t], CONV_W - 1, 0)
        buf_ref[b] = xprev_ref[b, 0:CONV_W - 1, :]
        acts.append(_silu(conv))
    gq = [acts[b][:, h * R_DH:(h + 1) * R_DH] for b, h in chains]
    gk = [acts[b][:, R_WIDTH + h * R_DH:R_WIDTH + (h + 1) * R_DH] for b, h in chains]
    q_ss = [jnp.sum(x * x, axis=1, keepdims=True) for x in gq]
    k_ss = [jnp.sum(x * x, axis=1, keepdims=True) for x in gk]
    pre = []
    for i, (b, h) in enumerate(chains):
        gv = acts[b][:, 2 * R_WIDTH + h * R_DH:2 * R_WIDTH + (h + 1) * R_DH]
        qn = gq[i] * lax.rsqrt(q_ss[i] + EPS) * (R_DH ** -0.5)
        kn = gk[i] * lax.rsqrt(k_ss[i] + EPS)
        beta = jax.nn.sigmoid(sm_ref[b, :, SM_GB + h:SM_GB + h + 1])
        eg = jnp.exp(gc_col[i])
        g_last = gc_col[i][t - 1:t, :]
        pre.append(dict(
            qn=qn, kn=kn, beta=beta, eg=eg, g_last=g_last,
            gam=jnp.exp(jnp.where(tril, gc_col[i] - gc_row[i], -jnp.inf)),
            rhs=jnp.concatenate([beta * gv, (beta * eg) * kn], axis=1),
            kdec=kn * jnp.exp(g_last - gc_col[i])))
    kq = [_dot_nt(_bf(jnp.concatenate([p["kn"], p["qn"]], axis=0)), _bf(p["kn"])) for p in pre]
    a_list = [jnp.where(strict, p["beta"] * x[:t] * p["gam"], 0.0) for p, x in zip(pre, kq)]
    n_inv = _neumann_all(a_list, t)
    sol = [p["rhs"] + _dot(_bf(n), _bf(p["rhs"])) for p, n in zip(pre, n_inv)]
    ws = [_dot(_bf(jnp.concatenate([s[:, R_DH:], p["qn"] * p["eg"]], axis=0)), _bf(s_ref[b, h]))
          for p, s, (b, h) in zip(pre, sol, chains)]
    delta = [_bf(s[:, :R_DH] - x[:t]) for s, x in zip(sol, ws)]
    o2 = [_dot(_bf(x[t:] * p["gam"]), d) for p, x, d in zip(pre, kq, delta)]
    upd = [_dot_tn(_bf(p["kdec"]), d) for p, d in zip(pre, delta)]
    o = [x[t:] + o2_c for x, o2_c in zip(ws, o2)]
    ms = [jnp.mean(x * x, axis=-1, keepdims=True) for x in o]
    for (b, h), p, o_c, ms_c, upd_c in zip(chains, pre, o, ms, upd):
        hs = slice(h * R_DH, (h + 1) * R_DH)
        s_ref[b, h] = jnp.exp(p["g_last"]) * s_ref[b, h] + upd_c
        out_ref[b, :, hs] = o_c * lax.rsqrt(ms_c + EPS) * gn_ref[...] * _silu(z_ref[b, :, hs])


def _gdn(proj3, lw, state, bb, t, shared_init):
    nb, seq, _ = proj3.shape
    s0, buf0 = state
    sb = 1 if shared_init else bb
    st = (lambda i, c: (0, 0, 0, 0)) if shared_init else (lambda i, c: (i, 0, 0, 0))
    st3 = (lambda i, c: (0, 0, 0)) if shared_init else (lambda i, c: (i, 0, 0))
    out, s1, buf1 = pl.pallas_call(
        functools.partial(_gdn_kernel, bb=bb, t=t, shared_init=shared_init),
        grid=(nb // bb, seq // t),
        in_specs=[
            pl.BlockSpec((bb, t, 3 * R_WIDTH), lambda i, c: (i, c, OFF_GQKV // (3 * R_WIDTH))),
            pl.BlockSpec((bb, t, R_WIDTH), lambda i, c: (i, c, OFF_ZG // R_WIDTH)),
            pl.BlockSpec((bb, t, LANES), lambda i, c: (i, c, OFF_SMALL // LANES)),
            pl.BlockSpec((CONV_W, 3 * R_WIDTH), lambda i, c: (0, 0)),
            pl.BlockSpec((1, R_HEADS), lambda i, c: (0, 0)),
            pl.BlockSpec((1, R_HEADS), lambda i, c: (0, 0)),
            pl.BlockSpec((1, R_DH), lambda i, c: (0, 0)),
            pl.BlockSpec((sb, R_HEADS, R_DH, R_DH), st),
            pl.BlockSpec((sb, CONV_W - 1, 3 * R_WIDTH), st3),
        ],
        out_specs=[
            pl.BlockSpec((bb, t, R_WIDTH), lambda i, c: (i, c, 0)),
            pl.BlockSpec((bb, R_HEADS, R_DH, R_DH), lambda i, c: (i, 0, 0, 0)),
            pl.BlockSpec((bb, CONV_W - 1, 3 * R_WIDTH), lambda i, c: (i, 0, 0)),
        ],
        out_shape=[
            jax.ShapeDtypeStruct((nb, seq, R_WIDTH), F32),
            jax.ShapeDtypeStruct((nb, R_HEADS, R_DH, R_DH), F32),
            jax.ShapeDtypeStruct((nb, CONV_W - 1, 3 * R_WIDTH), F32),
        ],
        scratch_shapes=[pltpu.VMEM((bb, 8, 3 * R_WIDTH), F32)],
        compiler_params=_cparams(("arbitrary", "arbitrary")),
        name="gdn",
    )(proj3, proj3, proj3, lw["g_conv_w"], lw["g_a_log"], lw["g_dt_bias"], lw["g_norm"], s0, buf0)
    return out, (s1, buf1)


def _outproj_kernel(*refs, final):
    if final:
        oa_ref, za_ref, mm_ref, mg_ref, x_ref, w_ref, fn_ref, y_ref = refs
    else:
        oa_ref, za_ref, mm_ref, mg_ref, x_ref, w_ref, y_ref = refs
    ma = oa_ref[...] * _silu(za_ref[...])
    acc = (_dot(_bf(ma), w_ref[0:R_WIDTH, :])
           + _dot(_bf(mm_ref[...]), w_ref[R_WIDTH:2 * R_WIDTH, :])
           + _dot(_bf(mg_ref[...]), w_ref[2 * R_WIDTH:3 * R_WIDTH, :]))
    hnew = x_ref[...] + acc
    y_ref[...] = _rms(hnew, fn_ref[...]) if final else hnew


def _outproj(oa, proj2, mm, mg, x2d, w_bf, final_norm):
    rows, d = x2d.shape
    tm = min(rows, 512)
    final = final_norm is not None
    in_specs = [
        pl.BlockSpec((tm, R_WIDTH), lambda i: (i, 0)),
        pl.BlockSpec((tm, R_WIDTH), lambda i: (i, OFF_ZA // R_WIDTH)),
        pl.BlockSpec((tm, R_WIDTH), lambda i: (i, 0)),
        pl.BlockSpec((tm, R_WIDTH), lambda i: (i, 0)),
        pl.BlockSpec((tm, d), lambda i: (i, 0)),
        pl.BlockSpec((3 * R_WIDTH, d), lambda i: (0, 0)),
    ]
    args = [oa, proj2, mm, mg, x2d, w_bf]
    if final:
        in_specs.append(pl.BlockSpec((1, d), lambda i: (0, 0)))
        args.append(final_norm.reshape(1, d))
    return pl.pallas_call(
        functools.partial(_outproj_kernel, final=final),
        grid=(rows // tm,),
        in_specs=in_specs,
        out_specs=pl.BlockSpec((tm, d), lambda i: (i, 0)),
        out_shape=jax.ShapeDtypeStruct((rows, d), F32),
        compiler_params=_cparams(("arbitrary",)),
        name="outproj",
    )(*args)


def _permute_w_in(w):
    d = w.shape[0]
    c_q, c_kv, k_r, z_a = w[:, 0:384], w[:, 384:640], w[:, 640:672], w[:, 672:1184]
    m_qkv, m_i, m_f = w[:, 1184:2720], w[:, 2720:2724], w[:, 2724:2728]
    m_oz = w[:, 2728:3752]
    g_qkv, g_a, g_b, z_g = w[:, 3752:5288], w[:, 5288:5292], w[:, 5292:5296], w[:, 5296:5808]
    small = jnp.concatenate([k_r, m_i, m_f, g_a, g_b, jnp.zeros((d, LANES - 48), w.dtype)], axis=1)
    return jnp.concatenate([g_qkv, m_qkv, m_oz, z_a, z_g, c_kv, c_q, small], axis=1).astype(BF16)


def _rope_tables(pos0, n):
    half = MLA_ROPE // 2
    freq = ROPE_BASE ** (-jnp.arange(half, dtype=F32) / half)
    ang = (pos0 + jnp.arange(n)).astype(F32)[:, None] * freq[None, :]
    cos, sin = jnp.cos(ang), jnp.sin(ang)
    one_lo = jnp.ones((n, MLA_NOPE), F32)
    one_hi = jnp.ones((n, LANES - MLA_NOPE - MLA_ROPE), F32)
    zero_lo = jnp.zeros((n, MLA_NOPE), F32)
    zero_hi = jnp.zeros((n, LANES - MLA_NOPE - MLA_ROPE), F32)
    zero_h = jnp.zeros((n, half), F32)
    tc = jnp.concatenate([one_lo, cos, cos, one_hi], axis=1)
    ts1 = jnp.concatenate([zero_lo, zero_h, sin, zero_hi], axis=1)
    ts2 = jnp.concatenate([zero_lo, -sin, zero_h, zero_hi], axis=1)
    return tc, ts1, ts2


def _layer_weights(l, norm_w, w_in, mla_q_norm, mla_w_uq, mla_kv_norm, mla_w_uk, mla_w_uv, mlstm_gate_b,
                   mlstm_norm, gdn_conv_w, gdn_a_log, gdn_dt_bias, gdn_norm, w_out):
    pad = HEAD_PAD - (MLA_NOPE + MLA_ROPE)
    w_uq = mla_w_uq[l].reshape(MLA_Q_LORA, N_HEADS_MLA, MLA_NOPE + MLA_ROPE)
    w_uq = jnp.pad(w_uq, ((0, 0), (0, 0), (0, pad))).reshape(MLA_Q_LORA, N_HEADS_MLA * HEAD_PAD)
    w_uk = jnp.pad(mla_w_uk[l], ((0, 0), (0, 0), (0, HEAD_PAD - MLA_NOPE)))
    w_uvt = jnp.pad(jnp.transpose(mla_w_uv[l], (1, 2, 0)), ((0, 0), (0, VT_ROWS - MLA_V), (0, 0)))
    v_one = jnp.zeros((N_HEADS_MLA, VT_ROWS, 1), F32).at[:, MLA_V, :].set(1.0)
    return {
        "w_uvt": w_uvt.reshape(N_HEADS_MLA * VT_ROWS, MLA_KV_LORA).astype(BF16),
        "v_one": v_one.reshape(N_HEADS_MLA * VT_ROWS, 1),
        "norm_w": norm_w[l],
        "w_in": _permute_w_in(w_in[l]),
        "q_norm": mla_q_norm[l].reshape(1, -1),
        "w_uq": w_uq.astype(BF16),
        "kv_norm": mla_kv_norm[l].reshape(1, -1),
        "w_uk": w_uk.reshape(MLA_KV_LORA, N_HEADS_MLA * HEAD_PAD).astype(BF16),
        "w_uv": mla_w_uv[l].reshape(MLA_KV_LORA, N_HEADS_MLA * MLA_V).astype(BF16),
        "m_gate_b": mlstm_gate_b[l],
        "m_norm": mlstm_norm[l].reshape(1, -1),
        "g_conv_w": gdn_conv_w[l],
        "g_a_log": gdn_a_log[l].reshape(1, -1),
        "g_dt_bias": gdn_dt_bias[l].reshape(1, -1),
        "g_norm": gdn_norm[l].reshape(1, -1),
        "w_out": w_out[l].astype(BF16),
    }


def _recurrent_groups(proj3, lw, m_state, g_state, bb, t, shared_init):
    mm, m_state = _mlstm(proj3, lw, m_state, bb, t, shared_init)
    mg, g_state = _gdn(proj3, lw, g_state, bb, t, shared_init)
    return mm, mg, m_state, g_state


def kernel(x_prompt, x_sample, cache_mla_latent, cache_mla_krope, state_mlstm_C, state_mlstm_n, state_mlstm_m, state_gdn_S, state_gdn_conv, meta_tokens, norm_w, w_in, mla_q_norm, mla_w_uq, mla_kv_norm, mla_w_uk, mla_w_uv, mlstm_gate_b, mlstm_norm, gdn_conv_w, gdn_a_log, gdn_dt_bias, gdn_norm, w_out, final_norm):
    nb, seq, d = x_prompt.shape
    ns, dseq, _ = x_sample.shape
    n_meta = meta_tokens.shape[0]
    n_cache = cache_mla_latent.shape[2]
    depth = norm_w.shape[0]
    assert seq % 256 == 0 and n_meta % 8 == 0 and dseq % 8 == 0 and n_meta <= CHUNK and dseq <= CHUNK

    tabs_m = _rope_tables(0, n_meta)
    tabs_p = _rope_tables(n_meta, seq)
    tabs_s = _rope_tables(n_cache, dseq)

    h_m = meta_tokens.astype(F32)
    h_p = x_prompt.reshape(nb * seq, d)
    h_s = x_sample.reshape(ns * dseq, d)
    zero_m = (jnp.zeros((1, R_HEADS, R_DH, R_DH), F32), jnp.zeros((1, R_HEADS, R_DH), F32),
              jnp.zeros((1, R_HEADS), F32))
    zero_g = (jnp.zeros((1, R_HEADS, R_DH, R_DH), F32), jnp.zeros((1, CONV_W - 1, 3 * R_WIDTH), F32))
    bb_p = 2 if nb % 2 == 0 else 1
    bb_s = 4 if ns % 4 == 0 else 1
    p_rows, s_rows = [], []
    for l in range(depth):
        lw = _layer_weights(l, norm_w, w_in, mla_q_norm, mla_w_uq, mla_kv_norm, mla_w_uk, mla_w_uv,
                            mlstm_gate_b, mlstm_norm, gdn_conv_w, gdn_a_log, gdn_dt_bias, gdn_norm, w_out)
        last = l == depth - 1

        proj_m = _inproj(h_m, lw["norm_w"], lw["w_in"])
        proj_m3 = proj_m.reshape(1, n_meta, NP)
        q_m, k_m, v_m, c_m, kr_m = _mla_prep(proj_m3, tabs_m, lw, n_meta)
        oa_m = _attention(q_m, k_m, v_m, None, n_meta, n_meta)
        mm_m, mg_m, mst, gst = _recurrent_groups(proj_m3, lw, zero_m, zero_g, 1, n_meta, True)
        if not last:
            h_m = _outproj(oa_m.reshape(n_meta, -1), proj_m, mm_m.reshape(n_meta, -1),
                           mg_m.reshape(n_meta, -1), h_m, lw["w_out"], None)

        proj_p = _inproj(h_p, lw["norm_w"], lw["w_in"])
        proj_p3 = proj_p.reshape(nb, seq, NP)
        q_p, k_p, v_p, c_p, kr_p = _mla_prep(proj_p3, tabs_p, lw, 256)
        oa_p = _attention(q_p, k_p, v_p, (k_m, v_m), 256, 256)
        mm_p, mg_p, mst, gst = _recurrent_groups(proj_p3, lw, mst, gst, bb_p, CHUNK, True)
        h_p = _outproj(oa_p.reshape(nb * seq, -1), proj_p, mm_p.reshape(nb * seq, -1),
                       mg_p.reshape(nb * seq, -1), h_p, lw["w_out"], final_norm if last else None)
        p_rows.append((
            jnp.concatenate([jnp.broadcast_to(c_m, (nb,) + c_m.shape[1:]), c_p], axis=1),
            jnp.concatenate([jnp.broadcast_to(kr_m, (nb,) + kr_m.shape[1:]), kr_p], axis=1),
            mst[0], mst[1], mst[2], gst[0], gst[1]))

        proj_s = _inproj(h_s, lw["norm_w"], lw["w_in"])
        proj_s3 = proj_s.reshape(ns, dseq, NP)
        oa_s, c_s, kr_s = _sample_attention(proj_s3, cache_mla_latent[l], cache_mla_krope[l], tabs_s, lw)
        mm_s, mg_s, sm_st, sg_st = _recurrent_groups(
            proj_s3, lw, (state_mlstm_C[l], state_mlstm_n[l], state_mlstm_m[l]),
            (state_gdn_S[l], state_gdn_conv[l]), bb_s, dseq, False)
        h_s = _outproj(oa_s.reshape(ns * dseq, -1), proj_s, mm_s.reshape(ns * dseq, -1),
                       mg_s.reshape(ns * dseq, -1), h_s, lw["w_out"], final_norm if last else None)
        s_rows.append((c_s, kr_s, sm_st[0], sm_st[1], sm_st[2], sg_st[0], sg_st[1]))

    y_prompt = h_p.reshape(nb, seq, d)
    y_sample = h_s.reshape(ns, dseq, d)
    stack = lambda rows, i: jnp.stack([r[i] for r in rows])
    return ((y_prompt, y_sample) + tuple(stack(p_rows, i) for i in range(7))
            + tuple(stack(s_rows, i) for i in range(7)))
```

```python
import functools
import math

import jax
import jax.numpy as jnp
import numpy as np
from jax import lax
from jax.experimental import pallas as pl
from jax.experimental.pallas import tpu as pltpu

F32 = jnp.float32
BF16 = jnp.bfloat16

EPS = 1e-6
ROPE_BASE = 10000.0
CHUNK = 64
N_HEADS_MLA = 8
MLA_NOPE, MLA_ROPE, MLA_V = 64, 32, 64
MLA_Q_LORA, MLA_KV_LORA = 384, 256
R_HEADS, R_DH = 4, 128
R_WIDTH = R_HEADS * R_DH
CONV_W = 4
TRI_BLOCK = 16
R_CHUNK = 128
LANES = 128
HEAD_PAD = 128
VT_ROWS = 80
SOFTMAX_SCALE = 1.0 / math.sqrt(MLA_NOPE + MLA_ROPE)
LOG2E = math.log2(math.e)
QK_LOOKAHEAD = 4

OFF_GQKV = 0
OFF_MQKV = 1536
OFF_MOZ = 3072
OFF_ZA = 4096
OFF_ZG = 4608
OFF_CKV = 5120
OFF_CQ = 5376
OFF_SMALL = 5760
NP = 5888
SM_KR, SM_MI, SM_MF, SM_GA, SM_GB = 0, 32, 36, 40, 44

VMEM_LIMIT = 56 * 1024 * 1024


def _cparams(sem):
    return pltpu.CompilerParams(dimension_semantics=sem, vmem_limit_bytes=VMEM_LIMIT)


def _bf(x):
    return x.astype(BF16)


def _dot(a, b):
    return jnp.dot(a, b, preferred_element_type=F32)


def _dot_nt(a, b):
    return lax.dot_general(a, b, (((1,), (1,)), ((), ())), preferred_element_type=F32)


def _dot_tn(a, b):
    return lax.dot_general(a, b, (((0,), (0,)), ((), ())), preferred_element_type=F32)


def _rms(x, w):
    return x * lax.rsqrt(jnp.mean(x * x, axis=-1, keepdims=True) + EPS) * w


def _silu(x):
    return x * jax.nn.sigmoid(x)


def _rope128(x, tc, ts1, ts2):
    return x * tc + pltpu.roll(x, 16, 1) * ts1 + pltpu.roll(x, LANES - 16, 1) * ts2


def _inproj_kernel(x_ref, nw_ref, w_ref, o_ref):
    xn = _bf(_rms(x_ref[...], nw_ref[...]))
    n0 = 0
    while n0 < NP:
        n1 = min(n0 + 512, NP)
        o_ref[:, n0:n1] = _dot(xn, w_ref[:, n0:n1])
        n0 = n1


def _inproj(x2d, norm_w, w_bf):
    rows, d = x2d.shape
    tm = min(rows, 512)
    return pl.pallas_call(
        _inproj_kernel,
        grid=(rows // tm,),
        in_specs=[
            pl.BlockSpec((tm, d), lambda i: (i, 0)),
            pl.BlockSpec((1, d), lambda i: (0, 0)),
            pl.BlockSpec((d, NP), lambda i: (0, 0), pipeline_mode=pl.Buffered(1)),
        ],
        out_specs=pl.BlockSpec((tm, NP), lambda i: (i, 0)),
        out_shape=jax.ShapeDtypeStruct((rows, NP), F32),
        compiler_params=_cparams(("arbitrary",)),
        name="inproj",
    )(x2d, norm_w.reshape(1, d), w_bf)


def _mla_prep_kernel(cq_ref, ckv_ref, sm_ref, tc_ref, ts1_ref, ts2_ref, qn_ref, wuq_ref, kvn_ref,
                     wuk_ref, wuvt_ref, vone_ref, q_out, k_out, vt_out, cn_out, kr_out):
    tc, ts1, ts2 = tc_ref[...], ts1_ref[...], ts2_ref[...]
    q = _dot(_bf(_rms(cq_ref[0], qn_ref[...])), wuq_ref[...])
    cn = _rms(ckv_ref[0], kvn_ref[...])
    cn_out[0] = cn
    cnb = _bf(cn)
    kn = _dot(cnb, wuk_ref[...])
    vt_out[0] = _bf(_dot_nt(wuvt_ref[...], cnb) + vone_ref[...])
    sm = sm_ref[0]
    lane = lax.broadcasted_iota(jnp.int32, sm.shape, 1)
    kr = pltpu.roll(jnp.where(lane < MLA_ROPE, sm, 0.0), MLA_NOPE, 1)
    kr = _rope128(kr, tc, ts1, ts2)
    kr_out[0] = pltpu.roll(kr, LANES - MLA_NOPE, 1)[:, :MLA_ROPE]
    for h in range(N_HEADS_MLA):
        sl = slice(h * HEAD_PAD, (h + 1) * HEAD_PAD)
        q_out[0, :, sl] = _bf(_rope128(q[:, sl], tc, ts1, ts2) * (SOFTMAX_SCALE * LOG2E))
        k_out[0, :, sl] = _bf(kn[:, sl] + kr)


def _mla_prep(proj3, tabs, lw, tr):
    nb, seq, _ = proj3.shape
    hq = N_HEADS_MLA * HEAD_PAD
    hvt = N_HEADS_MLA * VT_ROWS
    const = lambda i, b: (0, 0)
    return pl.pallas_call(
        _mla_prep_kernel,
        grid=(seq // tr, nb),
        in_specs=[
            pl.BlockSpec((1, tr, MLA_Q_LORA), lambda i, b: (b, i, OFF_CQ // MLA_Q_LORA)),
            pl.BlockSpec((1, tr, MLA_KV_LORA), lambda i, b: (b, i, OFF_CKV // MLA_KV_LORA)),
            pl.BlockSpec((1, tr, LANES), lambda i, b: (b, i, OFF_SMALL // LANES)),
            pl.BlockSpec((tr, LANES), lambda i, b: (i, 0)),
            pl.BlockSpec((tr, LANES), lambda i, b: (i, 0)),
            pl.BlockSpec((tr, LANES), lambda i, b: (i, 0)),
            pl.BlockSpec((1, MLA_Q_LORA), const),
            pl.BlockSpec((MLA_Q_LORA, hq), const),
            pl.BlockSpec((1, MLA_KV_LORA), const),
            pl.BlockSpec((MLA_KV_LORA, hq), const),
            pl.BlockSpec((hvt, MLA_KV_LORA), const),
            pl.BlockSpec((hvt, 1), const),
        ],
        out_specs=[
            pl.BlockSpec((1, tr, hq), lambda i, b: (b, i, 0)),
            pl.BlockSpec((1, tr, hq), lambda i, b: (b, i, 0)),
            pl.BlockSpec((1, hvt, tr), lambda i, b: (b, 0, i)),
            pl.BlockSpec((1, tr, MLA_KV_LORA), lambda i, b: (b, i, 0)),
            pl.BlockSpec((1, tr, MLA_ROPE), lambda i, b: (b, i, 0)),
        ],
        out_shape=[
            jax.ShapeDtypeStruct((nb, seq, hq), BF16),
            jax.ShapeDtypeStruct((nb, seq, hq), BF16),
            jax.ShapeDtypeStruct((nb, hvt, seq), BF16),
            jax.ShapeDtypeStruct((nb, seq, MLA_KV_LORA), F32),
            jax.ShapeDtypeStruct((nb, seq, MLA_ROPE), F32),
        ],
        compiler_params=_cparams(("arbitrary", "arbitrary")),
        name="mla_prep",
    )(proj3, proj3, proj3, *tabs, lw["q_norm"], lw["w_uq"], lw["kv_norm"], lw["w_uk"], lw["w_uvt"],
      lw["v_one"])


def _softmax_step(carry, s, vt):
    m, l, acc = carry
    m_new = jnp.maximum(m, jnp.max(s, axis=1, keepdims=True))
    alpha = jnp.exp(m - m_new)
    p = jnp.exp(s - m_new)
    l = alpha * l + jnp.sum(p, axis=1, keepdims=True)
    acc = alpha * acc + _dot(_bf(p), vt)
    return m_new, l, acc


def _attn_kernel(*refs, tq, tk, has_prefix, single_tile):
    if has_prefix:
        q_ref, k_ref, vt_ref, kp_ref, vtp_ref, o_ref, m_s, acc_s, st_s = refs
    else:
        q_ref, k_ref, vt_ref, o_ref, m_s, acc_s, st_s = refs
    i = pl.program_id(1)
    q0 = i * tq
    nfull = q0 // tk
    hsl = [slice(h * HEAD_PAD, (h + 1) * HEAD_PAD) for h in range(N_HEADS_MLA)]
    rsl = [slice(h * VT_ROWS, (h + 1) * VT_ROWS) for h in range(N_HEADS_MLA)]

    def scores(h, kt):
        return _dot_nt(kt, q_ref[0, :, hsl[h]])

    def softmax_pv(h, st, vt, mask):
        if mask is not None:
            st = jnp.where(mask, st, -jnp.inf)
        m_old = m_s[h]
        m_new = jnp.maximum(m_old, jnp.max(st, axis=0, keepdims=True))
        p = jnp.exp2(st - m_new)
        acc_s[h] = jnp.exp2(m_old - m_new) * acc_s[h] + _dot(vt, _bf(p))
        m_s[h] = m_new

    def tile(r0, mask=None, r0_next=None):
        st = [None] * N_HEADS_MLA
        for h in range(N_HEADS_MLA):
            ahead = h + QK_LOOKAHEAD
            if ahead < N_HEADS_MLA:
                st[ahead] = scores(ahead, k_ref[0, pl.ds(r0, tk), hsl[ahead]])
            cur = st_s[h] if h < QK_LOOKAHEAD else st[h]
            if ahead >= N_HEADS_MLA and r0_next is not None:
                hn = ahead - N_HEADS_MLA
                st_s[hn] = scores(hn, k_ref[0, pl.ds(r0_next, tk), hsl[hn]])
            softmax_pv(h, cur, vt_ref[0, rsl[h], pl.ds(r0, tk)], mask)

    for h in range(N_HEADS_MLA):
        m_s[h] = jnp.full((1, tq), -jnp.inf, F32)
        acc_s[h] = jnp.zeros((VT_ROWS, tq), F32)
    if has_prefix:
        st_pre = [scores(h, kp_ref[0, :, hsl[h]]) for h in range(N_HEADS_MLA)]
    for h in range(QK_LOOKAHEAD):
        st_s[h] = scores(h, k_ref[0, 0:tk, hsl[h]])
    if has_prefix:
        for h in range(N_HEADS_MLA):
            softmax_pv(h, st_pre[h], vtp_ref[0, rsl[h], :], None)

    def body(j, c):
        tile(pl.multiple_of(j * tk, tk), None, pl.multiple_of((j + 1) * tk, tk))
        return c

    if single_tile:
        r0 = 0
    else:
        lax.fori_loop(0, nfull, body, 0)
        r0 = pl.multiple_of(nfull * tk, tk)
    k_chunk = (nfull * tk + lax.broadcasted_iota(jnp.int32, (tk, tq), 0)) // CHUNK
    q_chunk = (q0 + lax.broadcasted_iota(jnp.int32, (tk, tq), 1)) // CHUNK
    tile(r0, k_chunk <= q_chunk)
    for hp in range(N_HEADS_MLA // 2):
        pair = []
        for h in (2 * hp, 2 * hp + 1):
            acc = acc_s[h]
            pair.append(acc[0:MLA_V] / acc[MLA_V:MLA_V + 1])
        o_ref[0, :, hp * LANES:(hp + 1) * LANES] = jnp.concatenate(pair, axis=0).T


def _attention(q, k, vt, prefix, tq, tk):
    nb, seq, hq = q.shape
    hvt = vt.shape[1]
    in_specs = [
        pl.BlockSpec((1, tq, hq), lambda b, i: (b, i, 0)),
        pl.BlockSpec((1, seq, hq), lambda b, i: (b, 0, 0)),
        pl.BlockSpec((1, hvt, seq), lambda b, i: (b, 0, 0)),
    ]
    args = [q, k, vt]
    if prefix is not None:
        kp, vtp = prefix
        npre = kp.shape[1]
        in_specs += [pl.BlockSpec((1, npre, hq), lambda b, i: (0, 0, 0)),
                     pl.BlockSpec((1, hvt, npre), lambda b, i: (0, 0, 0))]
        args += [kp, vtp]
    return pl.pallas_call(
        functools.partial(_attn_kernel, tq=tq, tk=tk, has_prefix=prefix is not None,
                          single_tile=seq == tk),
        grid=(nb, seq // tq),
        in_specs=in_specs,
        out_specs=pl.BlockSpec((1, tq, N_HEADS_MLA * MLA_V), lambda b, i: (b, i, 0)),
        out_shape=jax.ShapeDtypeStruct((nb, seq, N_HEADS_MLA * MLA_V), F32),
        scratch_shapes=[pltpu.VMEM((N_HEADS_MLA, 1, tq), F32),
                        pltpu.VMEM((N_HEADS_MLA, VT_ROWS, tq), F32),
                        pltpu.VMEM((QK_LOOKAHEAD, tk, tq), F32)],
        compiler_params=_cparams(("arbitrary", "arbitrary")),
        name="attn",
    )(*args)


def _sample_attn_kernel(cq_ref, ckv_ref, sm_ref, cc_ref, kc_ref, tc_ref, ts1_ref, ts2_ref, qn_ref,
                        wuq_ref, kvn_ref, wuk_ref, wuv_ref, o_ref, cn_out, kr_out, *, n_cache, tk, scale):
    tc, ts1, ts2 = tc_ref[...], ts1_ref[...], ts2_ref[...]
    t = cq_ref.shape[1]
    q = _dot(_bf(_rms(cq_ref[0], qn_ref[...])), wuq_ref[...])
    cn = _rms(ckv_ref[0], kvn_ref[...])
    cn_out[0] = cn
    sm = sm_ref[0]
    lane = lax.broadcasted_iota(jnp.int32, sm.shape, 1)
    kr = pltpu.roll(jnp.where(lane < MLA_ROPE, sm, 0.0), MLA_NOPE, 1)
    kr = _rope128(kr, tc, ts1, ts2)
    krn = pltpu.roll(kr, LANES - MLA_NOPE, 1)[:, :MLA_ROPE]
    kr_out[0] = krn
    ql, qr = [], []
    for h in range(N_HEADS_MLA):
        sl = slice(h * HEAD_PAD, (h + 1) * HEAD_PAD)
        qh = _rope128(q[:, sl], tc, ts1, ts2)
        ql.append(_dot_nt(_bf(qh), wuk_ref[:, sl]))
        qr.append(pltpu.roll(qh, LANES - MLA_NOPE, 1)[:, :MLA_ROPE])
    qlat = _bf(jnp.concatenate(ql, axis=0))
    qrope = _bf(jnp.concatenate(qr, axis=0))

    def step(carry, ct, kt):
        ctb = _bf(ct)
        s = (_dot_nt(qlat, ctb) + _dot_nt(qrope, _bf(kt))) * scale
        return _softmax_step(carry, s, ctb)

    rows = N_HEADS_MLA * t
    carry = (jnp.full((rows, 1), -jnp.inf, F32), jnp.zeros((rows, 1), F32),
             jnp.zeros((rows, MLA_KV_LORA), F32))
    nfull = n_cache // tk

    def body(j, c):
        r0 = pl.multiple_of(j * tk, tk)
        return step(c, cc_ref[0, pl.ds(r0, tk), :], kc_ref[0, pl.ds(r0, tk), :])

    carry = lax.fori_loop(0, nfull, body, carry)
    if nfull * tk < n_cache:
        carry = step(carry, cc_ref[0, nfull * tk:n_cache, :], kc_ref[0, nfull * tk:n_cache, :])
    m, l, acc = step(carry, cn, krn)
    olat = _bf(acc / l)
    out_lane = lax.broadcasted_iota(jnp.int32, (t, N_HEADS_MLA * MLA_V), 1) // MLA_V
    o = jnp.zeros((t, N_HEADS_MLA * MLA_V), F32)
    for h in range(N_HEADS_MLA):
        o = jnp.where(out_lane == h, _dot(olat[h * t:(h + 1) * t], wuv_ref[...]), o)
    o_ref[0] = o


def _sample_attention(proj3, cache_c, cache_kr, layer, tabs, lw):
    nb, t, _ = proj3.shape
    n_cache = cache_c.shape[2]
    hq = N_HEADS_MLA * HEAD_PAD
    hv = N_HEADS_MLA * MLA_V
    scale = 1.0 / math.sqrt(MLA_NOPE + MLA_ROPE)
    const = lambda b: (0, 0)
    return pl.pallas_call(
        functools.partial(_sample_attn_kernel, n_cache=n_cache, tk=min(512, n_cache), scale=scale),
        grid=(nb,),
        in_specs=[
            pl.BlockSpec((1, t, MLA_Q_LORA), lambda b: (b, 0, OFF_CQ // MLA_Q_LORA)),
            pl.BlockSpec((1, t, MLA_KV_LORA), lambda b: (b, 0, OFF_CKV // MLA_KV_LORA)),
            pl.BlockSpec((1, t, LANES), lambda b: (b, 0, OFF_SMALL // LANES)),
            pl.BlockSpec((None, 1, n_cache, MLA_KV_LORA), lambda b: (layer, b, 0, 0)),
            pl.BlockSpec((None, 1, n_cache, MLA_ROPE), lambda b: (layer, b, 0, 0)),
            pl.BlockSpec((t, LANES), const),
            pl.BlockSpec((t, LANES), const),
            pl.BlockSpec((t, LANES), const),
            pl.BlockSpec((1, MLA_Q_LORA), const),
            pl.BlockSpec((MLA_Q_LORA, hq), const),
            pl.BlockSpec((1, MLA_KV_LORA), const),
            pl.BlockSpec((MLA_KV_LORA, hq), const),
            pl.BlockSpec((MLA_KV_LORA, hv), const),
        ],
        out_specs=[
            pl.BlockSpec((1, t, hv), lambda b: (b, 0, 0)),
            pl.BlockSpec((1, t, MLA_KV_LORA), lambda b: (b, 0, 0)),
            pl.BlockSpec((1, t, MLA_ROPE), lambda b: (b, 0, 0)),
        ],
        out_shape=[
            jax.ShapeDtypeStruct((nb, t, hv), F32),
            jax.ShapeDtypeStruct((nb, t, MLA_KV_LORA), F32),
            jax.ShapeDtypeStruct((nb, t, MLA_ROPE), F32),
        ],
        compiler_params=_cparams(("arbitrary",)),
        name="sample_attn",
    )(proj3, proj3, proj3, cache_c, cache_kr, *tabs, lw["q_norm"], lw["w_uq"], lw["kv_norm"],
      lw["w_uk"], lw["w_uv"])


def _chunk_masks(t):
    row = lax.broadcasted_iota(jnp.int32, (t, t), 0)
    col = lax.broadcasted_iota(jnp.int32, (t, t), 1)
    return row == col, col <= row, row <= col, col < row


def _to_row(col_vec, eye):
    return jnp.sum(jnp.where(eye, col_vec, 0.0), axis=0, keepdims=True)


def _mlstm_kernel(qkv_ref, oz_ref, sm_ref, gb_ref, nrm_ref, c0_ref, n0_ref, m0_ref,
                  out_ref, c_ref, n_ref, m_ref, *, bb, t, shared_init):
    @pl.when(pl.program_id(1) == 0)
    def _():
        for b in range(bb):
            src = 0 if shared_init else b
            c_ref[b] = c0_ref[src]
            n_ref[b] = n0_ref[src]
            m_ref[0, b:b + 1, :] = m0_ref[0, src:src + 1, :]

    eye, tril, triu, _ = _chunk_masks(t)
    chains = [(b, h) for b in range(bb) for h in range(R_HEADS)]
    i_col = [sm_ref[b, :, SM_MI + h:SM_MI + h + 1] + gb_ref[0:1, h:h + 1] for b, h in chains]
    f_col = [jax.nn.log_sigmoid(sm_ref[b, :, SM_MF + h:SM_MF + h + 1] + gb_ref[1:2, h:h + 1])
             for b, h in chains]
    i_row = [_to_row(x, eye) for x in i_col]
    f_row = [_to_row(x, eye) for x in f_col]
    b_col = [jnp.sum(jnp.where(tril, x, 0.0), axis=1, keepdims=True) for x in f_row]
    b_row = [jnp.sum(jnp.where(triu, x, 0.0), axis=0, keepdims=True) for x in f_col]
    dmat = [jnp.where(tril, bc - br + ir, -jnp.inf) for bc, br, ir in zip(b_col, b_row, i_row)]
    dmax = [jnp.max(x, axis=1, keepdims=True) for x in dmat]
    pre = []
    for (b, h), ic, bc, dm, dx in zip(chains, i_col, b_col, dmat, dmax):
        hs = slice(h * R_DH, (h + 1) * R_DH)
        q = qkv_ref[b, :, hs]
        k = qkv_ref[b, :, R_WIDTH + h * R_DH:R_WIDTH + (h + 1) * R_DH] * (R_DH ** -0.5)
        v = qkv_ref[b, :, 2 * R_WIDTH + h * R_DH:2 * R_WIDTH + (h + 1) * R_DH]
        m_prev = m_ref[0, b:b + 1, h:h + 1]
        inter = bc + m_prev
        m_t = jnp.maximum(inter, dx)
        m_new = m_t[t - 1:t, :]
        b_last = bc[t - 1:t, :]
        pre.append(dict(
            q=q, k=k, v=v, qb=_bf(q), kb=_bf(k), vb=_bf(v), m_t=m_t, m_new=m_new,
            w_inter=jnp.exp(inter - m_t), e=jnp.exp(dm - m_t),
            g_state=jnp.exp(b_last + m_prev - m_new),
            g_tok=jnp.exp(b_last - bc + ic - m_new)))
    qk_raw = [_dot_nt(p["qb"], p["kb"]) for p in pre]
    qc = [_dot_nt(p["qb"], _bf(c_ref[b, h])) for p, (b, h) in zip(pre, chains)]
    qk = [r * p["e"] for r, p in zip(qk_raw, pre)]
    pv = [_dot(_bf(x), p["vb"]) for x, p in zip(qk, pre)]
    upd = [_dot_tn(_bf(p["g_tok"] * p["v"]), p["kb"]) for p in pre]
    nvec = [n_ref[b, h:h + 1, :] for b, h in chains]
    qn = [jnp.sum(p["q"] * nv, axis=1, keepdims=True) for p, nv in zip(pre, nvec)]
    qks = [jnp.sum(x, axis=1, keepdims=True) for x in qk]
    hm = []
    for (b, h), p, qn_c, qks_c, qc_c, pv_c in zip(chains, pre, qn, qks, qc, pv):
        num = p["w_inter"] * qc_c + pv_c
        den = p["w_inter"] * qn_c + qks_c
        hh = num / jnp.maximum(jnp.abs(den), jnp.exp(-p["m_t"]))
        hm.append(jax.nn.sigmoid(oz_ref[b, :, h * R_DH:(h + 1) * R_DH]) * hh)
    ms = [jnp.mean(x * x, axis=-1, keepdims=True) for x in hm]
    for (b, h), p, nv, upd_c, hm_c, ms_c in zip(chains, pre, nvec, upd, hm, ms):
        hs = slice(h * R_DH, (h + 1) * R_DH)
        c_ref[b, h] = p["g_state"] * c_ref[b, h] + upd_c
        n_ref[b, h:h + 1, :] = p["g_state"] * nv + jnp.sum(p["g_tok"] * p["k"], axis=0, keepdims=True)
        m_ref[0, b:b + 1, h:h + 1] = p["m_new"]
        om = hm_c * lax.rsqrt(ms_c + EPS) * nrm_ref[0:1, hs]
        out_ref[b, :, hs] = om * _silu(oz_ref[b, :, R_WIDTH + h * R_DH:R_WIDTH + (h + 1) * R_DH])


def _mlstm(proj3, lw, state, bb, t, shared_init):
    nb, seq, _ = proj3.shape
    c0, n0, m0 = state
    sb = 1 if shared_init else bb
    m0 = m0.reshape(-1, sb, R_HEADS)
    st = (lambda i, c: (0, 0, 0, 0)) if shared_init else (lambda i, c: (i, 0, 0, 0))
    st3 = (lambda i, c: (0, 0, 0)) if shared_init else (lambda i, c: (i, 0, 0))
    out, c1, n1, m1 = pl.pallas_call(
        functools.partial(_mlstm_kernel, bb=bb, t=t, shared_init=shared_init),
        grid=(nb // bb, seq // t),
        in_specs=[
            pl.BlockSpec((bb, t, 3 * R_WIDTH), lambda i, c: (i, c, OFF_MQKV // (3 * R_WIDTH))),
            pl.BlockSpec((bb, t, 2 * R_WIDTH), lambda i, c: (i, c, OFF_MOZ // (2 * R_WIDTH))),
            pl.BlockSpec((bb, t, LANES), lambda i, c: (i, c, OFF_SMALL // LANES)),
            pl.BlockSpec((2, R_HEADS), lambda i, c: (0, 0)),
            pl.BlockSpec((1, R_WIDTH), lambda i, c: (0, 0)),
            pl.BlockSpec((sb, R_HEADS, R_DH, R_DH), st),
            pl.BlockSpec((sb, R_HEADS, R_DH), st3),
            pl.BlockSpec((1, sb, R_HEADS), st3),
        ],
        out_specs=[
            pl.BlockSpec((bb, t, R_WIDTH), lambda i, c: (i, c, 0)),
            pl.BlockSpec((bb, R_HEADS, R_DH, R_DH), lambda i, c: (i, 0, 0, 0)),
            pl.BlockSpec((bb, R_HEADS, R_DH), lambda i, c: (i, 0, 0)),
            pl.BlockSpec((1, bb, R_HEADS), lambda i, c: (i, 0, 0)),
        ],
        out_shape=[
            jax.ShapeDtypeStruct((nb, seq, R_WIDTH), F32),
            jax.ShapeDtypeStruct((nb, R_HEADS, R_DH, R_DH), F32),
            jax.ShapeDtypeStruct((nb, R_HEADS, R_DH), F32),
            jax.ShapeDtypeStruct((nb // bb, bb, R_HEADS), F32),
        ],
        compiler_params=_cparams(("arbitrary", "arbitrary")),
        name="mlstm",
    )(proj3, proj3, proj3, lw["m_gate_b"], lw["m_norm"], c0, n0, m0)
    return out, (c1, n1, m1.reshape(nb, R_HEADS))


def _neumann_all(a_list, t, nil):
    levels = int(math.log2(nil)) - 1
    n_acc = [-a for a in a_list]
    pw = [_dot(_bf(a), _bf(a)) for a in a_list]
    for _ in range(levels - 1):
        r = [_dot(_bf(jnp.concatenate([p, n], axis=0)), _bf(p)) for p, n in zip(pw, n_acc)]
        n_acc = [n + p + x[t:] for n, p, x in zip(n_acc, pw, r)]
        pw = [x[:t] for x in r]
    r = [_dot(_bf(n), _bf(p)) for p, n in zip(pw, n_acc)]
    return [n + p + x for n, p, x in zip(n_acc, pw, r)]


def _unit_lower_inverse_all(a_list, t):
    blk = min(t, TRI_BLOCK)
    row = lax.broadcasted_iota(jnp.int32, (t, t), 0)
    col = lax.broadcasted_iota(jnp.int32, (t, t), 1)
    shift = int(math.log2(blk))
    same = jnp.right_shift(row, shift) == jnp.right_shift(col, shift)
    n_list = _neumann_all([jnp.where(same, a, 0.0) for a in a_list], t, blk)
    if blk == t:
        return n_list
    eye = (row == col).astype(F32)
    w_list = [n + eye for n in n_list]
    while blk < t:
        inner = ((jnp.right_shift(row, shift + 1) == jnp.right_shift(col, shift + 1))
                 & (jnp.right_shift(row, shift) > jnp.right_shift(col, shift)))
        z = [_dot(_bf(jnp.where(inner, a, 0.0)), _bf(w)) for a, w in zip(a_list, w_list)]
        w_list = [w - _dot(_bf(w), _bf(x)) for w, x in zip(w_list, z)]
        blk *= 2
        shift += 1
    return [w - eye for w in w_list]


def _gdn_kernel(x_ref, z_ref, sm_ref, cw_ref, alog_ref, dtb_ref, gn_ref, s0_ref, buf0_ref,
                out_ref, s_ref, buf_ref, xwin_ref, *, bb, t, shared_init):
    c = pl.program_id(1)

    @pl.when(c == 0)
    def _():
        for b in range(bb):
            src = 0 if shared_init else b
            s_ref[b] = s0_ref[src]
            xwin_ref[b, 0:8, :] = jnp.zeros((8, 3 * R_WIDTH), F32)
            xwin_ref[b, 8 - (CONV_W - 1):8, :] = buf0_ref[src]

    eye, tril, triu, strict = _chunk_masks(t)
    chains = [(b, h) for b in range(bb) for h in range(R_HEADS)]
    g_col = [-jnp.exp(alog_ref[0:1, h:h + 1])
             * jax.nn.softplus(sm_ref[b, :, SM_GA + h:SM_GA + h + 1] + dtb_ref[0:1, h:h + 1])
             for b, h in chains]
    g_row = [_to_row(x, eye) for x in g_col]
    gc_col = [jnp.sum(jnp.where(tril, x, 0.0), axis=1, keepdims=True) for x in g_row]
    gc_row = [jnp.sum(jnp.where(triu, x, 0.0), axis=0, keepdims=True) for x in g_col]
    acts = []
    for b in range(bb):
        xwin_ref[b, 8:8 + t, :] = x_ref[b]
        xw = xwin_ref[b]
        conv = x_ref[b] * cw_ref[CONV_W - 1:CONV_W, :]
        for j in range(1, CONV_W):
            conv = conv + pltpu.roll(xw, j, 0)[8:8 + t] * cw_ref[CONV_W - 1 - j:CONV_W - j, :]
        tail = xwin_ref[b, 8 + t - (CONV_W - 1):8 + t, :]
        xwin_ref[b, 8 - (CONV_W - 1):8, :] = tail
        buf_ref[b] = tail
        acts.append(_silu(conv))
    gq = [acts[b][:, h * R_DH:(h + 1) * R_DH] for b, h in chains]
    gk = [acts[b][:, R_WIDTH + h * R_DH:R_WIDTH + (h + 1) * R_DH] for b, h in chains]
    q_ss = [jnp.sum(x * x, axis=1, keepdims=True) for x in gq]
    k_ss = [jnp.sum(x * x, axis=1, keepdims=True) for x in gk]
    pre = []
    for i, (b, h) in enumerate(chains):
        gv = acts[b][:, 2 * R_WIDTH + h * R_DH:2 * R_WIDTH + (h + 1) * R_DH]
        qn = gq[i] * lax.rsqrt(q_ss[i] + EPS) * (R_DH ** -0.5)
        kn = gk[i] * lax.rsqrt(k_ss[i] + EPS)
        beta = jax.nn.sigmoid(sm_ref[b, :, SM_GB + h:SM_GB + h + 1])
        eg = jnp.exp(gc_col[i])
        g_last = gc_col[i][t - 1:t, :]
        pre.append(dict(
            qn=qn, kn=kn, beta=beta, eg=eg, g_last=g_last,
            gam=jnp.exp(jnp.where(tril, gc_col[i] - gc_row[i], -jnp.inf)),
            rhs=jnp.concatenate([beta * gv, (beta * eg) * kn], axis=1),
            kdec=kn * jnp.exp(g_last - gc_col[i])))
    kq = [_dot_nt(_bf(jnp.concatenate([p["kn"], p["qn"]], axis=0)), _bf(p["kn"])) for p in pre]
    a_list = [jnp.where(strict, p["beta"] * x[:t] * p["gam"], 0.0) for p, x in zip(pre, kq)]
    n_inv = _unit_lower_inverse_all(a_list, t)
    sol = [p["rhs"] + _dot(_bf(n), _bf(p["rhs"])) for p, n in zip(pre, n_inv)]
    ws = [_dot(_bf(jnp.concatenate([s[:, R_DH:], p["qn"] * p["eg"]], axis=0)), _bf(s_ref[b, h]))
          for p, s, (b, h) in zip(pre, sol, chains)]
    delta = [_bf(s[:, :R_DH] - x[:t]) for s, x in zip(sol, ws)]
    o2 = [_dot(_bf(x[t:] * p["gam"]), d) for p, x, d in zip(pre, kq, delta)]
    upd = [_dot_tn(_bf(p["kdec"]), d) for p, d in zip(pre, delta)]
    o = [x[t:] + o2_c for x, o2_c in zip(ws, o2)]
    ms = [jnp.mean(x * x, axis=-1, keepdims=True) for x in o]
    for (b, h), p, o_c, ms_c, upd_c in zip(chains, pre, o, ms, upd):
        hs = slice(h * R_DH, (h + 1) * R_DH)
        s_ref[b, h] = jnp.exp(p["g_last"]) * s_ref[b, h] + upd_c
        out_ref[b, :, hs] = o_c * lax.rsqrt(ms_c + EPS) * gn_ref[...] * _silu(z_ref[b, :, hs])


def _gdn(proj3, lw, state, bb, t, shared_init):
    nb, seq, _ = proj3.shape
    s0, buf0 = state
    sb = 1 if shared_init else bb
    st = (lambda i, c: (0, 0, 0, 0)) if shared_init else (lambda i, c: (i, 0, 0, 0))
    st3 = (lambda i, c: (0, 0, 0)) if shared_init else (lambda i, c: (i, 0, 0))
    out, s1, buf1 = pl.pallas_call(
        functools.partial(_gdn_kernel, bb=bb, t=t, shared_init=shared_init),
        grid=(nb // bb, seq // t),
        in_specs=[
            pl.BlockSpec((bb, t, 3 * R_WIDTH), lambda i, c: (i, c, OFF_GQKV // (3 * R_WIDTH))),
            pl.BlockSpec((bb, t, R_WIDTH), lambda i, c: (i, c, OFF_ZG // R_WIDTH)),
            pl.BlockSpec((bb, t, LANES), lambda i, c: (i, c, OFF_SMALL // LANES)),
            pl.BlockSpec((CONV_W, 3 * R_WIDTH), lambda i, c: (0, 0)),
            pl.BlockSpec((1, R_HEADS), lambda i, c: (0, 0)),
            pl.BlockSpec((1, R_HEADS), lambda i, c: (0, 0)),
            pl.BlockSpec((1, R_DH), lambda i, c: (0, 0)),
            pl.BlockSpec((sb, R_HEADS, R_DH, R_DH), st),
            pl.BlockSpec((sb, CONV_W - 1, 3 * R_WIDTH), st3),
        ],
        out_specs=[
            pl.BlockSpec((bb, t, R_WIDTH), lambda i, c: (i, c, 0)),
            pl.BlockSpec((bb, R_HEADS, R_DH, R_DH), lambda i, c: (i, 0, 0, 0)),
            pl.BlockSpec((bb, CONV_W - 1, 3 * R_WIDTH), lambda i, c: (i, 0, 0)),
        ],
        out_shape=[
            jax.ShapeDtypeStruct((nb, seq, R_WIDTH), F32),
            jax.ShapeDtypeStruct((nb, R_HEADS, R_DH, R_DH), F32),
            jax.ShapeDtypeStruct((nb, CONV_W - 1, 3 * R_WIDTH), F32),
        ],
        scratch_shapes=[pltpu.VMEM((bb, 8 + t, 3 * R_WIDTH), F32)],
        compiler_params=_cparams(("arbitrary", "arbitrary")),
        name="gdn",
    )(proj3, proj3, proj3, lw["g_conv_w"], lw["g_a_log"], lw["g_dt_bias"], lw["g_norm"], s0, buf0)
    return out, (s1, buf1)


def _outproj_kernel(*refs, final):
    if final:
        oa_ref, za_ref, mm_ref, mg_ref, x_ref, w_ref, fn_ref, y_ref = refs
    else:
        oa_ref, za_ref, mm_ref, mg_ref, x_ref, w_ref, y_ref = refs
    ma = oa_ref[...] * _silu(za_ref[...])
    acc = (_dot(_bf(ma), w_ref[0:R_WIDTH, :])
           + _dot(_bf(mm_ref[...]), w_ref[R_WIDTH:2 * R_WIDTH, :])
           + _dot(_bf(mg_ref[...]), w_ref[2 * R_WIDTH:3 * R_WIDTH, :]))
    hnew = x_ref[...] + acc
    y_ref[...] = _rms(hnew, fn_ref[...]) if final else hnew


def _outproj(oa, proj2, mm, mg, x2d, w_bf, final_norm):
    rows, d = x2d.shape
    tm = min(rows, 512)
    final = final_norm is not None
    in_specs = [
        pl.BlockSpec((tm, R_WIDTH), lambda i: (i, 0)),
        pl.BlockSpec((tm, R_WIDTH), lambda i: (i, OFF_ZA // R_WIDTH)),
        pl.BlockSpec((tm, R_WIDTH), lambda i: (i, 0)),
        pl.BlockSpec((tm, R_WIDTH), lambda i: (i, 0)),
        pl.BlockSpec((tm, d), lambda i: (i, 0)),
        pl.BlockSpec((3 * R_WIDTH, d), lambda i: (0, 0)),
    ]
    args = [oa, proj2, mm, mg, x2d, w_bf]
    if final:
        in_specs.append(pl.BlockSpec((1, d), lambda i: (0, 0)))
        args.append(final_norm.reshape(1, d))
    return pl.pallas_call(
        functools.partial(_outproj_kernel, final=final),
        grid=(rows // tm,),
        in_specs=in_specs,
        out_specs=pl.BlockSpec((tm, d), lambda i: (i, 0)),
        out_shape=jax.ShapeDtypeStruct((rows, d), F32),
        compiler_params=_cparams(("arbitrary",)),
        name="outproj",
    )(*args)


def _permute_w_in(w):
    d = w.shape[0]
    c_q, c_kv, k_r, z_a = w[:, 0:384], w[:, 384:640], w[:, 640:672], w[:, 672:1184]
    m_qkv, m_i, m_f = w[:, 1184:2720], w[:, 2720:2724], w[:, 2724:2728]
    m_oz = w[:, 2728:3752]
    g_qkv, g_a, g_b, z_g = w[:, 3752:5288], w[:, 5288:5292], w[:, 5292:5296], w[:, 5296:5808]
    small = jnp.concatenate([k_r, m_i, m_f, g_a, g_b, jnp.zeros((d, LANES - 48), w.dtype)], axis=1)
    return jnp.concatenate([g_qkv, m_qkv, m_oz, z_a, z_g, c_kv, c_q, small], axis=1).astype(BF16)


def _rope_tables(pos0, n):
    half = MLA_ROPE // 2
    freq = ROPE_BASE ** (-jnp.arange(half, dtype=F32) / half)
    ang = (pos0 + jnp.arange(n)).astype(F32)[:, None] * freq[None, :]
    cos, sin = jnp.cos(ang), jnp.sin(ang)
    one_lo = jnp.ones((n, MLA_NOPE), F32)
    one_hi = jnp.ones((n, LANES - MLA_NOPE - MLA_ROPE), F32)
    zero_lo = jnp.zeros((n, MLA_NOPE), F32)
    zero_hi = jnp.zeros((n, LANES - MLA_NOPE - MLA_ROPE), F32)
    zero_h = jnp.zeros((n, half), F32)
    tc = jnp.concatenate([one_lo, cos, cos, one_hi], axis=1)
    ts1 = jnp.concatenate([zero_lo, zero_h, sin, zero_hi], axis=1)
    ts2 = jnp.concatenate([zero_lo, -sin, zero_h, zero_hi], axis=1)
    return tc, ts1, ts2


def _layer_weights(l, norm_w, w_in, mla_q_norm, mla_w_uq, mla_kv_norm, mla_w_uk, mla_w_uv, mlstm_gate_b,
                   mlstm_norm, gdn_conv_w, gdn_a_log, gdn_dt_bias, gdn_norm, w_out):
    pad = HEAD_PAD - (MLA_NOPE + MLA_ROPE)
    w_uq = mla_w_uq[l].reshape(MLA_Q_LORA, N_HEADS_MLA, MLA_NOPE + MLA_ROPE)
    w_uq = jnp.pad(w_uq, ((0, 0), (0, 0), (0, pad))).reshape(MLA_Q_LORA, N_HEADS_MLA * HEAD_PAD)
    w_uk = jnp.pad(mla_w_uk[l], ((0, 0), (0, 0), (0, HEAD_PAD - MLA_NOPE)))
    w_uvt = jnp.pad(jnp.transpose(mla_w_uv[l], (1, 2, 0)), ((0, 0), (0, VT_ROWS - MLA_V), (0, 0)))
    v_one = jnp.zeros((N_HEADS_MLA, VT_ROWS, 1), F32).at[:, MLA_V, :].set(1.0)
    return {
        "w_uvt": w_uvt.reshape(N_HEADS_MLA * VT_ROWS, MLA_KV_LORA).astype(BF16),
        "v_one": v_one.reshape(N_HEADS_MLA * VT_ROWS, 1),
        "norm_w": norm_w[l],
        "w_in": _permute_w_in(w_in[l]),
        "q_norm": mla_q_norm[l].reshape(1, -1),
        "w_uq": w_uq.astype(BF16),
        "kv_norm": mla_kv_norm[l].reshape(1, -1),
        "w_uk": w_uk.reshape(MLA_KV_LORA, N_HEADS_MLA * HEAD_PAD).astype(BF16),
        "w_uv": mla_w_uv[l].reshape(MLA_KV_LORA, N_HEADS_MLA * MLA_V).astype(BF16),
        "m_gate_b": mlstm_gate_b[l],
        "m_norm": mlstm_norm[l].reshape(1, -1),
        "g_conv_w": gdn_conv_w[l],
        "g_a_log": gdn_a_log[l].reshape(1, -1),
        "g_dt_bias": gdn_dt_bias[l].reshape(1, -1),
        "g_norm": gdn_norm[l].reshape(1, -1),
        "w_out": w_out[l].astype(BF16),
    }


def _recurrent_groups(proj3, lw, m_state, g_state, bb, t, shared_init):
    mm, m_state = _mlstm(proj3, lw, m_state, bb, t, shared_init)
    mg, g_state = _gdn(proj3, lw, g_state, bb, t, shared_init)
    return mm, mg, m_state, g_state


def kernel(x_prompt, x_sample, cache_mla_latent, cache_mla_krope, state_mlstm_C, state_mlstm_n, state_mlstm_m, state_gdn_S, state_gdn_conv, meta_tokens, norm_w, w_in, mla_q_norm, mla_w_uq, mla_kv_norm, mla_w_uk, mla_w_uv, mlstm_gate_b, mlstm_norm, gdn_conv_w, gdn_a_log, gdn_dt_bias, gdn_norm, w_out, final_norm):
    nb, seq, d = x_prompt.shape
    ns, dseq, _ = x_sample.shape
    n_meta = meta_tokens.shape[0]
    n_cache = cache_mla_latent.shape[2]
    depth = norm_w.shape[0]
    assert seq % 256 == 0 and n_meta % 8 == 0 and dseq % 8 == 0 and n_meta <= CHUNK and dseq <= CHUNK

    tabs_m = _rope_tables(0, n_meta)
    tabs_p = _rope_tables(n_meta, seq)
    tabs_s = _rope_tables(n_cache, dseq)

    h_m = meta_tokens.astype(F32)
    h_p = x_prompt.reshape(nb * seq, d)
    h_s = x_sample.reshape(ns * dseq, d)
    zero_m = (jnp.zeros((1, R_HEADS, R_DH, R_DH), F32), jnp.zeros((1, R_HEADS, R_DH), F32),
              jnp.zeros((1, R_HEADS), F32))
    zero_g = (jnp.zeros((1, R_HEADS, R_DH, R_DH), F32), jnp.zeros((1, CONV_W - 1, 3 * R_WIDTH), F32))
    bb_p = 2 if nb % 2 == 0 else 1
    bb_s = 4 if ns % 4 == 0 else 1
    p_rows, s_rows = [], []
    for l in range(depth):
        lw = _layer_weights(l, norm_w, w_in, mla_q_norm, mla_w_uq, mla_kv_norm, mla_w_uk, mla_w_uv,
                            mlstm_gate_b, mlstm_norm, gdn_conv_w, gdn_a_log, gdn_dt_bias, gdn_norm, w_out)
        last = l == depth - 1

        proj_m = _inproj(h_m, lw["norm_w"], lw["w_in"])
        proj_m3 = proj_m.reshape(1, n_meta, NP)
        q_m, k_m, v_m, c_m, kr_m = _mla_prep(proj_m3, tabs_m, lw, n_meta)
        oa_m = _attention(q_m, k_m, v_m, None, n_meta, n_meta)
        mm_m, mg_m, mst, gst = _recurrent_groups(proj_m3, lw, zero_m, zero_g, 1, n_meta, True)
        if not last:
            h_m = _outproj(oa_m.reshape(n_meta, -1), proj_m, mm_m.reshape(n_meta, -1),
                           mg_m.reshape(n_meta, -1), h_m, lw["w_out"], None)

        proj_p = _inproj(h_p, lw["norm_w"], lw["w_in"])
        proj_p3 = proj_p.reshape(nb, seq, NP)
        q_p, k_p, v_p, c_p, kr_p = _mla_prep(proj_p3, tabs_p, lw, 256)
        oa_p = _attention(q_p, k_p, v_p, (k_m, v_m), 256, 256)
        mm_p, mg_p, mst, gst = _recurrent_groups(proj_p3, lw, mst, gst, bb_p, R_CHUNK, True)
        h_p = _outproj(oa_p.reshape(nb * seq, -1), proj_p, mm_p.reshape(nb * seq, -1),
                       mg_p.reshape(nb * seq, -1), h_p, lw["w_out"], final_norm if last else None)
        p_rows.append((
            jnp.concatenate([jnp.broadcast_to(c_m, (nb,) + c_m.shape[1:]), c_p], axis=1),
            jnp.concatenate([jnp.broadcast_to(kr_m, (nb,) + kr_m.shape[1:]), kr_p], axis=1),
            mst[0], mst[1], mst[2], gst[0], gst[1]))

        proj_s = _inproj(h_s, lw["norm_w"], lw["w_in"])
        proj_s3 = proj_s.reshape(ns, dseq, NP)
        oa_s, c_s, kr_s = _sample_attention(proj_s3, cache_mla_latent, cache_mla_krope, l, tabs_s, lw)
        mm_s, mg_s, sm_st, sg_st = _recurrent_groups(
            proj_s3, lw, (state_mlstm_C[l], state_mlstm_n[l], state_mlstm_m[l]),
            (state_gdn_S[l], state_gdn_conv[l]), bb_s, dseq, False)
        h_s = _outproj(oa_s.reshape(ns * dseq, -1), proj_s, mm_s.reshape(ns * dseq, -1),
                       mg_s.reshape(ns * dseq, -1), h_s, lw["w_out"], final_norm if last else None)
        s_rows.append((c_s, kr_s, sm_st[0], sm_st[1], sm_st[2], sg_st[0], sg_st[1]))

    y_prompt = h_p.reshape(nb, seq, d)
    y_sample = h_s.reshape(ns, dseq, d)
    stack = lambda rows, i: jnp.stack([r[i] for r in rows])
    return ((y_prompt, y_sample) + tuple(stack(p_rows, i) for i in range(7))
            + tuple(stack(s_rows, i) for i in range(7)))
```

```python
import functools
import math

import jax
import jax.numpy as jnp
import numpy as np
from jax import lax
from jax.experimental import pallas as pl
from jax.experimental.pallas import tpu as pltpu

F32 = jnp.float32
BF16 = jnp.bfloat16

EPS = 1e-6
ROPE_BASE = 10000.0
CHUNK = 64
N_HEADS_MLA = 8
MLA_NOPE, MLA_ROPE, MLA_V = 64, 32, 64
MLA_Q_LORA, MLA_KV_LORA = 384, 256
R_HEADS, R_DH = 4, 128
R_WIDTH = R_HEADS * R_DH
CONV_W = 4
TRI_BLOCK = 16
R_CHUNK = 128
LANES = 128
HEAD_PAD = 128
VT_ROWS = 80
SOFTMAX_SCALE = 1.0 / math.sqrt(MLA_NOPE + MLA_ROPE)
LOG2E = math.log2(math.e)
QK_LOOKAHEAD = 4

OFF_GQKV = 0
OFF_MQKV = 1536
OFF_MOZ = 3072
OFF_ZA = 4096
OFF_ZG = 4608
OFF_CKV = 5120
OFF_CQ = 5376
OFF_SMALL = 5760
NP = 5888
SM_KR, SM_MI, SM_MF, SM_GA, SM_GB = 0, 32, 36, 40, 44

VMEM_LIMIT = 56 * 1024 * 1024


def _cparams(sem):
    return pltpu.CompilerParams(dimension_semantics=sem, vmem_limit_bytes=VMEM_LIMIT)


def _bf(x):
    return x.astype(BF16)


def _dot(a, b):
    return jnp.dot(a, b, preferred_element_type=F32)


def _dot_nt(a, b):
    return lax.dot_general(a, b, (((1,), (1,)), ((), ())), preferred_element_type=F32)


def _dot_tn(a, b):
    return lax.dot_general(a, b, (((0,), (0,)), ((), ())), preferred_element_type=F32)


def _rms(x, w):
    return x * lax.rsqrt(jnp.mean(x * x, axis=-1, keepdims=True) + EPS) * w


def _silu(x):
    return x * jax.nn.sigmoid(x)


def _rope128(x, tc, ts1, ts2):
    return x * tc + pltpu.roll(x, 16, 1) * ts1 + pltpu.roll(x, LANES - 16, 1) * ts2


def _inproj_kernel(x_ref, nw_ref, w_ref, o_ref):
    xn = _bf(_rms(x_ref[...], nw_ref[...]))
    n0 = 0
    while n0 < NP:
        n1 = min(n0 + 512, NP)
        o_ref[:, n0:n1] = _dot(xn, w_ref[:, n0:n1])
        n0 = n1


def _inproj(x2d, norm_w, w_bf):
    rows, d = x2d.shape
    tm = min(rows, 512)
    return pl.pallas_call(
        _inproj_kernel,
        grid=(rows // tm,),
        in_specs=[
            pl.BlockSpec((tm, d), lambda i: (i, 0)),
            pl.BlockSpec((1, d), lambda i: (0, 0)),
            pl.BlockSpec((d, NP), lambda i: (0, 0), pipeline_mode=pl.Buffered(1)),
        ],
        out_specs=pl.BlockSpec((tm, NP), lambda i: (i, 0)),
        out_shape=jax.ShapeDtypeStruct((rows, NP), F32),
        compiler_params=_cparams(("arbitrary",)),
        name="inproj",
    )(x2d, norm_w.reshape(1, d), w_bf)


def _inproj_mla_kernel(x_ref, nw_ref, w_ref, tc_ref, ts1_ref, ts2_ref, qn_ref, wuq_ref, kvn_ref,
                       wuk_ref, wuvt_ref, vone_ref, o_ref, q_out, k_out, vt_out, cn_out, kr_out):
    _inproj_kernel(x_ref, nw_ref, w_ref, o_ref)
    tc, ts1, ts2 = tc_ref[...], ts1_ref[...], ts2_ref[...]
    cq = o_ref[:, OFF_CQ:OFF_CQ + MLA_Q_LORA]
    q = _dot(_bf(_rms(cq, qn_ref[...])), wuq_ref[...])
    cn = _rms(o_ref[:, OFF_CKV:OFF_CKV + MLA_KV_LORA], kvn_ref[...])
    cn_out[...] = cn
    cnb = _bf(cn)
    kn = _dot(cnb, wuk_ref[...])
    vt_out[0] = _bf(_dot_nt(wuvt_ref[...], cnb) + vone_ref[...])
    sm = o_ref[:, OFF_SMALL:OFF_SMALL + LANES]
    lane = lax.broadcasted_iota(jnp.int32, sm.shape, 1)
    kr = pltpu.roll(jnp.where(lane < MLA_ROPE, sm, 0.0), MLA_NOPE, 1)
    kr = _rope128(kr, tc, ts1, ts2)
    kr_out[...] = pltpu.roll(kr, LANES - MLA_NOPE, 1)[:, :MLA_ROPE]
    for h in range(N_HEADS_MLA):
        sl = slice(h * HEAD_PAD, (h + 1) * HEAD_PAD)
        q_out[:, sl] = _bf(_rope128(q[:, sl], tc, ts1, ts2) * (SOFTMAX_SCALE * LOG2E))
        k_out[:, sl] = _bf(kn[:, sl] + kr)


def _inproj_mla(x2d, nb, tabs, lw):
    rows, d = x2d.shape
    seq = rows // nb
    tm = min(seq, 512)
    tpb = seq // tm
    hq = N_HEADS_MLA * HEAD_PAD
    hvt = N_HEADS_MLA * VT_ROWS
    const = lambda i: (0, 0)
    row_tile = lambda i: (i, 0)
    tab_tile = lambda i: (i % tpb, 0)
    outs = pl.pallas_call(
        _inproj_mla_kernel,
        grid=(rows // tm,),
        in_specs=[
            pl.BlockSpec((tm, d), row_tile),
            pl.BlockSpec((1, d), const),
            pl.BlockSpec((d, NP), const, pipeline_mode=pl.Buffered(1)),
            pl.BlockSpec((tm, LANES), tab_tile),
            pl.BlockSpec((tm, LANES), tab_tile),
            pl.BlockSpec((tm, LANES), tab_tile),
            pl.BlockSpec((1, MLA_Q_LORA), const),
            pl.BlockSpec((MLA_Q_LORA, hq), const),
            pl.BlockSpec((1, MLA_KV_LORA), const),
            pl.BlockSpec((MLA_KV_LORA, hq), const),
            pl.BlockSpec((hvt, MLA_KV_LORA), const),
            pl.BlockSpec((hvt, 1), const),
        ],
        out_specs=[
            pl.BlockSpec((tm, NP), row_tile),
            pl.BlockSpec((tm, hq), row_tile),
            pl.BlockSpec((tm, hq), row_tile),
            pl.BlockSpec((1, hvt, tm), lambda i: (i // tpb, 0, i % tpb)),
            pl.BlockSpec((tm, MLA_KV_LORA), row_tile),
            pl.BlockSpec((tm, MLA_ROPE), row_tile),
        ],
        out_shape=[
            jax.ShapeDtypeStruct((rows, NP), F32),
            jax.ShapeDtypeStruct((rows, hq), BF16),
            jax.ShapeDtypeStruct((rows, hq), BF16),
            jax.ShapeDtypeStruct((nb, hvt, seq), BF16),
            jax.ShapeDtypeStruct((rows, MLA_KV_LORA), F32),
            jax.ShapeDtypeStruct((rows, MLA_ROPE), F32),
        ],
        compiler_params=_cparams(("arbitrary",)),
        name="inproj_mla",
    )(x2d, lw["norm_w"].reshape(1, d), lw["w_in"], *tabs, lw["q_norm"], lw["w_uq"], lw["kv_norm"],
      lw["w_uk"], lw["w_uvt"], lw["v_one"])
    proj, q, k, vt, cn, kr = outs
    return (proj, q.reshape(nb, seq, hq), k.reshape(nb, seq, hq), vt,
            cn.reshape(nb, seq, MLA_KV_LORA), kr.reshape(nb, seq, MLA_ROPE))


def _attn_kernel(*refs, tq, tk, has_prefix, single_tile):
    if has_prefix:
        q_ref, k_ref, vt_ref, kp_ref, vtp_ref, o_ref, m_s, acc_s, st_s = refs
    else:
        q_ref, k_ref, vt_ref, o_ref, m_s, acc_s, st_s = refs
    i = pl.program_id(1)
    q0 = i * tq
    nfull = q0 // tk
    hsl = [slice(h * HEAD_PAD, (h + 1) * HEAD_PAD) for h in range(N_HEADS_MLA)]
    rsl = [slice(h * VT_ROWS, (h + 1) * VT_ROWS) for h in range(N_HEADS_MLA)]

    def scores(h, kt):
        return _dot_nt(kt, q_ref[0, :, hsl[h]])

    def softmax_pv(h, st, vt, mask):
        if mask is not None:
            st = jnp.where(mask, st, -jnp.inf)
        m_old = m_s[h]
        m_new = jnp.maximum(m_old, jnp.max(st, axis=0, keepdims=True))
        p = jnp.exp2(st - m_new)
        acc_s[h] = jnp.exp2(m_old - m_new) * acc_s[h] + _dot(vt, _bf(p))
        m_s[h] = m_new

    def tile(r0, mask=None, r0_next=None):
        st = [None] * N_HEADS_MLA
        for h in range(N_HEADS_MLA):
            ahead = h + QK_LOOKAHEAD
            if ahead < N_HEADS_MLA:
                st[ahead] = scores(ahead, k_ref[0, pl.ds(r0, tk), hsl[ahead]])
            cur = st_s[h] if h < QK_LOOKAHEAD else st[h]
            if ahead >= N_HEADS_MLA and r0_next is not None:
                hn = ahead - N_HEADS_MLA
                st_s[hn] = scores(hn, k_ref[0, pl.ds(r0_next, tk), hsl[hn]])
            softmax_pv(h, cur, vt_ref[0, rsl[h], pl.ds(r0, tk)], mask)

    for h in range(N_HEADS_MLA):
        m_s[h] = jnp.full((1, tq), -jnp.inf, F32)
        acc_s[h] = jnp.zeros((VT_ROWS, tq), F32)
    if has_prefix:
        st_pre = [scores(h, kp_ref[0, :, hsl[h]]) for h in range(N_HEADS_MLA)]
    for h in range(QK_LOOKAHEAD):
        st_s[h] = scores(h, k_ref[0, 0:tk, hsl[h]])
    if has_prefix:
        for h in range(N_HEADS_MLA):
            softmax_pv(h, st_pre[h], vtp_ref[0, rsl[h], :], None)

    def body(j, c):
        tile(pl.multiple_of(j * tk, tk), None, pl.multiple_of((j + 1) * tk, tk))
        return c

    if single_tile:
        r0 = 0
    else:
        lax.fori_loop(0, nfull, body, 0)
        r0 = pl.multiple_of(nfull * tk, tk)
    k_chunk = (nfull * tk + lax.broadcasted_iota(jnp.int32, (tk, tq), 0)) // CHUNK
    q_chunk = (q0 + lax.broadcasted_iota(jnp.int32, (tk, tq), 1)) // CHUNK
    tile(r0, k_chunk <= q_chunk)
    for hp in range(N_HEADS_MLA // 2):
        pair = []
        for h in (2 * hp, 2 * hp + 1):
            acc = acc_s[h]
            pair.append(acc[0:MLA_V] / acc[MLA_V:MLA_V + 1])
        o_ref[0, :, hp * LANES:(hp + 1) * LANES] = jnp.concatenate(pair, axis=0).T


def _attention(q, k, vt, prefix, tq, tk):
    nb, seq, hq = q.shape
    hvt = vt.shape[1]
    in_specs = [
        pl.BlockSpec((1, tq, hq), lambda b, i: (b, i, 0)),
        pl.BlockSpec((1, seq, hq), lambda b, i: (b, 0, 0)),
        pl.BlockSpec((1, hvt, seq), lambda b, i: (b, 0, 0)),
    ]
    args = [q, k, vt]
    if prefix is not None:
        kp, vtp = prefix
        npre = kp.shape[1]
        in_specs += [pl.BlockSpec((1, npre, hq), lambda b, i: (0, 0, 0)),
                     pl.BlockSpec((1, hvt, npre), lambda b, i: (0, 0, 0))]
        args += [kp, vtp]
    return pl.pallas_call(
        functools.partial(_attn_kernel, tq=tq, tk=tk, has_prefix=prefix is not None,
                          single_tile=seq == tk),
        grid=(nb, seq // tq),
        in_specs=in_specs,
        out_specs=pl.BlockSpec((1, tq, N_HEADS_MLA * MLA_V), lambda b, i: (b, i, 0)),
        out_shape=jax.ShapeDtypeStruct((nb, seq, N_HEADS_MLA * MLA_V), F32),
        scratch_shapes=[pltpu.VMEM((N_HEADS_MLA, 1, tq), F32),
                        pltpu.VMEM((N_HEADS_MLA, VT_ROWS, tq), F32),
                        pltpu.VMEM((QK_LOOKAHEAD, tk, tq), F32)],
        compiler_params=_cparams(("arbitrary", "arbitrary")),
        name="attn",
    )(*args)


def _sample_attn_kernel(cq_ref, ckv_ref, sm_ref, cc_ref, kc_ref, tc_ref, ts1_ref, ts2_ref, qn_ref,
                        wuq_ref, kvn_ref, wuk_ref, wuv_ref, o_ref, cn_out, kr_out, *, n_cache, tk):
    tc, ts1, ts2 = tc_ref[...], ts1_ref[...], ts2_ref[...]
    t = cq_ref.shape[1]
    q = _dot(_bf(_rms(cq_ref[0], qn_ref[...])), wuq_ref[...])
    cn = _rms(ckv_ref[0], kvn_ref[...])
    cn_out[0] = cn
    sm = sm_ref[0]
    lane = lax.broadcasted_iota(jnp.int32, sm.shape, 1)
    kr = pltpu.roll(jnp.where(lane < MLA_ROPE, sm, 0.0), MLA_NOPE, 1)
    kr = _rope128(kr, tc, ts1, ts2)
    krn = pltpu.roll(kr, LANES - MLA_NOPE, 1)[:, :MLA_ROPE]
    kr_out[0] = krn
    ql, qr = [], []
    for h in range(N_HEADS_MLA):
        sl = slice(h * HEAD_PAD, (h + 1) * HEAD_PAD)
        qh = _rope128(q[:, sl], tc, ts1, ts2)
        ql.append(_dot_nt(_bf(qh), wuk_ref[:, sl]))
        qr.append(pltpu.roll(qh, LANES - MLA_NOPE, 1)[:, :MLA_ROPE])
    qlat = _bf(jnp.concatenate(ql, axis=0) * (SOFTMAX_SCALE * LOG2E))
    qrope = _bf(jnp.concatenate(qr, axis=0) * (SOFTMAX_SCALE * LOG2E))

    bounds = [(r0, min(r0 + tk, n_cache)) for r0 in range(0, n_cache, tk)]
    c_bf = [_bf(cc_ref[0, r0:r1, :]) for r0, r1 in bounds] + [_bf(cn)]
    k_bf = [_bf(kc_ref[0, r0:r1, :]) for r0, r1 in bounds] + [_bf(krn)]
    s = [_dot_nt(qlat, c) + _dot_nt(qrope, k) for c, k in zip(c_bf, k_bf)]

    def lane_blocks(x):
        w = x.shape[1]
        return [x[:, i:i + LANES] for i in range(0, w, LANES)] if w % LANES == 0 else None

    def folded(xs, op):
        wide = [b for x in xs if lane_blocks(x) for b in lane_blocks(x)]
        narrow = [x for x in xs if lane_blocks(x) is None]
        return ([functools.reduce(op, wide)] if wide else []) + narrow

    m = functools.reduce(jnp.maximum, [jnp.max(x, axis=1, keepdims=True) for x in folded(s, jnp.maximum)])
    p = [jnp.exp2(x - m) for x in s]
    l = functools.reduce(jnp.add, [jnp.sum(x, axis=1, keepdims=True) for x in folded(p, jnp.add)])
    acc = functools.reduce(jnp.add, [_dot(_bf(x), c) for x, c in zip(p, c_bf)])
    olat = _bf(acc / l)
    out_lane = lax.broadcasted_iota(jnp.int32, (t, N_HEADS_MLA * MLA_V), 1) // MLA_V
    o = jnp.zeros((t, N_HEADS_MLA * MLA_V), F32)
    for h in range(N_HEADS_MLA):
        o = jnp.where(out_lane == h, _dot(olat[h * t:(h + 1) * t], wuv_ref[...]), o)
    o_ref[0] = o


def _sample_attention(proj3, cache_c, cache_kr, layer, tabs, lw):
    nb, t, _ = proj3.shape
    n_cache = cache_c.shape[2]
    hq = N_HEADS_MLA * HEAD_PAD
    hv = N_HEADS_MLA * MLA_V
    const = lambda b: (0, 0)
    return pl.pallas_call(
        functools.partial(_sample_attn_kernel, n_cache=n_cache, tk=min(512, n_cache)),
        grid=(nb,),
        in_specs=[
            pl.BlockSpec((1, t, MLA_Q_LORA), lambda b: (b, 0, OFF_CQ // MLA_Q_LORA)),
            pl.BlockSpec((1, t, MLA_KV_LORA), lambda b: (b, 0, OFF_CKV // MLA_KV_LORA)),
            pl.BlockSpec((1, t, LANES), lambda b: (b, 0, OFF_SMALL // LANES)),
            pl.BlockSpec((None, 1, n_cache, MLA_KV_LORA), lambda b: (layer, b, 0, 0)),
            pl.BlockSpec((None, 1, n_cache, MLA_ROPE), lambda b: (layer, b, 0, 0)),
            pl.BlockSpec((t, LANES), const),
            pl.BlockSpec((t, LANES), const),
            pl.BlockSpec((t, LANES), const),
            pl.BlockSpec((1, MLA_Q_LORA), const),
            pl.BlockSpec((MLA_Q_LORA, hq), const),
            pl.BlockSpec((1, MLA_KV_LORA), const),
            pl.BlockSpec((MLA_KV_LORA, hq), const),
            pl.BlockSpec((MLA_KV_LORA, hv), const),
        ],
        out_specs=[
            pl.BlockSpec((1, t, hv), lambda b: (b, 0, 0)),
            pl.BlockSpec((1, t, MLA_KV_LORA), lambda b: (b, 0, 0)),
            pl.BlockSpec((1, t, MLA_ROPE), lambda b: (b, 0, 0)),
        ],
        out_shape=[
            jax.ShapeDtypeStruct((nb, t, hv), F32),
            jax.ShapeDtypeStruct((nb, t, MLA_KV_LORA), F32),
            jax.ShapeDtypeStruct((nb, t, MLA_ROPE), F32),
        ],
        compiler_params=_cparams(("arbitrary",)),
        name="sample_attn",
    )(proj3, proj3, proj3, cache_c, cache_kr, *tabs, lw["q_norm"], lw["w_uq"], lw["kv_norm"],
      lw["w_uk"], lw["w_uv"])


def _chunk_masks(t):
    row = lax.broadcasted_iota(jnp.int32, (t, t), 0)
    col = lax.broadcasted_iota(jnp.int32, (t, t), 1)
    return row == col, col <= row, row <= col, col < row


def _to_row(col_vec, eye):
    return jnp.sum(jnp.where(eye, col_vec, 0.0), axis=0, keepdims=True)


def _mlstm_kernel(qkv_ref, oz_ref, sm_ref, gb_ref, nrm_ref, c0_ref, n0_ref, m0_ref,
                  out_ref, c_ref, n_ref, m_ref, *, bb, t, shared_init):
    @pl.when(pl.program_id(1) == 0)
    def _():
        for b in range(bb):
            src = 0 if shared_init else b
            c_ref[b] = c0_ref[src]
            n_ref[b] = n0_ref[src]
            m_ref[0, b:b + 1, :] = m0_ref[0, src:src + 1, :]

    eye, tril, triu, _ = _chunk_masks(t)
    chains = [(b, h) for b in range(bb) for h in range(R_HEADS)]
    i_col = [sm_ref[b, :, SM_MI + h:SM_MI + h + 1] + gb_ref[0:1, h:h + 1] for b, h in chains]
    f_col = [jax.nn.log_sigmoid(sm_ref[b, :, SM_MF + h:SM_MF + h + 1] + gb_ref[1:2, h:h + 1])
             for b, h in chains]
    i_row = [_to_row(x, eye) for x in i_col]
    f_row = [_to_row(x, eye) for x in f_col]
    b_col = [jnp.sum(jnp.where(tril, x, 0.0), axis=1, keepdims=True) for x in f_row]
    b_row = [jnp.sum(jnp.where(triu, x, 0.0), axis=0, keepdims=True) for x in f_col]
    dmat = [jnp.where(tril, bc - br + ir, -jnp.inf) for bc, br, ir in zip(b_col, b_row, i_row)]
    dmax = [jnp.max(x, axis=1, keepdims=True) for x in dmat]
    pre = []
    for (b, h), ic, bc, dm, dx in zip(chains, i_col, b_col, dmat, dmax):
        hs = slice(h * R_DH, (h + 1) * R_DH)
        q = qkv_ref[b, :, hs]
        k = qkv_ref[b, :, R_WIDTH + h * R_DH:R_WIDTH + (h + 1) * R_DH] * (R_DH ** -0.5)
        v = qkv_ref[b, :, 2 * R_WIDTH + h * R_DH:2 * R_WIDTH + (h + 1) * R_DH]
        m_prev = m_ref[0, b:b + 1, h:h + 1]
        inter = bc + m_prev
        m_t = jnp.maximum(inter, dx)
        m_new = m_t[t - 1:t, :]
        b_last = bc[t - 1:t, :]
        pre.append(dict(
            q=q, k=k, v=v, qb=_bf(q), kb=_bf(k), vb=_bf(v), m_t=m_t, m_new=m_new,
            w_inter=jnp.exp(inter - m_t), e=jnp.exp(dm - m_t),
            g_state=jnp.exp(b_last + m_prev - m_new),
            g_tok=jnp.exp(b_last - bc + ic - m_new)))
    qk_raw = [_dot_nt(p["qb"], p["kb"]) for p in pre]
    qc = [_dot_nt(p["qb"], _bf(c_ref[b, h])) for p, (b, h) in zip(pre, chains)]
    qk = [r * p["e"] for r, p in zip(qk_raw, pre)]
    pv = [_dot(_bf(x), p["vb"]) for x, p in zip(qk, pre)]
    upd = [_dot_tn(_bf(p["g_tok"] * p["v"]), p["kb"]) for p in pre]
    nvec = [n_ref[b, h:h + 1, :] for b, h in chains]
    qn = [jnp.sum(p["q"] * nv, axis=1, keepdims=True) for p, nv in zip(pre, nvec)]
    qks = [jnp.sum(x, axis=1, keepdims=True) for x in qk]
    hm = []
    for (b, h), p, qn_c, qks_c, qc_c, pv_c in zip(chains, pre, qn, qks, qc, pv):
        num = p["w_inter"] * qc_c + pv_c
        den = p["w_inter"] * qn_c + qks_c
        hh = num / jnp.maximum(jnp.abs(den), jnp.exp(-p["m_t"]))
        hm.append(jax.nn.sigmoid(oz_ref[b, :, h * R_DH:(h + 1) * R_DH]) * hh)
    ms = [jnp.mean(x * x, axis=-1, keepdims=True) for x in hm]
    for (b, h), p, nv, upd_c, hm_c, ms_c in zip(chains, pre, nvec, upd, hm, ms):
        hs = slice(h * R_DH, (h + 1) * R_DH)
        c_ref[b, h] = p["g_state"] * c_ref[b, h] + upd_c
        n_ref[b, h:h + 1, :] = p["g_state"] * nv + jnp.sum(p["g_tok"] * p["k"], axis=0, keepdims=True)
        m_ref[0, b:b + 1, h:h + 1] = p["m_new"]
        om = hm_c * lax.rsqrt(ms_c + EPS) * nrm_ref[0:1, hs]
        out_ref[b, :, hs] = om * _silu(oz_ref[b, :, R_WIDTH + h * R_DH:R_WIDTH + (h + 1) * R_DH])


def _mlstm(proj3, lw, state, bb, t, shared_init):
    nb, seq, _ = proj3.shape
    c0, n0, m0 = state
    sb = 1 if shared_init else bb
    m0 = m0.reshape(-1, sb, R_HEADS)
    st = (lambda i, c: (0, 0, 0, 0)) if shared_init else (lambda i, c: (i, 0, 0, 0))
    st3 = (lambda i, c: (0, 0, 0)) if shared_init else (lambda i, c: (i, 0, 0))
    out, c1, n1, m1 = pl.pallas_call(
        functools.partial(_mlstm_kernel, bb=bb, t=t, shared_init=shared_init),
        grid=(nb // bb, seq // t),
        in_specs=[
            pl.BlockSpec((bb, t, 3 * R_WIDTH), lambda i, c: (i, c, OFF_MQKV // (3 * R_WIDTH))),
            pl.BlockSpec((bb, t, 2 * R_WIDTH), lambda i, c: (i, c, OFF_MOZ // (2 * R_WIDTH))),
            pl.BlockSpec((bb, t, LANES), lambda i, c: (i, c, OFF_SMALL // LANES)),
            pl.BlockSpec((2, R_HEADS), lambda i, c: (0, 0)),
            pl.BlockSpec((1, R_WIDTH), lambda i, c: (0, 0)),
            pl.BlockSpec((sb, R_HEADS, R_DH, R_DH), st),
            pl.BlockSpec((sb, R_HEADS, R_DH), st3),
            pl.BlockSpec((1, sb, R_HEADS), st3),
        ],
        out_specs=[
            pl.BlockSpec((bb, t, R_WIDTH), lambda i, c: (i, c, 0)),
            pl.BlockSpec((bb, R_HEADS, R_DH, R_DH), lambda i, c: (i, 0, 0, 0)),
            pl.BlockSpec((bb, R_HEADS, R_DH), lambda i, c: (i, 0, 0)),
            pl.BlockSpec((1, bb, R_HEADS), lambda i, c: (i, 0, 0)),
        ],
        out_shape=[
            jax.ShapeDtypeStruct((nb, seq, R_WIDTH), F32),
            jax.ShapeDtypeStruct((nb, R_HEADS, R_DH, R_DH), F32),
            jax.ShapeDtypeStruct((nb, R_HEADS, R_DH), F32),
            jax.ShapeDtypeStruct((nb // bb, bb, R_HEADS), F32),
        ],
        compiler_params=_cparams(("arbitrary", "arbitrary")),
        name="mlstm",
    )(proj3, proj3, proj3, lw["m_gate_b"], lw["m_norm"], c0, n0, m0)
    return out, (c1, n1, m1.reshape(nb, R_HEADS))


def _neumann_all(a_list, t, nil):
    levels = int(math.log2(nil)) - 1
    n_acc = [-a for a in a_list]
    pw = [_dot(_bf(a), _bf(a)) for a in a_list]
    for _ in range(levels - 1):
        r = [_dot(_bf(jnp.concatenate([p, n], axis=0)), _bf(p)) for p, n in zip(pw, n_acc)]
        n_acc = [n + p + x[t:] for n, p, x in zip(n_acc, pw, r)]
        pw = [x[:t] for x in r]
    r = [_dot(_bf(n), _bf(p)) for p, n in zip(pw, n_acc)]
    return [n + p + x for n, p, x in zip(n_acc, pw, r)]


def _unit_lower_inverse_all(a_list, t):
    blk = min(t, TRI_BLOCK)
    row = lax.broadcasted_iota(jnp.int32, (t, t), 0)
    col = lax.broadcasted_iota(jnp.int32, (t, t), 1)
    shift = int(math.log2(blk))
    same = jnp.right_shift(row, shift) == jnp.right_shift(col, shift)
    n_list = _neumann_all([jnp.where(same, a, 0.0) for a in a_list], t, blk)
    if blk == t:
        return n_list
    eye = (row == col).astype(F32)
    w_list = [n + eye for n in n_list]
    while blk < t:
        inner = ((jnp.right_shift(row, shift + 1) == jnp.right_shift(col, shift + 1))
                 & (jnp.right_shift(row, shift) > jnp.right_shift(col, shift)))
        z = [_dot(_bf(jnp.where(inner, a, 0.0)), _bf(w)) for a, w in zip(a_list, w_list)]
        w_list = [w - _dot(_bf(w), _bf(x)) for w, x in zip(w_list, z)]
        blk *= 2
        shift += 1
    return [w - eye for w in w_list]


def _gdn_kernel(x_ref, z_ref, sm_ref, cw_ref, alog_ref, dtb_ref, gn_ref, s0_ref, buf0_ref,
                out_ref, s_ref, buf_ref, xwin_ref, *, bb, t, shared_init):
    c = pl.program_id(1)

    @pl.when(c == 0)
    def _():
        for b in range(bb):
            src = 0 if shared_init else b
            s_ref[b] = s0_ref[src]
            xwin_ref[b, 0:8, :] = jnp.zeros((8, 3 * R_WIDTH), F32)
            xwin_ref[b, 8 - (CONV_W - 1):8, :] = buf0_ref[src]

    eye, tril, triu, strict = _chunk_masks(t)
    chains = [(b, h) for b in range(bb) for h in range(R_HEADS)]
    g_col = [-jnp.exp(alog_ref[0:1, h:h + 1])
             * jax.nn.softplus(sm_ref[b, :, SM_GA + h:SM_GA + h + 1] + dtb_ref[0:1, h:h + 1])
             for b, h in chains]
    g_row = [_to_row(x, eye) for x in g_col]
    gc_col = [jnp.sum(jnp.where(tril, x, 0.0), axis=1, keepdims=True) for x in g_row]
    gc_row = [jnp.sum(jnp.where(triu, x, 0.0), axis=0, keepdims=True) for x in g_col]
    acts = []
    for b in range(bb):
        xwin_ref[b, 8:8 + t, :] = x_ref[b]
        xw = xwin_ref[b]
        conv = x_ref[b] * cw_ref[CONV_W - 1:CONV_W, :]
        for j in range(1, CONV_W):
            conv = conv + pltpu.roll(xw, j, 0)[8:8 + t] * cw_ref[CONV_W - 1 - j:CONV_W - j, :]
        tail = xwin_ref[b, 8 + t - (CONV_W - 1):8 + t, :]
        xwin_ref[b, 8 - (CONV_W - 1):8, :] = tail
        buf_ref[b] = tail
        acts.append(_silu(conv))
    gq = [acts[b][:, h * R_DH:(h + 1) * R_DH] for b, h in chains]
    gk = [acts[b][:, R_WIDTH + h * R_DH:R_WIDTH + (h + 1) * R_DH] for b, h in chains]
    q_ss = [jnp.sum(x * x, axis=1, keepdims=True) for x in gq]
    k_ss = [jnp.sum(x * x, axis=1, keepdims=True) for x in gk]
    pre = []
    for i, (b, h) in enumerate(chains):
        gv = acts[b][:, 2 * R_WIDTH + h * R_DH:2 * R_WIDTH + (h + 1) * R_DH]
        qn = gq[i] * lax.rsqrt(q_ss[i] + EPS) * (R_DH ** -0.5)
        kn = gk[i] * lax.rsqrt(k_ss[i] + EPS)
        beta = jax.nn.sigmoid(sm_ref[b, :, SM_GB + h:SM_GB + h + 1])
        eg = jnp.exp(gc_col[i])
        g_last = gc_col[i][t - 1:t, :]
        pre.append(dict(
            qn=qn, kn=kn, beta=beta, eg=eg, g_last=g_last,
            gam=jnp.exp(jnp.where(tril, gc_col[i] - gc_row[i], -jnp.inf)),
            rhs=jnp.concatenate([beta * gv, (beta * eg) * kn], axis=1),
            kdec=kn * jnp.exp(g_last - gc_col[i])))
    kq = [_dot_nt(_bf(jnp.concatenate([p["kn"], p["qn"]], axis=0)), _bf(p["kn"])) for p in pre]
    a_list = [jnp.where(strict, p["beta"] * x[:t] * p["gam"], 0.0) for p, x in zip(pre, kq)]
    n_inv = _unit_lower_inverse_all(a_list, t)
    sol = [p["rhs"] + _dot(_bf(n), _bf(p["rhs"])) for p, n in zip(pre, n_inv)]
    ws = [_dot(_bf(jnp.concatenate([s[:, R_DH:], p["qn"] * p["eg"]], axis=0)), _bf(s_ref[b, h]))
          for p, s, (b, h) in zip(pre, sol, chains)]
    delta = [_bf(s[:, :R_DH] - x[:t]) for s, x in zip(sol, ws)]
    o2 = [_dot(_bf(x[t:] * p["gam"]), d) for p, x, d in zip(pre, kq, delta)]
    upd = [_dot_tn(_bf(p["kdec"]), d) for p, d in zip(pre, delta)]
    o = [x[t:] + o2_c for x, o2_c in zip(ws, o2)]
    ms = [jnp.mean(x * x, axis=-1, keepdims=True) for x in o]
    for (b, h), p, o_c, ms_c, upd_c in zip(chains, pre, o, ms, upd):
        hs = slice(h * R_DH, (h + 1) * R_DH)
        s_ref[b, h] = jnp.exp(p["g_last"]) * s_ref[b, h] + upd_c
        out_ref[b, :, hs] = o_c * lax.rsqrt(ms_c + EPS) * gn_ref[...] * _silu(z_ref[b, :, hs])


def _gdn(proj3, lw, state, bb, t, shared_init):
    nb, seq, _ = proj3.shape
    s0, buf0 = state
    sb = 1 if shared_init else bb
    st = (lambda i, c: (0, 0, 0, 0)) if shared_init else (lambda i, c: (i, 0, 0, 0))
    st3 = (lambda i, c: (0, 0, 0)) if shared_init else (lambda i, c: (i, 0, 0))
    out, s1, buf1 = pl.pallas_call(
        functools.partial(_gdn_kernel, bb=bb, t=t, shared_init=shared_init),
        grid=(nb // bb, seq // t),
        in_specs=[
            pl.BlockSpec((bb, t, 3 * R_WIDTH), lambda i, c: (i, c, OFF_GQKV // (3 * R_WIDTH))),
            pl.BlockSpec((bb, t, R_WIDTH), lambda i, c: (i, c, OFF_ZG // R_WIDTH)),
            pl.BlockSpec((bb, t, LANES), lambda i, c: (i, c, OFF_SMALL // LANES)),
            pl.BlockSpec((CONV_W, 3 * R_WIDTH), lambda i, c: (0, 0)),
            pl.BlockSpec((1, R_HEADS), lambda i, c: (0, 0)),
            pl.BlockSpec((1, R_HEADS), lambda i, c: (0, 0)),
            pl.BlockSpec((1, R_DH), lambda i, c: (0, 0)),
            pl.BlockSpec((sb, R_HEADS, R_DH, R_DH), st),
            pl.BlockSpec((sb, CONV_W - 1, 3 * R_WIDTH), st3),
        ],
        out_specs=[
            pl.BlockSpec((bb, t, R_WIDTH), lambda i, c: (i, c, 0)),
            pl.BlockSpec((bb, R_HEADS, R_DH, R_DH), lambda i, c: (i, 0, 0, 0)),
            pl.BlockSpec((bb, CONV_W - 1, 3 * R_WIDTH), lambda i, c: (i, 0, 0)),
        ],
        out_shape=[
            jax.ShapeDtypeStruct((nb, seq, R_WIDTH), F32),
            jax.ShapeDtypeStruct((nb, R_HEADS, R_DH, R_DH), F32),
            jax.ShapeDtypeStruct((nb, CONV_W - 1, 3 * R_WIDTH), F32),
        ],
        scratch_shapes=[pltpu.VMEM((bb, 8 + t, 3 * R_WIDTH), F32)],
        compiler_params=_cparams(("arbitrary", "arbitrary")),
        name="gdn",
    )(proj3, proj3, proj3, lw["g_conv_w"], lw["g_a_log"], lw["g_dt_bias"], lw["g_norm"], s0, buf0)
    return out, (s1, buf1)


def _outproj_kernel(*refs, final):
    if final:
        oa_ref, za_ref, mm_ref, mg_ref, x_ref, w_ref, fn_ref, y_ref = refs
    else:
        oa_ref, za_ref, mm_ref, mg_ref, x_ref, w_ref, y_ref = refs
    ma = oa_ref[...] * _silu(za_ref[...])
    acc = (_dot(_bf(ma), w_ref[0:R_WIDTH, :])
           + _dot(_bf(mm_ref[...]), w_ref[R_WIDTH:2 * R_WIDTH, :])
           + _dot(_bf(mg_ref[...]), w_ref[2 * R_WIDTH:3 * R_WIDTH, :]))
    hnew = x_ref[...] + acc
    y_ref[...] = _rms(hnew, fn_ref[...]) if final else hnew


def _outproj(oa, proj2, mm, mg, x2d, w_bf, final_norm):
    rows, d = x2d.shape
    tm = min(rows, 512)
    final = final_norm is not None
    in_specs = [
        pl.BlockSpec((tm, R_WIDTH), lambda i: (i, 0)),
        pl.BlockSpec((tm, R_WIDTH), lambda i: (i, OFF_ZA // R_WIDTH)),
        pl.BlockSpec((tm, R_WIDTH), lambda i: (i, 0)),
        pl.BlockSpec((tm, R_WIDTH), lambda i: (i, 0)),
        pl.BlockSpec((tm, d), lambda i: (i, 0)),
        pl.BlockSpec((3 * R_WIDTH, d), lambda i: (0, 0)),
    ]
    args = [oa, proj2, mm, mg, x2d, w_bf]
    if final:
        in_specs.append(pl.BlockSpec((1, d), lambda i: (0, 0)))
        args.append(final_norm.reshape(1, d))
    return pl.pallas_call(
        functools.partial(_outproj_kernel, final=final),
        grid=(rows // tm,),
        in_specs=in_specs,
        out_specs=pl.BlockSpec((tm, d), lambda i: (i, 0)),
        out_shape=jax.ShapeDtypeStruct((rows, d), F32),
        compiler_params=_cparams(("arbitrary",)),
        name="outproj",
    )(*args)


def _permute_w_in(w):
    d = w.shape[0]
    c_q, c_kv, k_r, z_a = w[:, 0:384], w[:, 384:640], w[:, 640:672], w[:, 672:1184]
    m_qkv, m_i, m_f = w[:, 1184:2720], w[:, 2720:2724], w[:, 2724:2728]
    m_oz = w[:, 2728:3752]
    g_qkv, g_a, g_b, z_g = w[:, 3752:5288], w[:, 5288:5292], w[:, 5292:5296], w[:, 5296:5808]
    small = jnp.concatenate([k_r, m_i, m_f, g_a, g_b, jnp.zeros((d, LANES - 48), w.dtype)], axis=1)
    return jnp.concatenate([g_qkv, m_qkv, m_oz, z_a, z_g, c_kv, c_q, small], axis=1).astype(BF16)


def _rope_tables(pos0, n):
    half = MLA_ROPE // 2
    freq = ROPE_BASE ** (-jnp.arange(half, dtype=F32) / half)
    ang = (pos0 + jnp.arange(n)).astype(F32)[:, None] * freq[None, :]
    cos, sin = jnp.cos(ang), jnp.sin(ang)
    one_lo = jnp.ones((n, MLA_NOPE), F32)
    one_hi = jnp.ones((n, LANES - MLA_NOPE - MLA_ROPE), F32)
    zero_lo = jnp.zeros((n, MLA_NOPE), F32)
    zero_hi = jnp.zeros((n, LANES - MLA_NOPE - MLA_ROPE), F32)
    zero_h = jnp.zeros((n, half), F32)
    tc = jnp.concatenate([one_lo, cos, cos, one_hi], axis=1)
    ts1 = jnp.concatenate([zero_lo, zero_h, sin, zero_hi], axis=1)
    ts2 = jnp.concatenate([zero_lo, -sin, zero_h, zero_hi], axis=1)
    return tc, ts1, ts2


def _layer_weights(l, norm_w, w_in, mla_q_norm, mla_w_uq, mla_kv_norm, mla_w_uk, mla_w_uv, mlstm_gate_b,
                   mlstm_norm, gdn_conv_w, gdn_a_log, gdn_dt_bias, gdn_norm, w_out):
    pad = HEAD_PAD - (MLA_NOPE + MLA_ROPE)
    w_uq = mla_w_uq[l].reshape(MLA_Q_LORA, N_HEADS_MLA, MLA_NOPE + MLA_ROPE)
    w_uq = jnp.pad(w_uq, ((0, 0), (0, 0), (0, pad))).reshape(MLA_Q_LORA, N_HEADS_MLA * HEAD_PAD)
    w_uk = jnp.pad(mla_w_uk[l], ((0, 0), (0, 0), (0, HEAD_PAD - MLA_NOPE)))
    w_uvt = jnp.pad(jnp.transpose(mla_w_uv[l], (1, 2, 0)), ((0, 0), (0, VT_ROWS - MLA_V), (0, 0)))
    v_one = jnp.zeros((N_HEADS_MLA, VT_ROWS, 1), F32).at[:, MLA_V, :].set(1.0)
    return {
        "w_uvt": w_uvt.reshape(N_HEADS_MLA * VT_ROWS, MLA_KV_LORA).astype(BF16),
        "v_one": v_one.reshape(N_HEADS_MLA * VT_ROWS, 1),
        "norm_w": norm_w[l],
        "w_in": _permute_w_in(w_in[l]),
        "q_norm": mla_q_norm[l].reshape(1, -1),
        "w_uq": w_uq.astype(BF16),
        "kv_norm": mla_kv_norm[l].reshape(1, -1),
        "w_uk": w_uk.reshape(MLA_KV_LORA, N_HEADS_MLA * HEAD_PAD).astype(BF16),
        "w_uv": mla_w_uv[l].reshape(MLA_KV_LORA, N_HEADS_MLA * MLA_V).astype(BF16),
        "m_gate_b": mlstm_gate_b[l],
        "m_norm": mlstm_norm[l].reshape(1, -1),
        "g_conv_w": gdn_conv_w[l],
        "g_a_log": gdn_a_log[l].reshape(1, -1),
        "g_dt_bias": gdn_dt_bias[l].reshape(1, -1),
        "g_norm": gdn_norm[l].reshape(1, -1),
        "w_out": w_out[l].astype(BF16),
    }


def _recurrent_groups(proj3, lw, m_state, g_state, bb, t, shared_init):
    mm, m_state = _mlstm(proj3, lw, m_state, bb, t, shared_init)
    mg, g_state = _gdn(proj3, lw, g_state, bb, t, shared_init)
    return mm, mg, m_state, g_state


def kernel(x_prompt, x_sample, cache_mla_latent, cache_mla_krope, state_mlstm_C, state_mlstm_n, state_mlstm_m, state_gdn_S, state_gdn_conv, meta_tokens, norm_w, w_in, mla_q_norm, mla_w_uq, mla_kv_norm, mla_w_uk, mla_w_uv, mlstm_gate_b, mlstm_norm, gdn_conv_w, gdn_a_log, gdn_dt_bias, gdn_norm, w_out, final_norm):
    nb, seq, d = x_prompt.shape
    ns, dseq, _ = x_sample.shape
    n_meta = meta_tokens.shape[0]
    n_cache = cache_mla_latent.shape[2]
    depth = norm_w.shape[0]
    assert seq % 256 == 0 and n_meta % 8 == 0 and dseq % 8 == 0 and n_meta <= CHUNK and dseq <= CHUNK

    tabs_m = _rope_tables(0, n_meta)
    tabs_p = _rope_tables(n_meta, seq)
    tabs_s = _rope_tables(n_cache, dseq)

    h_m = meta_tokens.astype(F32)
    h_p = x_prompt.reshape(nb * seq, d)
    h_s = x_sample.reshape(ns * dseq, d)
    zero_m = (jnp.zeros((1, R_HEADS, R_DH, R_DH), F32), jnp.zeros((1, R_HEADS, R_DH), F32),
              jnp.zeros((1, R_HEADS), F32))
    zero_g = (jnp.zeros((1, R_HEADS, R_DH, R_DH), F32), jnp.zeros((1, CONV_W - 1, 3 * R_WIDTH), F32))
    bb_p = 2 if nb % 2 == 0 else 1
    bb_s = 4 if ns % 4 == 0 else 1
    p_rows, s_rows = [], []
    for l in range(depth):
        lw = _layer_weights(l, norm_w, w_in, mla_q_norm, mla_w_uq, mla_kv_norm, mla_w_uk, mla_w_uv,
                            mlstm_gate_b, mlstm_norm, gdn_conv_w, gdn_a_log, gdn_dt_bias, gdn_norm, w_out)
        last = l == depth - 1

        proj_m, q_m, k_m, v_m, c_m, kr_m = _inproj_mla(h_m, 1, tabs_m, lw)
        proj_m3 = proj_m.reshape(1, n_meta, NP)
        oa_m = _attention(q_m, k_m, v_m, None, n_meta, n_meta)
        mm_m, mg_m, mst, gst = _recurrent_groups(proj_m3, lw, zero_m, zero_g, 1, n_meta, True)
        if not last:
            h_m = _outproj(oa_m.reshape(n_meta, -1), proj_m, mm_m.reshape(n_meta, -1),
                           mg_m.reshape(n_meta, -1), h_m, lw["w_out"], None)

        proj_p, q_p, k_p, v_p, c_p, kr_p = _inproj_mla(h_p, nb, tabs_p, lw)
        proj_p3 = proj_p.reshape(nb, seq, NP)
        oa_p = _attention(q_p, k_p, v_p, (k_m, v_m), 256, 256)
        mm_p, mg_p, mst, gst = _recurrent_groups(proj_p3, lw, mst, gst, bb_p, R_CHUNK, True)
        h_p = _outproj(oa_p.reshape(nb * seq, -1), proj_p, mm_p.reshape(nb * seq, -1),
                       mg_p.reshape(nb * seq, -1), h_p, lw["w_out"], final_norm if last else None)
        p_rows.append((
            jnp.concatenate([jnp.broadcast_to(c_m, (nb,) + c_m.shape[1:]), c_p], axis=1),
            jnp.concatenate([jnp.broadcast_to(kr_m, (nb,) + kr_m.shape[1:]), kr_p], axis=1),
            mst[0], mst[1], mst[2], gst[0], gst[1]))

        proj_s = _inproj(h_s, lw["norm_w"], lw["w_in"])
        proj_s3 = proj_s.reshape(ns, dseq, NP)
        oa_s, c_s, kr_s = _sample_attention(proj_s3, cache_mla_latent, cache_mla_krope, l, tabs_s, lw)
        mm_s, mg_s, sm_st, sg_st = _recurrent_groups(
            proj_s3, lw, (state_mlstm_C[l], state_mlstm_n[l], state_mlstm_m[l]),
            (state_gdn_S[l], state_gdn_conv[l]), bb_s, dseq, False)
        h_s = _outproj(oa_s.reshape(ns * dseq, -1), proj_s, mm_s.reshape(ns * dseq, -1),
                       mg_s.reshape(ns * dseq, -1), h_s, lw["w_out"], final_norm if last else None)
        s_rows.append((c_s, kr_s, sm_st[0], sm_st[1], sm_st[2], sg_st[0], sg_st[1]))

    y_prompt = h_p.reshape(nb, seq, d)
    y_sample = h_s.reshape(ns, dseq, d)
    stack = lambda rows, i: jnp.stack([r[i] for r in rows])
    return ((y_prompt, y_sample) + tuple(stack(p_rows, i) for i in range(7))
            + tuple(stack(s_rows, i) for i in range(7)))
```

```python
import functools
import math

import jax
import jax.numpy as jnp
import numpy as np
from jax import lax
from jax.experimental import pallas as pl
from jax.experimental.pallas import tpu as pltpu

F32 = jnp.float32
BF16 = jnp.bfloat16

EPS = 1e-6
ROPE_BASE = 10000.0
CHUNK = 64
N_HEADS_MLA = 8
MLA_NOPE, MLA_ROPE, MLA_V = 64, 32, 64
MLA_Q_LORA, MLA_KV_LORA = 384, 256
R_HEADS, R_DH = 4, 128
R_WIDTH = R_HEADS * R_DH
CONV_W = 4
TRI_BLOCK = 16
R_CHUNK = 128
LANES = 128
HEAD_PAD = 128
VT_ROWS = 80
SOFTMAX_SCALE = 1.0 / math.sqrt(MLA_NOPE + MLA_ROPE)
LOG2E = math.log2(math.e)
QK_LOOKAHEAD = 4

OFF_GQKV = 0
OFF_MQKV = 1536
OFF_MOZ = 3072
OFF_ZA = 4096
OFF_ZG = 4608
OFF_CKV = 5120
OFF_CQ = 5376
OFF_SMALL = 5760
NP = 5888
SM_KR, SM_MI, SM_MF, SM_GA, SM_GB = 0, 32, 36, 40, 44

VMEM_LIMIT = 56 * 1024 * 1024


def _cparams(sem):
    return pltpu.CompilerParams(dimension_semantics=sem, vmem_limit_bytes=VMEM_LIMIT)


def _bf(x):
    return x.astype(BF16)


def _dot(a, b):
    return jnp.dot(a, b, preferred_element_type=F32)


def _dot_nt(a, b):
    return lax.dot_general(a, b, (((1,), (1,)), ((), ())), preferred_element_type=F32)


def _dot_tn(a, b):
    return lax.dot_general(a, b, (((0,), (0,)), ((), ())), preferred_element_type=F32)


def _rms(x, w):
    return x * lax.rsqrt(jnp.mean(x * x, axis=-1, keepdims=True) + EPS) * w


def _silu(x):
    return x * jax.nn.sigmoid(x)


def _rope128(x, tc, ts1, ts2):
    return x * tc + pltpu.roll(x, 16, 1) * ts1 + pltpu.roll(x, LANES - 16, 1) * ts2


def _inproj_kernel(x_ref, nw_ref, w_ref, o_ref):
    xn = _bf(_rms(x_ref[...], nw_ref[...]))
    n0 = 0
    while n0 < NP:
        n1 = min(n0 + 512, NP)
        o_ref[:, n0:n1] = _dot(xn, w_ref[:, n0:n1])
        n0 = n1


def _inproj(x2d, norm_w, w_bf):
    rows, d = x2d.shape
    tm = min(rows, 512)
    return pl.pallas_call(
        _inproj_kernel,
        grid=(rows // tm,),
        in_specs=[
            pl.BlockSpec((tm, d), lambda i: (i, 0)),
            pl.BlockSpec((1, d), lambda i: (0, 0)),
            pl.BlockSpec((d, NP), lambda i: (0, 0), pipeline_mode=pl.Buffered(1)),
        ],
        out_specs=pl.BlockSpec((tm, NP), lambda i: (i, 0)),
        out_shape=jax.ShapeDtypeStruct((rows, NP), F32),
        compiler_params=_cparams(("arbitrary",)),
        name="inproj",
    )(x2d, norm_w.reshape(1, d), w_bf)


def _inproj_mla_kernel(x_ref, nw_ref, w_ref, tc_ref, ts1_ref, ts2_ref, qn_ref, wuq_ref, kvn_ref,
                       wuk_ref, wuvt_ref, vone_ref, o_ref, q_out, k_out, vt_out, cn_out, kr_out):
    _inproj_kernel(x_ref, nw_ref, w_ref, o_ref)
    tc, ts1, ts2 = tc_ref[...], ts1_ref[...], ts2_ref[...]
    cq = o_ref[:, OFF_CQ:OFF_CQ + MLA_Q_LORA]
    q = _dot(_bf(_rms(cq, qn_ref[...])), wuq_ref[...])
    cn = _rms(o_ref[:, OFF_CKV:OFF_CKV + MLA_KV_LORA], kvn_ref[...])
    cn_out[...] = cn
    cnb = _bf(cn)
    kn = _dot(cnb, wuk_ref[...])
    vt_out[0] = _bf(_dot_nt(wuvt_ref[...], cnb) + vone_ref[...])
    sm = o_ref[:, OFF_SMALL:OFF_SMALL + LANES]
    lane = lax.broadcasted_iota(jnp.int32, sm.shape, 1)
    kr = pltpu.roll(jnp.where(lane < MLA_ROPE, sm, 0.0), MLA_NOPE, 1)
    kr = _rope128(kr, tc, ts1, ts2)
    kr_out[...] = pltpu.roll(kr, LANES - MLA_NOPE, 1)[:, :MLA_ROPE]
    for h in range(N_HEADS_MLA):
        sl = slice(h * HEAD_PAD, (h + 1) * HEAD_PAD)
        q_out[:, sl] = _bf(_rope128(q[:, sl], tc, ts1, ts2) * (SOFTMAX_SCALE * LOG2E))
        k_out[:, sl] = _bf(kn[:, sl] + kr)


def _inproj_mla(x2d, nb, tabs, lw):
    rows, d = x2d.shape
    seq = rows // nb
    tm = min(seq, 512)
    tpb = seq // tm
    hq = N_HEADS_MLA * HEAD_PAD
    hvt = N_HEADS_MLA * VT_ROWS
    const = lambda i: (0, 0)
    row_tile = lambda i: (i, 0)
    tab_tile = lambda i: (i % tpb, 0)
    outs = pl.pallas_call(
        _inproj_mla_kernel,
        grid=(rows // tm,),
        in_specs=[
            pl.BlockSpec((tm, d), row_tile),
            pl.BlockSpec((1, d), const),
            pl.BlockSpec((d, NP), const, pipeline_mode=pl.Buffered(1)),
            pl.BlockSpec((tm, LANES), tab_tile),
            pl.BlockSpec((tm, LANES), tab_tile),
            pl.BlockSpec((tm, LANES), tab_tile),
            pl.BlockSpec((1, MLA_Q_LORA), const),
            pl.BlockSpec((MLA_Q_LORA, hq), const),
            pl.BlockSpec((1, MLA_KV_LORA), const),
            pl.BlockSpec((MLA_KV_LORA, hq), const),
            pl.BlockSpec((hvt, MLA_KV_LORA), const),
            pl.BlockSpec((hvt, 1), const),
        ],
        out_specs=[
            pl.BlockSpec((tm, NP), row_tile),
            pl.BlockSpec((tm, hq), row_tile),
            pl.BlockSpec((tm, hq), row_tile),
            pl.BlockSpec((1, hvt, tm), lambda i: (i // tpb, 0, i % tpb)),
            pl.BlockSpec((tm, MLA_KV_LORA), row_tile),
            pl.BlockSpec((tm, MLA_ROPE), row_tile),
        ],
        out_shape=[
            jax.ShapeDtypeStruct((rows, NP), F32),
            jax.ShapeDtypeStruct((rows, hq), BF16),
            jax.ShapeDtypeStruct((rows, hq), BF16),
            jax.ShapeDtypeStruct((nb, hvt, seq), BF16),
            jax.ShapeDtypeStruct((rows, MLA_KV_LORA), F32),
            jax.ShapeDtypeStruct((rows, MLA_ROPE), F32),
        ],
        compiler_params=_cparams(("arbitrary",)),
        name="inproj_mla",
    )(x2d, lw["norm_w"].reshape(1, d), lw["w_in"], *tabs, lw["q_norm"], lw["w_uq"], lw["kv_norm"],
      lw["w_uk"], lw["w_uvt"], lw["v_one"])
    proj, q, k, vt, cn, kr = outs
    return (proj, q.reshape(nb, seq, hq), k.reshape(nb, seq, hq), vt,
            cn.reshape(nb, seq, MLA_KV_LORA), kr.reshape(nb, seq, MLA_ROPE))


def _attn_kernel(*refs, tq, tk, has_prefix, single_tile):
    if has_prefix:
        q_ref, k_ref, vt_ref, kp_ref, vtp_ref, o_ref, m_s, acc_s, st_s = refs
    else:
        q_ref, k_ref, vt_ref, o_ref, m_s, acc_s, st_s = refs
    i = pl.program_id(1)
    q0 = i * tq
    nfull = q0 // tk
    hsl = [slice(h * HEAD_PAD, (h + 1) * HEAD_PAD) for h in range(N_HEADS_MLA)]
    rsl = [slice(h * VT_ROWS, (h + 1) * VT_ROWS) for h in range(N_HEADS_MLA)]

    def scores(h, kt):
        return _dot_nt(kt, q_ref[0, :, hsl[h]])

    def softmax_pv(h, st, vt, mask):
        if mask is not None:
            st = jnp.where(mask, st, -jnp.inf)
        m_old = m_s[h]
        m_new = jnp.maximum(m_old, jnp.max(st, axis=0, keepdims=True))
        p = jnp.exp2(st - m_new)
        acc_s[h] = jnp.exp2(m_old - m_new) * acc_s[h] + _dot(vt, _bf(p))
        m_s[h] = m_new

    def tile(r0, mask=None, r0_next=None):
        st = [None] * N_HEADS_MLA
        for h in range(N_HEADS_MLA):
            ahead = h + QK_LOOKAHEAD
            if ahead < N_HEADS_MLA:
                st[ahead] = scores(ahead, k_ref[0, pl.ds(r0, tk), hsl[ahead]])
            cur = st_s[h] if h < QK_LOOKAHEAD else st[h]
            if ahead >= N_HEADS_MLA and r0_next is not None:
                hn = ahead - N_HEADS_MLA
                st_s[hn] = scores(hn, k_ref[0, pl.ds(r0_next, tk), hsl[hn]])
            softmax_pv(h, cur, vt_ref[0, rsl[h], pl.ds(r0, tk)], mask)

    for h in range(N_HEADS_MLA):
        m_s[h] = jnp.full((1, tq), -jnp.inf, F32)
        acc_s[h] = jnp.zeros((VT_ROWS, tq), F32)
    if has_prefix:
        st_pre = [scores(h, kp_ref[0, :, hsl[h]]) for h in range(N_HEADS_MLA)]
    for h in range(QK_LOOKAHEAD):
        st_s[h] = scores(h, k_ref[0, 0:tk, hsl[h]])
    if has_prefix:
        for h in range(N_HEADS_MLA):
            softmax_pv(h, st_pre[h], vtp_ref[0, rsl[h], :], None)

    def full_tile(j):
        tile(pl.multiple_of(j * tk, tk), None, pl.multiple_of((j + 1) * tk, tk))

    def body(j, c):
        full_tile(2 * j)
        full_tile(2 * j + 1)
        return c

    if single_tile:
        r0 = 0
    else:
        lax.fori_loop(0, nfull // 2, body, 0)

        @pl.when(nfull % 2 == 1)
        def _():
            full_tile(nfull - 1)

        r0 = pl.multiple_of(nfull * tk, tk)
    k_chunk = (nfull * tk + lax.broadcasted_iota(jnp.int32, (tk, tq), 0)) // CHUNK
    q_chunk = (q0 + lax.broadcasted_iota(jnp.int32, (tk, tq), 1)) // CHUNK
    tile(r0, k_chunk <= q_chunk)
    for hp in range(N_HEADS_MLA // 2):
        pair = []
        for h in (2 * hp, 2 * hp + 1):
            acc = acc_s[h]
            pair.append(acc[0:MLA_V] / acc[MLA_V:MLA_V + 1])
        o_ref[0, :, hp * LANES:(hp + 1) * LANES] = _bf(jnp.concatenate(pair, axis=0).T)


def _attention(q, k, vt, prefix, tq, tk):
    nb, seq, hq = q.shape
    hvt = vt.shape[1]
    in_specs = [
        pl.BlockSpec((1, tq, hq), lambda b, i: (b, i, 0)),
        pl.BlockSpec((1, seq, hq), lambda b, i: (b, 0, 0)),
        pl.BlockSpec((1, hvt, seq), lambda b, i: (b, 0, 0)),
    ]
    args = [q, k, vt]
    if prefix is not None:
        kp, vtp = prefix
        npre = kp.shape[1]
        in_specs += [pl.BlockSpec((1, npre, hq), lambda b, i: (0, 0, 0)),
                     pl.BlockSpec((1, hvt, npre), lambda b, i: (0, 0, 0))]
        args += [kp, vtp]
    return pl.pallas_call(
        functools.partial(_attn_kernel, tq=tq, tk=tk, has_prefix=prefix is not None,
                          single_tile=seq == tk),
        grid=(nb, seq // tq),
        in_specs=in_specs,
        out_specs=pl.BlockSpec((1, tq, N_HEADS_MLA * MLA_V), lambda b, i: (b, i, 0)),
        out_shape=jax.ShapeDtypeStruct((nb, seq, N_HEADS_MLA * MLA_V), BF16),
        scratch_shapes=[pltpu.VMEM((N_HEADS_MLA, 1, tq), F32),
                        pltpu.VMEM((N_HEADS_MLA, VT_ROWS, tq), F32),
                        pltpu.VMEM((QK_LOOKAHEAD, tk, tq), F32)],
        compiler_params=_cparams(("arbitrary", "arbitrary")),
        name="attn",
    )(*args)


def _sample_attn_kernel(cq_ref, ckv_ref, sm_ref, cc_ref, kc_ref, tc_ref, ts1_ref, ts2_ref, qn_ref,
                        wuq_ref, kvn_ref, wuk_ref, wuv_ref, o_ref, cn_out, kr_out, *, n_cache, tk):
    tc, ts1, ts2 = tc_ref[...], ts1_ref[...], ts2_ref[...]
    t = cq_ref.shape[1]
    q = _dot(_bf(_rms(cq_ref[0], qn_ref[...])), wuq_ref[...])
    cn = _rms(ckv_ref[0], kvn_ref[...])
    cn_out[0] = cn
    sm = sm_ref[0]
    lane = lax.broadcasted_iota(jnp.int32, sm.shape, 1)
    kr = pltpu.roll(jnp.where(lane < MLA_ROPE, sm, 0.0), MLA_NOPE, 1)
    kr = _rope128(kr, tc, ts1, ts2)
    krn = pltpu.roll(kr, LANES - MLA_NOPE, 1)[:, :MLA_ROPE]
    kr_out[0] = krn
    ql, qr = [], []
    for h in range(N_HEADS_MLA):
        sl = slice(h * HEAD_PAD, (h + 1) * HEAD_PAD)
        qh = _rope128(q[:, sl], tc, ts1, ts2)
        ql.append(_dot_nt(_bf(qh), wuk_ref[:, sl]))
        qr.append(pltpu.roll(qh, LANES - MLA_NOPE, 1)[:, :MLA_ROPE])
    qlat = _bf(jnp.concatenate(ql, axis=0) * (SOFTMAX_SCALE * LOG2E))
    qrope = _bf(jnp.concatenate(qr, axis=0) * (SOFTMAX_SCALE * LOG2E))

    bounds = [(r0, min(r0 + tk, n_cache)) for r0 in range(0, n_cache, tk)]
    c_bf = [_bf(cc_ref[0, r0:r1, :]) for r0, r1 in bounds] + [_bf(cn)]
    k_bf = [_bf(kc_ref[0, r0:r1, :]) for r0, r1 in bounds] + [_bf(krn)]
    s = [_dot_nt(qlat, c) + _dot_nt(qrope, k) for c, k in zip(c_bf, k_bf)]

    def lane_blocks(x):
        w = x.shape[1]
        return [x[:, i:i + LANES] for i in range(0, w, LANES)] if w % LANES == 0 else None

    def folded(xs, op):
        wide = [b for x in xs if lane_blocks(x) for b in lane_blocks(x)]
        narrow = [x for x in xs if lane_blocks(x) is None]
        return ([functools.reduce(op, wide)] if wide else []) + narrow

    m = functools.reduce(jnp.maximum, [jnp.max(x, axis=1, keepdims=True) for x in folded(s, jnp.maximum)])
    p = [jnp.exp2(x - m) for x in s]
    l = functools.reduce(jnp.add, [jnp.sum(x, axis=1, keepdims=True) for x in folded(p, jnp.add)])
    acc = functools.reduce(jnp.add, [_dot(_bf(x), c) for x, c in zip(p, c_bf)])
    olat = _bf(acc / l)
    out_lane = lax.broadcasted_iota(jnp.int32, (t, N_HEADS_MLA * MLA_V), 1) // MLA_V
    o = jnp.zeros((t, N_HEADS_MLA * MLA_V), F32)
    for h in range(N_HEADS_MLA):
        o = jnp.where(out_lane == h, _dot(olat[h * t:(h + 1) * t], wuv_ref[...]), o)
    o_ref[0] = _bf(o)


def _sample_attention(proj3, cache_c, cache_kr, layer, tabs, lw):
    nb, t, _ = proj3.shape
    n_cache = cache_c.shape[2]
    hq = N_HEADS_MLA * HEAD_PAD
    hv = N_HEADS_MLA * MLA_V
    const = lambda b: (0, 0)
    return pl.pallas_call(
        functools.partial(_sample_attn_kernel, n_cache=n_cache, tk=min(512, n_cache)),
        grid=(nb,),
        in_specs=[
            pl.BlockSpec((1, t, MLA_Q_LORA), lambda b: (b, 0, OFF_CQ // MLA_Q_LORA)),
            pl.BlockSpec((1, t, MLA_KV_LORA), lambda b: (b, 0, OFF_CKV // MLA_KV_LORA)),
            pl.BlockSpec((1, t, LANES), lambda b: (b, 0, OFF_SMALL // LANES)),
            pl.BlockSpec((None, 1, n_cache, MLA_KV_LORA), lambda b: (layer, b, 0, 0)),
            pl.BlockSpec((None, 1, n_cache, MLA_ROPE), lambda b: (layer, b, 0, 0)),
            pl.BlockSpec((t, LANES), const),
            pl.BlockSpec((t, LANES), const),
            pl.BlockSpec((t, LANES), const),
            pl.BlockSpec((1, MLA_Q_LORA), const),
            pl.BlockSpec((MLA_Q_LORA, hq), const),
            pl.BlockSpec((1, MLA_KV_LORA), const),
            pl.BlockSpec((MLA_KV_LORA, hq), const),
            pl.BlockSpec((MLA_KV_LORA, hv), const),
        ],
        out_specs=[
            pl.BlockSpec((1, t, hv), lambda b: (b, 0, 0)),
            pl.BlockSpec((1, t, MLA_KV_LORA), lambda b: (b, 0, 0)),
            pl.BlockSpec((1, t, MLA_ROPE), lambda b: (b, 0, 0)),
        ],
        out_shape=[
            jax.ShapeDtypeStruct((nb, t, hv), BF16),
            jax.ShapeDtypeStruct((nb, t, MLA_KV_LORA), F32),
            jax.ShapeDtypeStruct((nb, t, MLA_ROPE), F32),
        ],
        compiler_params=_cparams(("arbitrary",)),
        name="sample_attn",
    )(proj3, proj3, proj3, cache_c, cache_kr, *tabs, lw["q_norm"], lw["w_uq"], lw["kv_norm"],
      lw["w_uk"], lw["w_uv"])


def _chunk_masks(t):
    row = lax.broadcasted_iota(jnp.int32, (t, t), 0)
    col = lax.broadcasted_iota(jnp.int32, (t, t), 1)
    return row == col, col <= row, row <= col, col < row


def _to_row(col_vec, eye):
    return jnp.sum(jnp.where(eye, col_vec, 0.0), axis=0, keepdims=True)


def _mlstm_kernel(qkv_ref, oz_ref, sm_ref, gb_ref, nrm_ref, c0_ref, n0_ref, m0_ref,
                  out_ref, c_ref, n_ref, m_ref, *, bb, t, shared_init):
    @pl.when(pl.program_id(1) == 0)
    def _():
        for b in range(bb):
            src = 0 if shared_init else b
            c_ref[b] = c0_ref[src]
            n_ref[b] = n0_ref[src]
            m_ref[0, b:b + 1, :] = m0_ref[0, src:src + 1, :]

    eye, tril, triu, _ = _chunk_masks(t)
    chains = [(b, h) for b in range(bb) for h in range(R_HEADS)]
    i_col = [sm_ref[b, :, SM_MI + h:SM_MI + h + 1] + gb_ref[0:1, h:h + 1] for b, h in chains]
    f_col = [jax.nn.log_sigmoid(sm_ref[b, :, SM_MF + h:SM_MF + h + 1] + gb_ref[1:2, h:h + 1])
             for b, h in chains]
    i_row = [_to_row(x, eye) for x in i_col]
    f_row = [_to_row(x, eye) for x in f_col]
    b_col = [jnp.sum(jnp.where(tril, x, 0.0), axis=1, keepdims=True) for x in f_row]
    b_row = [jnp.sum(jnp.where(triu, x, 0.0), axis=0, keepdims=True) for x in f_col]
    dmat = [jnp.where(tril, bc - br + ir, -jnp.inf) for bc, br, ir in zip(b_col, b_row, i_row)]
    dmax = [jnp.max(x, axis=1, keepdims=True) for x in dmat]
    pre = []
    for (b, h), ic, bc, dm, dx in zip(chains, i_col, b_col, dmat, dmax):
        hs = slice(h * R_DH, (h + 1) * R_DH)
        q = qkv_ref[b, :, hs]
        k = qkv_ref[b, :, R_WIDTH + h * R_DH:R_WIDTH + (h + 1) * R_DH] * (R_DH ** -0.5)
        v = qkv_ref[b, :, 2 * R_WIDTH + h * R_DH:2 * R_WIDTH + (h + 1) * R_DH]
        m_prev = m_ref[0, b:b + 1, h:h + 1]
        inter = bc + m_prev
        m_t = jnp.maximum(inter, dx)
        m_new = m_t[t - 1:t, :]
        b_last = bc[t - 1:t, :]
        pre.append(dict(
            q=q, k=k, v=v, qb=_bf(q), kb=_bf(k), vb=_bf(v), m_t=m_t, m_new=m_new,
            w_inter=jnp.exp(inter - m_t), e=jnp.exp(dm - m_t),
            g_state=jnp.exp(b_last + m_prev - m_new),
            g_tok=jnp.exp(b_last - bc + ic - m_new)))
    qk_raw = [_dot_nt(p["qb"], p["kb"]) for p in pre]
    qc = [_dot_nt(p["qb"], _bf(c_ref[b, h])) for p, (b, h) in zip(pre, chains)]
    qk = [r * p["e"] for r, p in zip(qk_raw, pre)]
    pv = [_dot(_bf(x), p["vb"]) for x, p in zip(qk, pre)]
    upd = [_dot_tn(_bf(p["g_tok"] * p["v"]), p["kb"]) for p in pre]
    nvec = [n_ref[b, h:h + 1, :] for b, h in chains]
    qn = [jnp.sum(p["q"] * nv, axis=1, keepdims=True) for p, nv in zip(pre, nvec)]
    qks = [jnp.sum(x, axis=1, keepdims=True) for x in qk]
    hm = []
    for (b, h), p, qn_c, qks_c, qc_c, pv_c in zip(chains, pre, qn, qks, qc, pv):
        num = p["w_inter"] * qc_c + pv_c
        den = p["w_inter"] * qn_c + qks_c
        hh = num / jnp.maximum(jnp.abs(den), jnp.exp(-p["m_t"]))
        hm.append(jax.nn.sigmoid(oz_ref[b, :, h * R_DH:(h + 1) * R_DH]) * hh)
    ms = [jnp.mean(x * x, axis=-1, keepdims=True) for x in hm]
    for (b, h), p, nv, upd_c, hm_c, ms_c in zip(chains, pre, nvec, upd, hm, ms):
        hs = slice(h * R_DH, (h + 1) * R_DH)
        c_ref[b, h] = p["g_state"] * c_ref[b, h] + upd_c
        n_ref[b, h:h + 1, :] = p["g_state"] * nv + jnp.sum(p["g_tok"] * p["k"], axis=0, keepdims=True)
        m_ref[0, b:b + 1, h:h + 1] = p["m_new"]
        om = hm_c * lax.rsqrt(ms_c + EPS) * nrm_ref[0:1, hs]
        out_ref[b, :, hs] = _bf(om * _silu(oz_ref[b, :, R_WIDTH + h * R_DH:R_WIDTH + (h + 1) * R_DH]))


def _mlstm(proj3, lw, state, bb, t, shared_init):
    nb, seq, _ = proj3.shape
    c0, n0, m0 = state
    sb = 1 if shared_init else bb
    m0 = m0.reshape(-1, sb, R_HEADS)
    st = (lambda i, c: (0, 0, 0, 0)) if shared_init else (lambda i, c: (i, 0, 0, 0))
    st3 = (lambda i, c: (0, 0, 0)) if shared_init else (lambda i, c: (i, 0, 0))
    out, c1, n1, m1 = pl.pallas_call(
        functools.partial(_mlstm_kernel, bb=bb, t=t, shared_init=shared_init),
        grid=(nb // bb, seq // t),
        in_specs=[
            pl.BlockSpec((bb, t, 3 * R_WIDTH), lambda i, c: (i, c, OFF_MQKV // (3 * R_WIDTH))),
            pl.BlockSpec((bb, t, 2 * R_WIDTH), lambda i, c: (i, c, OFF_MOZ // (2 * R_WIDTH))),
            pl.BlockSpec((bb, t, LANES), lambda i, c: (i, c, OFF_SMALL // LANES)),
            pl.BlockSpec((2, R_HEADS), lambda i, c: (0, 0)),
            pl.BlockSpec((1, R_WIDTH), lambda i, c: (0, 0)),
            pl.BlockSpec((sb, R_HEADS, R_DH, R_DH), st),
            pl.BlockSpec((sb, R_HEADS, R_DH), st3),
            pl.BlockSpec((1, sb, R_HEADS), st3),
        ],
        out_specs=[
            pl.BlockSpec((bb, t, R_WIDTH), lambda i, c: (i, c, 0)),
            pl.BlockSpec((bb, R_HEADS, R_DH, R_DH), lambda i, c: (i, 0, 0, 0)),
            pl.BlockSpec((bb, R_HEADS, R_DH), lambda i, c: (i, 0, 0)),
            pl.BlockSpec((1, bb, R_HEADS), lambda i, c: (i, 0, 0)),
        ],
        out_shape=[
            jax.ShapeDtypeStruct((nb, seq, R_WIDTH), BF16),
            jax.ShapeDtypeStruct((nb, R_HEADS, R_DH, R_DH), F32),
            jax.ShapeDtypeStruct((nb, R_HEADS, R_DH), F32),
            jax.ShapeDtypeStruct((nb // bb, bb, R_HEADS), F32),
        ],
        compiler_params=_cparams(("arbitrary", "arbitrary")),
        name="mlstm",
    )(proj3, proj3, proj3, lw["m_gate_b"], lw["m_norm"], c0, n0, m0)
    return out, (c1, n1, m1.reshape(nb, R_HEADS))


def _neumann_all(a_list, t, nil):
    levels = int(math.log2(nil)) - 1
    n_acc = [-a for a in a_list]
    pw = [_dot(_bf(a), _bf(a)) for a in a_list]
    for _ in range(levels - 1):
        r = [_dot(_bf(jnp.concatenate([p, n], axis=0)), _bf(p)) for p, n in zip(pw, n_acc)]
        n_acc = [n + p + x[t:] for n, p, x in zip(n_acc, pw, r)]
        pw = [x[:t] for x in r]
    r = [_dot(_bf(n), _bf(p)) for p, n in zip(pw, n_acc)]
    return [n + p + x for n, p, x in zip(n_acc, pw, r)]


def _unit_lower_inverse_all(a_list, t):
    blk = min(t, TRI_BLOCK)
    row = lax.broadcasted_iota(jnp.int32, (t, t), 0)
    col = lax.broadcasted_iota(jnp.int32, (t, t), 1)
    shift = int(math.log2(blk))
    same = jnp.right_shift(row, shift) == jnp.right_shift(col, shift)
    n_list = _neumann_all([jnp.where(same, a, 0.0) for a in a_list], t, blk)
    if blk == t:
        return n_list
    eye = (row == col).astype(F32)
    w_list = [n + eye for n in n_list]
    while blk < t:
        inner = ((jnp.right_shift(row, shift + 1) == jnp.right_shift(col, shift + 1))
                 & (jnp.right_shift(row, shift) > jnp.right_shift(col, shift)))
        z = [_dot(_bf(jnp.where(inner, a, 0.0)), _bf(w)) for a, w in zip(a_list, w_list)]
        w_list = [w - _dot(_bf(w), _bf(x)) for w, x in zip(w_list, z)]
        blk *= 2
        shift += 1
    return [w - eye for w in w_list]


def _conv_silu_rows(xwin, r0, rb, cw_ref):
    xw = xwin[r0:r0 + rb + 8]
    conv = xw[8:8 + rb] * cw_ref[CONV_W - 1:CONV_W, :]
    for j in range(1, CONV_W):
        conv = conv + pltpu.roll(xw, j, 0)[8:8 + rb] * cw_ref[CONV_W - 1 - j:CONV_W - j, :]
    return _silu(conv)


def _gdn_kernel(x_ref, z_ref, sm_ref, cw_ref, alog_ref, dtb_ref, gn_ref, s0_ref, buf0_ref,
                out_ref, s_ref, buf_ref, xwin_ref, *, bb, t, shared_init):
    c = pl.program_id(1)

    @pl.when(c == 0)
    def _():
        for b in range(bb):
            src = 0 if shared_init else b
            s_ref[b] = s0_ref[src]
            xwin_ref[b, 0:8, :] = jnp.zeros((8, 3 * R_WIDTH), F32)
            xwin_ref[b, 8 - (CONV_W - 1):8, :] = buf0_ref[src]

    eye, tril, triu, strict = _chunk_masks(t)
    chains = [(b, h) for b in range(bb) for h in range(R_HEADS)]
    g_col = [-jnp.exp(alog_ref[0:1, h:h + 1])
             * jax.nn.softplus(sm_ref[b, :, SM_GA + h:SM_GA + h + 1] + dtb_ref[0:1, h:h + 1])
             for b, h in chains]
    g_row = [_to_row(x, eye) for x in g_col]
    gc_col = [jnp.sum(jnp.where(tril, x, 0.0), axis=1, keepdims=True) for x in g_row]
    gc_row = [jnp.sum(jnp.where(triu, x, 0.0), axis=0, keepdims=True) for x in g_col]
    acts = []
    for b in range(bb):
        xwin_ref[b, 8:8 + t, :] = x_ref[b]
        acts.append(_conv_silu_rows(xwin_ref.at[b], 0, t, cw_ref))
        tail = xwin_ref[b, 8 + t - (CONV_W - 1):8 + t, :]
        xwin_ref[b, 8 - (CONV_W - 1):8, :] = tail
        buf_ref[b] = tail
    gq = [acts[b][:, h * R_DH:(h + 1) * R_DH] for b, h in chains]
    gk = [acts[b][:, R_WIDTH + h * R_DH:R_WIDTH + (h + 1) * R_DH] for b, h in chains]
    q_ss = [jnp.sum(x * x, axis=1, keepdims=True) for x in gq]
    k_ss = [jnp.sum(x * x, axis=1, keepdims=True) for x in gk]
    pre = []
    for i, (b, h) in enumerate(chains):
        gv = acts[b][:, 2 * R_WIDTH + h * R_DH:2 * R_WIDTH + (h + 1) * R_DH]
        qn = gq[i] * lax.rsqrt(q_ss[i] + EPS) * (R_DH ** -0.5)
        kn = gk[i] * lax.rsqrt(k_ss[i] + EPS)
        beta = jax.nn.sigmoid(sm_ref[b, :, SM_GB + h:SM_GB + h + 1])
        eg = jnp.exp(gc_col[i])
        g_last = gc_col[i][t - 1:t, :]
        pre.append(dict(
            qn=qn, kn=kn, beta=beta, eg=eg, g_last=g_last,
            gam=jnp.exp(jnp.where(tril, gc_col[i] - gc_row[i], -jnp.inf)),
            rhs=jnp.concatenate([beta * gv, (beta * eg) * kn], axis=1),
            kdec=kn * jnp.exp(g_last - gc_col[i])))
    kq = [_dot_nt(_bf(jnp.concatenate([p["kn"], p["qn"]], axis=0)), _bf(p["kn"])) for p in pre]
    a_list = [jnp.where(strict, p["beta"] * x[:t] * p["gam"], 0.0) for p, x in zip(pre, kq)]
    n_inv = _unit_lower_inverse_all(a_list, t)
    sol = [p["rhs"] + _dot(_bf(n), _bf(p["rhs"])) for p, n in zip(pre, n_inv)]
    ws = [_dot(_bf(jnp.concatenate([s[:, R_DH:], p["qn"] * p["eg"]], axis=0)), _bf(s_ref[b, h]))
          for p, s, (b, h) in zip(pre, sol, chains)]
    delta = [_bf(s[:, :R_DH] - x[:t]) for s, x in zip(sol, ws)]
    o2 = [_dot(_bf(x[t:] * p["gam"]), d) for p, x, d in zip(pre, kq, delta)]
    upd = [_dot_tn(_bf(p["kdec"]), d) for p, d in zip(pre, delta)]
    o = [x[t:] + o2_c for x, o2_c in zip(ws, o2)]
    ms = [jnp.mean(x * x, axis=-1, keepdims=True) for x in o]
    for (b, h), p, o_c, ms_c, upd_c in zip(chains, pre, o, ms, upd):
        hs = slice(h * R_DH, (h + 1) * R_DH)
        s_ref[b, h] = jnp.exp(p["g_last"]) * s_ref[b, h] + upd_c
        out_ref[b, :, hs] = _bf(o_c * lax.rsqrt(ms_c + EPS) * gn_ref[...] * _silu(z_ref[b, :, hs]))


def _gdn(proj3, lw, state, bb, t, shared_init):
    nb, seq, _ = proj3.shape
    s0, buf0 = state
    sb = 1 if shared_init else bb
    st = (lambda i, c: (0, 0, 0, 0)) if shared_init else (lambda i, c: (i, 0, 0, 0))
    st3 = (lambda i, c: (0, 0, 0)) if shared_init else (lambda i, c: (i, 0, 0))
    const = lambda i, c: (0, 0)
    in_specs = [
        pl.BlockSpec((bb, t, 3 * R_WIDTH), lambda i, c: (i, c, OFF_GQKV // (3 * R_WIDTH))),
        pl.BlockSpec((bb, t, R_WIDTH), lambda i, c: (i, c, OFF_ZG // R_WIDTH)),
        pl.BlockSpec((bb, t, LANES), lambda i, c: (i, c, OFF_SMALL // LANES)),
        pl.BlockSpec((CONV_W, 3 * R_WIDTH), const),
        pl.BlockSpec((1, R_HEADS), const),
        pl.BlockSpec((1, R_HEADS), const),
        pl.BlockSpec((1, R_DH), const),
        pl.BlockSpec((sb, R_HEADS, R_DH, R_DH), st),
        pl.BlockSpec((sb, CONV_W - 1, 3 * R_WIDTH), st3),
    ]
    args = [proj3, proj3, proj3, lw["g_conv_w"], lw["g_a_log"], lw["g_dt_bias"], lw["g_norm"], s0, buf0]
    out_specs = [
        pl.BlockSpec((bb, t, R_WIDTH), lambda i, c: (i, c, 0)),
        pl.BlockSpec((bb, R_HEADS, R_DH, R_DH), lambda i, c: (i, 0, 0, 0)),
        pl.BlockSpec((bb, CONV_W - 1, 3 * R_WIDTH), lambda i, c: (i, 0, 0)),
    ]
    out_shape = [
        jax.ShapeDtypeStruct((nb, seq, R_WIDTH), BF16),
        jax.ShapeDtypeStruct((nb, R_HEADS, R_DH, R_DH), F32),
        jax.ShapeDtypeStruct((nb, CONV_W - 1, 3 * R_WIDTH), F32),
    ]
    out, s1, buf1 = pl.pallas_call(
        functools.partial(_gdn_kernel, bb=bb, t=t, shared_init=shared_init),
        grid=(nb // bb, seq // t),
        in_specs=in_specs,
        out_specs=out_specs,
        out_shape=out_shape,
        scratch_shapes=[pltpu.VMEM((bb, 8 + t, 3 * R_WIDTH), F32)],
        compiler_params=_cparams(("arbitrary", "arbitrary")),
        name="gdn",
    )(*args)
    return out, (s1, buf1)


def _outproj_kernel(*refs, final):
    if final:
        oa_ref, za_ref, mm_ref, mg_ref, x_ref, w_ref, fn_ref, y_ref = refs
    else:
        oa_ref, za_ref, mm_ref, mg_ref, x_ref, w_ref, y_ref = refs
    ma = oa_ref[...].astype(F32) * _silu(za_ref[...])
    acc = (_dot(_bf(ma), w_ref[0:R_WIDTH, :])
           + _dot(mm_ref[...], w_ref[R_WIDTH:2 * R_WIDTH, :])
           + _dot(mg_ref[...], w_ref[2 * R_WIDTH:3 * R_WIDTH, :]))
    hnew = x_ref[...] + acc
    y_ref[...] = _rms(hnew, fn_ref[...]) if final else hnew


def _outproj(oa, proj2, mm, mg, x2d, w_bf, final_norm):
    rows, d = x2d.shape
    tm = min(rows, 512)
    final = final_norm is not None
    in_specs = [
        pl.BlockSpec((tm, R_WIDTH), lambda i: (i, 0)),
        pl.BlockSpec((tm, R_WIDTH), lambda i: (i, OFF_ZA // R_WIDTH)),
        pl.BlockSpec((tm, R_WIDTH), lambda i: (i, 0)),
        pl.BlockSpec((tm, R_WIDTH), lambda i: (i, 0)),
        pl.BlockSpec((tm, d), lambda i: (i, 0)),
        pl.BlockSpec((3 * R_WIDTH, d), lambda i: (0, 0)),
    ]
    args = [oa, proj2, mm, mg, x2d, w_bf]
    if final:
        in_specs.append(pl.BlockSpec((1, d), lambda i: (0, 0)))
        args.append(final_norm.reshape(1, d))
    return pl.pallas_call(
        functools.partial(_outproj_kernel, final=final),
        grid=(rows // tm,),
        in_specs=in_specs,
        out_specs=pl.BlockSpec((tm, d), lambda i: (i, 0)),
        out_shape=jax.ShapeDtypeStruct((rows, d), F32),
        compiler_params=_cparams(("arbitrary",)),
        name="outproj",
    )(*args)


def _permute_w_in(w):
    d = w.shape[0]
    c_q, c_kv, k_r, z_a = w[:, 0:384], w[:, 384:640], w[:, 640:672], w[:, 672:1184]
    m_qkv, m_i, m_f = w[:, 1184:2720], w[:, 2720:2724], w[:, 2724:2728]
    m_oz = w[:, 2728:3752]
    g_qkv, g_a, g_b, z_g = w[:, 3752:5288], w[:, 5288:5292], w[:, 5292:5296], w[:, 5296:5808]
    small = jnp.concatenate([k_r, m_i, m_f, g_a, g_b, jnp.zeros((d, LANES - 48), w.dtype)], axis=1)
    return jnp.concatenate([g_qkv, m_qkv, m_oz, z_a, z_g, c_kv, c_q, small], axis=1).astype(BF16)


def _rope_tables(pos0, n):
    half = MLA_ROPE // 2
    freq = ROPE_BASE ** (-np.arange(half, dtype=np.float64) / half)
    ang = (pos0 + np.arange(n)).astype(np.float64)[:, None] * freq[None, :]
    cos, sin = np.cos(ang).astype(np.float32), np.sin(ang).astype(np.float32)
    one_lo = np.ones((n, MLA_NOPE), np.float32)
    one_hi = np.ones((n, LANES - MLA_NOPE - MLA_ROPE), np.float32)
    zero_lo = np.zeros((n, MLA_NOPE), np.float32)
    zero_hi = np.zeros((n, LANES - MLA_NOPE - MLA_ROPE), np.float32)
    zero_h = np.zeros((n, half), np.float32)
    tc = np.concatenate([one_lo, cos, cos, one_hi], axis=1)
    ts1 = np.concatenate([zero_lo, zero_h, sin, zero_hi], axis=1)
    ts2 = np.concatenate([zero_lo, -sin, zero_h, zero_hi], axis=1)
    return jnp.asarray(tc), jnp.asarray(ts1), jnp.asarray(ts2)


def _layer_weights(l, norm_w, w_in, mla_q_norm, mla_w_uq, mla_kv_norm, mla_w_uk, mla_w_uv, mlstm_gate_b,
                   mlstm_norm, gdn_conv_w, gdn_a_log, gdn_dt_bias, gdn_norm, w_out):
    pad = HEAD_PAD - (MLA_NOPE + MLA_ROPE)
    w_uq = mla_w_uq[l].reshape(MLA_Q_LORA, N_HEADS_MLA, MLA_NOPE + MLA_ROPE)
    w_uq = jnp.pad(w_uq, ((0, 0), (0, 0), (0, pad))).reshape(MLA_Q_LORA, N_HEADS_MLA * HEAD_PAD)
    w_uk = jnp.pad(mla_w_uk[l], ((0, 0), (0, 0), (0, HEAD_PAD - MLA_NOPE)))
    w_uvt = jnp.pad(jnp.transpose(mla_w_uv[l], (1, 2, 0)), ((0, 0), (0, VT_ROWS - MLA_V), (0, 0)))
    v_one = jnp.zeros((N_HEADS_MLA, VT_ROWS, 1), F32).at[:, MLA_V, :].set(1.0)
    return {
        "w_uvt": w_uvt.reshape(N_HEADS_MLA * VT_ROWS, MLA_KV_LORA).astype(BF16),
        "v_one": v_one.reshape(N_HEADS_MLA * VT_ROWS, 1),
        "norm_w": norm_w[l],
        "w_in": _permute_w_in(w_in[l]),
        "q_norm": mla_q_norm[l].reshape(1, -1),
        "w_uq": w_uq.astype(BF16),
        "kv_norm": mla_kv_norm[l].reshape(1, -1),
        "w_uk": w_uk.reshape(MLA_KV_LORA, N_HEADS_MLA * HEAD_PAD).astype(BF16),
        "w_uv": mla_w_uv[l].reshape(MLA_KV_LORA, N_HEADS_MLA * MLA_V).astype(BF16),
        "m_gate_b": mlstm_gate_b[l],
        "m_norm": mlstm_norm[l].reshape(1, -1),
        "g_conv_w": gdn_conv_w[l],
        "g_a_log": gdn_a_log[l].reshape(1, -1),
        "g_dt_bias": gdn_dt_bias[l].reshape(1, -1),
        "g_norm": gdn_norm[l].reshape(1, -1),
        "w_out": w_out[l].astype(BF16),
    }


def _recurrent_groups(proj3, lw, m_state, g_state, bb, t, shared_init):
    mm, m_state = _mlstm(proj3, lw, m_state, bb, t, shared_init)
    mg, g_state = _gdn(proj3, lw, g_state, bb, t, shared_init)
    return mm, mg, m_state, g_state


def kernel(x_prompt, x_sample, cache_mla_latent, cache_mla_krope, state_mlstm_C, state_mlstm_n, state_mlstm_m, state_gdn_S, state_gdn_conv, meta_tokens, norm_w, w_in, mla_q_norm, mla_w_uq, mla_kv_norm, mla_w_uk, mla_w_uv, mlstm_gate_b, mlstm_norm, gdn_conv_w, gdn_a_log, gdn_dt_bias, gdn_norm, w_out, final_norm):
    nb, seq, d = x_prompt.shape
    ns, dseq, _ = x_sample.shape
    n_meta = meta_tokens.shape[0]
    n_cache = cache_mla_latent.shape[2]
    depth = norm_w.shape[0]
    assert seq % 256 == 0 and n_meta % 8 == 0 and dseq % 8 == 0 and n_meta <= CHUNK and dseq <= CHUNK

    tabs_m = _rope_tables(0, n_meta)
    tabs_p = _rope_tables(n_meta, seq)
    tabs_s = _rope_tables(n_cache, dseq)

    h_m = meta_tokens.astype(F32)
    h_p = x_prompt.reshape(nb * seq, d)
    h_s = x_sample.reshape(ns * dseq, d)
    zero_m = (jnp.zeros((1, R_HEADS, R_DH, R_DH), F32), jnp.zeros((1, R_HEADS, R_DH), F32),
              jnp.zeros((1, R_HEADS), F32))
    zero_g = (jnp.zeros((1, R_HEADS, R_DH, R_DH), F32), jnp.zeros((1, CONV_W - 1, 3 * R_WIDTH), F32))
    bb_p = 2 if nb % 2 == 0 else 1
    bb_s = 4 if ns % 4 == 0 else 1
    p_rows, s_rows = [], []
    for l in range(depth):
        lw = _layer_weights(l, norm_w, w_in, mla_q_norm, mla_w_uq, mla_kv_norm, mla_w_uk, mla_w_uv,
                            mlstm_gate_b, mlstm_norm, gdn_conv_w, gdn_a_log, gdn_dt_bias, gdn_norm, w_out)
        last = l == depth - 1

        proj_m, q_m, k_m, v_m, c_m, kr_m = _inproj_mla(h_m, 1, tabs_m, lw)
        proj_m3 = proj_m.reshape(1, n_meta, NP)
        oa_m = _attention(q_m, k_m, v_m, None, n_meta, n_meta)
        mm_m, mg_m, mst, gst = _recurrent_groups(proj_m3, lw, zero_m, zero_g, 1, n_meta, True)
        if not last:
            h_m = _outproj(oa_m.reshape(n_meta, -1), proj_m, mm_m.reshape(n_meta, -1),
                           mg_m.reshape(n_meta, -1), h_m, lw["w_out"], None)

        proj_p, q_p, k_p, v_p, c_p, kr_p = _inproj_mla(h_p, nb, tabs_p, lw)
        proj_p3 = proj_p.reshape(nb, seq, NP)
        oa_p = _attention(q_p, k_p, v_p, (k_m, v_m), 256, 256)
        mm_p, mg_p, mst, gst = _recurrent_groups(proj_p3, lw, mst, gst, bb_p, R_CHUNK, True)
        h_p = _outproj(oa_p.reshape(nb * seq, -1), proj_p, mm_p.reshape(nb * seq, -1),
                       mg_p.reshape(nb * seq, -1), h_p, lw["w_out"], final_norm if last else None)
        p_rows.append((
            jnp.concatenate([jnp.broadcast_to(c_m, (nb,) + c_m.shape[1:]), c_p], axis=1),
            jnp.concatenate([jnp.broadcast_to(kr_m, (nb,) + kr_m.shape[1:]), kr_p], axis=1),
            mst[0], mst[1], mst[2], gst[0], gst[1]))

        proj_s = _inproj(h_s, lw["norm_w"], lw["w_in"])
        proj_s3 = proj_s.reshape(ns, dseq, NP)
        oa_s, c_s, kr_s = _sample_attention(proj_s3, cache_mla_latent, cache_mla_krope, l, tabs_s, lw)
        mm_s, mg_s, sm_st, sg_st = _recurrent_groups(
            proj_s3, lw, (state_mlstm_C[l], state_mlstm_n[l], state_mlstm_m[l]),
            (state_gdn_S[l], state_gdn_conv[l]), bb_s, dseq, False)
        h_s = _outproj(oa_s.reshape(ns * dseq, -1), proj_s, mm_s.reshape(ns * dseq, -1),
                       mg_s.reshape(ns * dseq, -1), h_s, lw["w_out"], final_norm if last else None)
        s_rows.append((c_s, kr_s, sm_st[0], sm_st[1], sm_st[2], sg_st[0], sg_st[1]))

    y_prompt = h_p.reshape(nb, seq, d)
    y_sample = h_s.reshape(ns, dseq, d)
    stack = lambda rows, i: jnp.stack([r[i] for r in rows])
    return ((y_prompt, y_sample) + tuple(stack(p_rows, i) for i in range(7))
            + tuple(stack(s_rows, i) for i in range(7)))
```

```python
import functools
import math

import jax
import jax.numpy as jnp
import numpy as np
from jax import lax
from jax.experimental import pallas as pl
from jax.experimental.pallas import tpu as pltpu

F32 = jnp.float32
BF16 = jnp.bfloat16

EPS = 1e-6
ROPE_BASE = 10000.0
CHUNK = 64
N_HEADS_MLA = 8
MLA_NOPE, MLA_ROPE, MLA_V = 64, 32, 64
MLA_Q_LORA, MLA_KV_LORA = 384, 256
R_HEADS, R_DH = 4, 128
R_WIDTH = R_HEADS * R_DH
CONV_W = 4
TRI_BLOCK = 16
R_CHUNK = 128
LANES = 128
HEAD_PAD = 128
VT_ROWS = 80
SOFTMAX_SCALE = 1.0 / math.sqrt(MLA_NOPE + MLA_ROPE)
LOG2E = math.log2(math.e)
QK_LOOKAHEAD = 4

OFF_GQKV = 0
OFF_MQKV = 1536
OFF_MOZ = 3072
OFF_ZA = 4096
OFF_ZG = 4608
OFF_CKV = 5120
OFF_CQ = 5376
OFF_SMALL = 5760
NP = 5888
PREP_AFTER = 2048
SM_KR, SM_MI, SM_MF, SM_GA, SM_GB = 0, 32, 36, 40, 44

VMEM_LIMIT = 56 * 1024 * 1024


def _cparams(sem):
    return pltpu.CompilerParams(dimension_semantics=sem, vmem_limit_bytes=VMEM_LIMIT)


def _bf(x):
    return x.astype(BF16)


def _dot(a, b):
    return jnp.dot(a, b, preferred_element_type=F32)


def _dot_nt(a, b):
    return lax.dot_general(a, b, (((1,), (1,)), ((), ())), preferred_element_type=F32)


def _dot_tn(a, b):
    return lax.dot_general(a, b, (((0,), (0,)), ((), ())), preferred_element_type=F32)


def _rms(x, w):
    return x * lax.rsqrt(jnp.mean(x * x, axis=-1, keepdims=True) + EPS) * w


def _silu(x):
    return x * jax.nn.sigmoid(x)


def _rope128(x, tc, ts1, ts2):
    return x * tc + pltpu.roll(x, 16, 1) * ts1 + pltpu.roll(x, LANES - 16, 1) * ts2


def _inproj_chunks(xn, w_ref, o_ref, n0, n_end):
    while n0 < n_end:
        n1 = min(n0 + 512, n_end)
        o_ref[:, n0:n1] = _dot(xn, w_ref[:, n0:n1])
        n0 = n1


def _inproj_kernel(x_ref, nw_ref, w_ref, o_ref):
    _inproj_chunks(_bf(_rms(x_ref[...], nw_ref[...])), w_ref, o_ref, 0, NP)


def _inproj(x2d, norm_w, w_bf):
    rows, d = x2d.shape
    tm = min(rows, 512)
    return pl.pallas_call(
        _inproj_kernel,
        grid=(rows // tm,),
        in_specs=[
            pl.BlockSpec((tm, d), lambda i: (i, 0)),
            pl.BlockSpec((1, d), lambda i: (0, 0)),
            pl.BlockSpec((d, NP), lambda i: (0, 0), pipeline_mode=pl.Buffered(1)),
        ],
        out_specs=pl.BlockSpec((tm, NP), lambda i: (i, 0)),
        out_shape=jax.ShapeDtypeStruct((rows, NP), F32),
        compiler_params=_cparams(("arbitrary",)),
        name="inproj",
    )(x2d, norm_w.reshape(1, d), w_bf)


def _inproj_mla_kernel(x_ref, nw_ref, w_ref, tc_ref, ts1_ref, ts2_ref, qn_ref, wuq_ref, kvn_ref,
                       wuk_ref, wuvt_ref, vone_ref, o_ref, q_out, k_out, vt_out, cn_out, kr_out):
    xn = _bf(_rms(x_ref[...], nw_ref[...]))
    _inproj_chunks(xn, w_ref, o_ref, OFF_CKV, NP)
    _inproj_chunks(xn, w_ref, o_ref, 0, PREP_AFTER)
    tc, ts1, ts2 = tc_ref[...], ts1_ref[...], ts2_ref[...]
    cq = o_ref[:, OFF_CQ:OFF_CQ + MLA_Q_LORA]
    q = _dot(_bf(_rms(cq, qn_ref[...])), wuq_ref[...])
    cn = _rms(o_ref[:, OFF_CKV:OFF_CKV + MLA_KV_LORA], kvn_ref[...])
    cn_out[...] = cn
    cnb = _bf(cn)
    kn = _dot(cnb, wuk_ref[...])
    vt_out[0] = _bf(_dot_nt(wuvt_ref[...], cnb) + vone_ref[...])
    sm = o_ref[:, OFF_SMALL:OFF_SMALL + LANES]
    lane = lax.broadcasted_iota(jnp.int32, sm.shape, 1)
    kr = pltpu.roll(jnp.where(lane < MLA_ROPE, sm, 0.0), MLA_NOPE, 1)
    kr = _rope128(kr, tc, ts1, ts2)
    kr_out[...] = pltpu.roll(kr, LANES - MLA_NOPE, 1)[:, :MLA_ROPE]
    for h in range(N_HEADS_MLA):
        sl = slice(h * HEAD_PAD, (h + 1) * HEAD_PAD)
        q_out[:, sl] = _bf(_rope128(q[:, sl], tc, ts1, ts2) * (SOFTMAX_SCALE * LOG2E))
        k_out[:, sl] = _bf(kn[:, sl] + kr)
    _inproj_chunks(xn, w_ref, o_ref, PREP_AFTER, OFF_CKV)


def _inproj_mla(x2d, nb, tabs, lw):
    rows, d = x2d.shape
    seq = rows // nb
    tm = min(seq, 512)
    tpb = seq // tm
    hq = N_HEADS_MLA * HEAD_PAD
    hvt = N_HEADS_MLA * VT_ROWS
    const = lambda i: (0, 0)
    row_tile = lambda i: (i, 0)
    tab_tile = lambda i: (i % tpb, 0)
    outs = pl.pallas_call(
        _inproj_mla_kernel,
        grid=(rows // tm,),
        in_specs=[
            pl.BlockSpec((tm, d), row_tile),
            pl.BlockSpec((1, d), const),
            pl.BlockSpec((d, NP), const, pipeline_mode=pl.Buffered(1)),
            pl.BlockSpec((tm, LANES), tab_tile),
            pl.BlockSpec((tm, LANES), tab_tile),
            pl.BlockSpec((tm, LANES), tab_tile),
            pl.BlockSpec((1, MLA_Q_LORA), const),
            pl.BlockSpec((MLA_Q_LORA, hq), const),
            pl.BlockSpec((1, MLA_KV_LORA), const),
            pl.BlockSpec((MLA_KV_LORA, hq), const),
            pl.BlockSpec((hvt, MLA_KV_LORA), const),
            pl.BlockSpec((hvt, 1), const),
        ],
        out_specs=[
            pl.BlockSpec((tm, NP), row_tile),
            pl.BlockSpec((tm, hq), row_tile),
            pl.BlockSpec((tm, hq), row_tile),
            pl.BlockSpec((1, hvt, tm), lambda i: (i // tpb, 0, i % tpb)),
            pl.BlockSpec((tm, MLA_KV_LORA), row_tile),
            pl.BlockSpec((tm, MLA_ROPE), row_tile),
        ],
        out_shape=[
            jax.ShapeDtypeStruct((rows, NP), F32),
            jax.ShapeDtypeStruct((rows, hq), BF16),
            jax.ShapeDtypeStruct((rows, hq), BF16),
            jax.ShapeDtypeStruct((nb, hvt, seq), BF16),
            jax.ShapeDtypeStruct((rows, MLA_KV_LORA), F32),
            jax.ShapeDtypeStruct((rows, MLA_ROPE), F32),
        ],
        compiler_params=_cparams(("arbitrary",)),
        name="inproj_mla",
    )(x2d, lw["norm_w"].reshape(1, d), lw["w_in"], *tabs, lw["q_norm"], lw["w_uq"], lw["kv_norm"],
      lw["w_uk"], lw["w_uvt"], lw["v_one"])
    proj, q, k, vt, cn, kr = outs
    return (proj, q.reshape(nb, seq, hq), k.reshape(nb, seq, hq), vt,
            cn.reshape(nb, seq, MLA_KV_LORA), kr.reshape(nb, seq, MLA_ROPE))


def _attn_kernel(*refs, tq, tk, has_prefix, single_tile):
    if has_prefix:
        q_ref, k_ref, vt_ref, kp_ref, vtp_ref, o_ref, m_s, acc_s, st_s = refs
    else:
        q_ref, k_ref, vt_ref, o_ref, m_s, acc_s, st_s = refs
    i = pl.program_id(1)
    q0 = i * tq
    nfull = q0 // tk
    hsl = [slice(h * HEAD_PAD, (h + 1) * HEAD_PAD) for h in range(N_HEADS_MLA)]
    rsl = [slice(h * VT_ROWS, (h + 1) * VT_ROWS) for h in range(N_HEADS_MLA)]

    def scores(h, kt):
        return _dot_nt(kt, q_ref[0, :, hsl[h]])

    def softmax_pv(h, st, vt, mask):
        if mask is not None:
            st = jnp.where(mask, st, -jnp.inf)
        m_old = m_s[h]
        m_new = jnp.maximum(m_old, jnp.max(st, axis=0, keepdims=True))
        p = jnp.exp2(st - m_new)
        acc_s[h] = jnp.exp2(m_old - m_new) * acc_s[h] + _dot(vt, _bf(p))
        m_s[h] = m_new

    def tile(r0, mask=None, r0_next=None):
        st = [None] * N_HEADS_MLA
        for h in range(N_HEADS_MLA):
            ahead = h + QK_LOOKAHEAD
            if ahead < N_HEADS_MLA:
                st[ahead] = scores(ahead, k_ref[0, pl.ds(r0, tk), hsl[ahead]])
            cur = st_s[h] if h < QK_LOOKAHEAD else st[h]
            if ahead >= N_HEADS_MLA and r0_next is not None:
                hn = ahead - N_HEADS_MLA
                st_s[hn] = scores(hn, k_ref[0, pl.ds(r0_next, tk), hsl[hn]])
            softmax_pv(h, cur, vt_ref[0, rsl[h], pl.ds(r0, tk)], mask)

    for h in range(N_HEADS_MLA):
        m_s[h] = jnp.full((1, tq), -jnp.inf, F32)
        acc_s[h] = jnp.zeros((VT_ROWS, tq), F32)
    if has_prefix:
        st_pre = [scores(h, kp_ref[0, :, hsl[h]]) for h in range(N_HEADS_MLA)]
    for h in range(QK_LOOKAHEAD):
        st_s[h] = scores(h, k_ref[0, 0:tk, hsl[h]])
    if has_prefix:
        for h in range(N_HEADS_MLA):
            softmax_pv(h, st_pre[h], vtp_ref[0, rsl[h], :], None)

    def full_tile(j):
        tile(pl.multiple_of(j * tk, tk), None, pl.multiple_of((j + 1) * tk, tk))

    def body(j, c):
        full_tile(2 * j)
        full_tile(2 * j + 1)
        return c

    if single_tile:
        r0 = 0
    else:
        lax.fori_loop(0, nfull // 2, body, 0)

        @pl.when(nfull % 2 == 1)
        def _():
            full_tile(nfull - 1)

        r0 = pl.multiple_of(nfull * tk, tk)
    k_chunk = (nfull * tk + lax.broadcasted_iota(jnp.int32, (tk, tq), 0)) // CHUNK
    q_chunk = (q0 + lax.broadcasted_iota(jnp.int32, (tk, tq), 1)) // CHUNK
    tile(r0, k_chunk <= q_chunk)
    for hp in range(N_HEADS_MLA // 2):
        pair = []
        for h in (2 * hp, 2 * hp + 1):
            acc = acc_s[h]
            pair.append(acc[0:MLA_V] / acc[MLA_V:MLA_V + 1])
        o_ref[0, :, hp * LANES:(hp + 1) * LANES] = _bf(jnp.concatenate(pair, axis=0).T)


def _attention(q, k, vt, prefix, tq, tk):
    nb, seq, hq = q.shape
    hvt = vt.shape[1]
    in_specs = [
        pl.BlockSpec((1, tq, hq), lambda b, i: (b, i, 0)),
        pl.BlockSpec((1, seq, hq), lambda b, i: (b, 0, 0)),
        pl.BlockSpec((1, hvt, seq), lambda b, i: (b, 0, 0)),
    ]
    args = [q, k, vt]
    if prefix is not None:
        kp, vtp = prefix
        npre = kp.shape[1]
        in_specs += [pl.BlockSpec((1, npre, hq), lambda b, i: (0, 0, 0)),
                     pl.BlockSpec((1, hvt, npre), lambda b, i: (0, 0, 0))]
        args += [kp, vtp]
    return pl.pallas_call(
        functools.partial(_attn_kernel, tq=tq, tk=tk, has_prefix=prefix is not None,
                          single_tile=seq == tk),
        grid=(nb, seq // tq),
        in_specs=in_specs,
        out_specs=pl.BlockSpec((1, tq, N_HEADS_MLA * MLA_V), lambda b, i: (b, i, 0)),
        out_shape=jax.ShapeDtypeStruct((nb, seq, N_HEADS_MLA * MLA_V), BF16),
        scratch_shapes=[pltpu.VMEM((N_HEADS_MLA, 1, tq), F32),
                        pltpu.VMEM((N_HEADS_MLA, VT_ROWS, tq), F32),
                        pltpu.VMEM((QK_LOOKAHEAD, tk, tq), F32)],
        compiler_params=_cparams(("arbitrary", "arbitrary")),
        name="attn",
    )(*args)


def _sample_attn_kernel(cq_ref, ckv_ref, sm_ref, cc_ref, kc_ref, tc_ref, ts1_ref, ts2_ref, qn_ref,
                        wuq_ref, kvn_ref, wuk_ref, wuv_ref, o_ref, cn_out, kr_out, *, n_cache, tk):
    tc, ts1, ts2 = tc_ref[...], ts1_ref[...], ts2_ref[...]
    t = cq_ref.shape[1]
    q = _dot(_bf(_rms(cq_ref[0], qn_ref[...])), wuq_ref[...])
    cn = _rms(ckv_ref[0], kvn_ref[...])
    cn_out[0] = cn
    sm = sm_ref[0]
    lane = lax.broadcasted_iota(jnp.int32, sm.shape, 1)
    kr = pltpu.roll(jnp.where(lane < MLA_ROPE, sm, 0.0), MLA_NOPE, 1)
    kr = _rope128(kr, tc, ts1, ts2)
    krn = pltpu.roll(kr, LANES - MLA_NOPE, 1)[:, :MLA_ROPE]
    kr_out[0] = krn
    ql, qr = [], []
    for h in range(N_HEADS_MLA):
        sl = slice(h * HEAD_PAD, (h + 1) * HEAD_PAD)
        qh = _rope128(q[:, sl], tc, ts1, ts2)
        ql.append(_dot_nt(_bf(qh), wuk_ref[:, sl]))
        qr.append(pltpu.roll(qh, LANES - MLA_NOPE, 1)[:, :MLA_ROPE])
    qlat = _bf(jnp.concatenate(ql, axis=0) * (SOFTMAX_SCALE * LOG2E))
    qrope = _bf(jnp.concatenate(qr, axis=0) * (SOFTMAX_SCALE * LOG2E))

    bounds = [(r0, min(r0 + tk, n_cache)) for r0 in range(0, n_cache, tk)]
    c_bf = [_bf(cc_ref[0, r0:r1, :]) for r0, r1 in bounds] + [_bf(cn)]
    k_bf = [_bf(kc_ref[0, r0:r1, :]) for r0, r1 in bounds] + [_bf(krn)]
    s = [_dot_nt(qlat, c) + _dot_nt(qrope, k) for c, k in zip(c_bf, k_bf)]

    def lane_blocks(x):
        w = x.shape[1]
        return [x[:, i:i + LANES] for i in range(0, w, LANES)] if w % LANES == 0 else None

    def folded(xs, op):
        wide = [b for x in xs if lane_blocks(x) for b in lane_blocks(x)]
        narrow = [x for x in xs if lane_blocks(x) is None]
        return ([functools.reduce(op, wide)] if wide else []) + narrow

    m = functools.reduce(jnp.maximum, [jnp.max(x, axis=1, keepdims=True) for x in folded(s, jnp.maximum)])
    p = [jnp.exp2(x - m) for x in s]
    l = functools.reduce(jnp.add, [jnp.sum(x, axis=1, keepdims=True) for x in folded(p, jnp.add)])
    acc = functools.reduce(jnp.add, [_dot(_bf(x), c) for x, c in zip(p, c_bf)])
    olat = _bf(acc / l)
    out_lane = lax.broadcasted_iota(jnp.int32, (t, N_HEADS_MLA * MLA_V), 1) // MLA_V
    o = jnp.zeros((t, N_HEADS_MLA * MLA_V), F32)
    for h in range(N_HEADS_MLA):
        o = jnp.where(out_lane == h, _dot(olat[h * t:(h + 1) * t], wuv_ref[...]), o)
    o_ref[0] = _bf(o)


def _sample_attention(proj3, cache_c, cache_kr, layer, tabs, lw):
    nb, t, _ = proj3.shape
    n_cache = cache_c.shape[2]
    hq = N_HEADS_MLA * HEAD_PAD
    hv = N_HEADS_MLA * MLA_V
    const = lambda b: (0, 0)
    return pl.pallas_call(
        functools.partial(_sample_attn_kernel, n_cache=n_cache, tk=min(512, n_cache)),
        grid=(nb,),
        in_specs=[
            pl.BlockSpec((1, t, MLA_Q_LORA), lambda b: (b, 0, OFF_CQ // MLA_Q_LORA)),
            pl.BlockSpec((1, t, MLA_KV_LORA), lambda b: (b, 0, OFF_CKV // MLA_KV_LORA)),
            pl.BlockSpec((1, t, LANES), lambda b: (b, 0, OFF_SMALL // LANES)),
            pl.BlockSpec((1, n_cache, MLA_KV_LORA), lambda b: (layer * nb + b, 0, 0)),
            pl.BlockSpec((1, n_cache, MLA_ROPE), lambda b: (layer * nb + b, 0, 0)),
            pl.BlockSpec((t, LANES), const),
            pl.BlockSpec((t, LANES), const),
            pl.BlockSpec((t, LANES), const),
            pl.BlockSpec((1, MLA_Q_LORA), const),
            pl.BlockSpec((MLA_Q_LORA, hq), const),
            pl.BlockSpec((1, MLA_KV_LORA), const),
            pl.BlockSpec((MLA_KV_LORA, hq), const),
            pl.BlockSpec((MLA_KV_LORA, hv), const),
        ],
        out_specs=[
            pl.BlockSpec((1, t, hv), lambda b: (b, 0, 0)),
            pl.BlockSpec((1, t, MLA_KV_LORA), lambda b: (b, 0, 0)),
            pl.BlockSpec((1, t, MLA_ROPE), lambda b: (b, 0, 0)),
        ],
        out_shape=[
            jax.ShapeDtypeStruct((nb, t, hv), BF16),
            jax.ShapeDtypeStruct((nb, t, MLA_KV_LORA), F32),
            jax.ShapeDtypeStruct((nb, t, MLA_ROPE), F32),
        ],
        compiler_params=_cparams(("arbitrary",)),
        name="sample_attn",
    )(proj3, proj3, proj3, cache_c.reshape(-1, n_cache, MLA_KV_LORA),
      cache_kr.reshape(-1, n_cache, MLA_ROPE), *tabs, lw["q_norm"], lw["w_uq"], lw["kv_norm"],
      lw["w_uk"], lw["w_uv"])


def _chunk_masks(t):
    row = lax.broadcasted_iota(jnp.int32, (t, t), 0)
    col = lax.broadcasted_iota(jnp.int32, (t, t), 1)
    return row == col, col <= row, row <= col, col < row


def _to_row(col_vec, eye):
    return jnp.sum(jnp.where(eye, col_vec, 0.0), axis=0, keepdims=True)


def _mlstm_kernel(qkv_ref, oz_ref, sm_ref, gb_ref, nrm_ref, c0_ref, n0_ref, m0_ref,
                  out_ref, c_ref, n_ref, m_ref, *, bb, t, shared_init):
    @pl.when(pl.program_id(1) == 0)
    def _():
        for b in range(bb):
            src = 0 if shared_init else b
            c_ref[b] = c0_ref[src]
            n_ref[b] = n0_ref[src]
            m_ref[0, b:b + 1, :] = m0_ref[0, src:src + 1, :]

    eye, tril, triu, _ = _chunk_masks(t)
    chains = [(b, h) for b in range(bb) for h in range(R_HEADS)]
    i_col = [sm_ref[b, :, SM_MI + h:SM_MI + h + 1] + gb_ref[0:1, h:h + 1] for b, h in chains]
    f_col = [jax.nn.log_sigmoid(sm_ref[b, :, SM_MF + h:SM_MF + h + 1] + gb_ref[1:2, h:h + 1])
             for b, h in chains]
    i_row = [_to_row(x, eye) for x in i_col]
    f_row = [_to_row(x, eye) for x in f_col]
    b_col = [jnp.sum(jnp.where(tril, x, 0.0), axis=1, keepdims=True) for x in f_row]
    b_row = [jnp.sum(jnp.where(triu, x, 0.0), axis=0, keepdims=True) for x in f_col]
    dmat = [jnp.where(tril, bc - br + ir, -jnp.inf) for bc, br, ir in zip(b_col, b_row, i_row)]
    dmax = [jnp.max(x, axis=1, keepdims=True) for x in dmat]
    pre = []
    for (b, h), ic, bc, dm, dx in zip(chains, i_col, b_col, dmat, dmax):
        hs = slice(h * R_DH, (h + 1) * R_DH)
        q = qkv_ref[b, :, hs]
        k = qkv_ref[b, :, R_WIDTH + h * R_DH:R_WIDTH + (h + 1) * R_DH] * (R_DH ** -0.5)
        v = qkv_ref[b, :, 2 * R_WIDTH + h * R_DH:2 * R_WIDTH + (h + 1) * R_DH]
        m_prev = m_ref[0, b:b + 1, h:h + 1]
        inter = bc + m_prev
        m_t = jnp.maximum(inter, dx)
        m_new = m_t[t - 1:t, :]
        b_last = bc[t - 1:t, :]
        pre.append(dict(
            q=q, k=k, v=v, qb=_bf(q), kb=_bf(k), vb=_bf(v), m_t=m_t, m_new=m_new,
            w_inter=jnp.exp(inter - m_t), e=jnp.exp(dm - m_t),
            g_state=jnp.exp(b_last + m_prev - m_new),
            g_tok=jnp.exp(b_last - bc + ic - m_new)))
    qk_raw = [_dot_nt(p["qb"], p["kb"]) for p in pre]
    qc = [_dot_nt(p["qb"], _bf(c_ref[b, h])) for p, (b, h) in zip(pre, chains)]
    qk = [r * p["e"] for r, p in zip(qk_raw, pre)]
    pv = [_dot(_bf(x), p["vb"]) for x, p in zip(qk, pre)]
    upd = [_dot_tn(_bf(p["g_tok"] * p["v"]), p["kb"]) for p in pre]
    nvec = [n_ref[b, h:h + 1, :] for b, h in chains]
    qn = [jnp.sum(p["q"] * nv, axis=1, keepdims=True) for p, nv in zip(pre, nvec)]
    qks = [jnp.sum(x, axis=1, keepdims=True) for x in qk]
    hm = []
    for (b, h), p, qn_c, qks_c, qc_c, pv_c in zip(chains, pre, qn, qks, qc, pv):
        num = p["w_inter"] * qc_c + pv_c
        den = p["w_inter"] * qn_c + qks_c
        hh = num / jnp.maximum(jnp.abs(den), jnp.exp(-p["m_t"]))
        hm.append(jax.nn.sigmoid(oz_ref[b, :, h * R_DH:(h + 1) * R_DH]) * hh)
    ms = [jnp.mean(x * x, axis=-1, keepdims=True) for x in hm]
    for (b, h), p, nv, upd_c, hm_c, ms_c in zip(chains, pre, nvec, upd, hm, ms):
        hs = slice(h * R_DH, (h + 1) * R_DH)
        c_ref[b, h] = p["g_state"] * c_ref[b, h] + upd_c
        n_ref[b, h:h + 1, :] = p["g_state"] * nv + jnp.sum(p["g_tok"] * p["k"], axis=0, keepdims=True)
        m_ref[0, b:b + 1, h:h + 1] = p["m_new"]
        om = hm_c * lax.rsqrt(ms_c + EPS) * nrm_ref[0:1, hs]
        out_ref[b, :, hs] = _bf(om * _silu(oz_ref[b, :, R_WIDTH + h * R_DH:R_WIDTH + (h + 1) * R_DH]))


def _mlstm(proj3, lw, state, bb, t, shared_init):
    nb, seq, _ = proj3.shape
    c0, n0, m0 = state
    sb = 1 if shared_init else bb
    m0 = m0.reshape(-1, sb, R_HEADS)
    st = (lambda i, c: (0, 0, 0, 0)) if shared_init else (lambda i, c: (i, 0, 0, 0))
    st3 = (lambda i, c: (0, 0, 0)) if shared_init else (lambda i, c: (i, 0, 0))
    out, c1, n1, m1 = pl.pallas_call(
        functools.partial(_mlstm_kernel, bb=bb, t=t, shared_init=shared_init),
        grid=(nb // bb, seq // t),
        in_specs=[
            pl.BlockSpec((bb, t, 3 * R_WIDTH), lambda i, c: (i, c, OFF_MQKV // (3 * R_WIDTH))),
            pl.BlockSpec((bb, t, 2 * R_WIDTH), lambda i, c: (i, c, OFF_MOZ // (2 * R_WIDTH))),
            pl.BlockSpec((bb, t, LANES), lambda i, c: (i, c, OFF_SMALL // LANES)),
            pl.BlockSpec((2, R_HEADS), lambda i, c: (0, 0)),
            pl.BlockSpec((1, R_WIDTH), lambda i, c: (0, 0)),
            pl.BlockSpec((sb, R_HEADS, R_DH, R_DH), st),
            pl.BlockSpec((sb, R_HEADS, R_DH), st3),
            pl.BlockSpec((1, sb, R_HEADS), st3),
        ],
        out_specs=[
            pl.BlockSpec((bb, t, R_WIDTH), lambda i, c: (i, c, 0)),
            pl.BlockSpec((bb, R_HEADS, R_DH, R_DH), lambda i, c: (i, 0, 0, 0)),
            pl.BlockSpec((bb, R_HEADS, R_DH), lambda i, c: (i, 0, 0)),
            pl.BlockSpec((1, bb, R_HEADS), lambda i, c: (i, 0, 0)),
        ],
        out_shape=[
            jax.ShapeDtypeStruct((nb, seq, R_WIDTH), BF16),
            jax.ShapeDtypeStruct((nb, R_HEADS, R_DH, R_DH), F32),
            jax.ShapeDtypeStruct((nb, R_HEADS, R_DH), F32),
            jax.ShapeDtypeStruct((nb // bb, bb, R_HEADS), F32),
        ],
        compiler_params=_cparams(("arbitrary", "arbitrary")),
        name="mlstm",
    )(proj3, proj3, proj3, lw["m_gate_b"], lw["m_norm"], c0, n0, m0)
    return out, (c1, n1, m1.reshape(nb, R_HEADS))


def _neumann_all(a_list, t, nil):
    levels = int(math.log2(nil)) - 1
    n_acc = [-a for a in a_list]
    pw = [_dot(_bf(a), _bf(a)) for a in a_list]
    for _ in range(levels - 1):
        r = [_dot(_bf(jnp.concatenate([p, n], axis=0)), _bf(p)) for p, n in zip(pw, n_acc)]
        n_acc = [n + p + x[t:] for n, p, x in zip(n_acc, pw, r)]
        pw = [x[:t] for x in r]
    r = [_dot(_bf(n), _bf(p)) for p, n in zip(pw, n_acc)]
    return [n + p + x for n, p, x in zip(n_acc, pw, r)]


def _unit_lower_inverse_all(a_list, t):
    blk = min(t, TRI_BLOCK)
    row = lax.broadcasted_iota(jnp.int32, (t, t), 0)
    col = lax.broadcasted_iota(jnp.int32, (t, t), 1)
    shift = int(math.log2(blk))
    same = jnp.right_shift(row, shift) == jnp.right_shift(col, shift)
    n_list = _neumann_all([jnp.where(same, a, 0.0) for a in a_list], t, blk)
    if blk == t:
        return n_list
    eye = (row == col).astype(F32)
    w_list = [n + eye for n in n_list]
    while blk < t:
        inner = ((jnp.right_shift(row, shift + 1) == jnp.right_shift(col, shift + 1))
                 & (jnp.right_shift(row, shift) > jnp.right_shift(col, shift)))
        z = [_dot(_bf(jnp.where(inner, a, 0.0)), _bf(w)) for a, w in zip(a_list, w_list)]
        w_list = [w - _dot(_bf(w), _bf(x)) for w, x in zip(w_list, z)]
        blk *= 2
        shift += 1
    return [w - eye for w in w_list]


def _conv_silu_rows(xwin, r0, rb, cw_ref):
    xw = xwin[r0:r0 + rb + 8]
    conv = xw[8:8 + rb] * cw_ref[CONV_W - 1:CONV_W, :]
    for j in range(1, CONV_W):
        conv = conv + pltpu.roll(xw, j, 0)[8:8 + rb] * cw_ref[CONV_W - 1 - j:CONV_W - j, :]
    return _silu(conv)


def _gdn_kernel(x_ref, z_ref, sm_ref, cw_ref, alog_ref, dtb_ref, gn_ref, s0_ref, buf0_ref,
                out_ref, s_ref, buf_ref, xwin_ref, *, bb, t, shared_init):
    c = pl.program_id(1)

    @pl.when(c == 0)
    def _():
        for b in range(bb):
            src = 0 if shared_init else b
            s_ref[b] = s0_ref[src]
            xwin_ref[b, 0:8, :] = jnp.zeros((8, 3 * R_WIDTH), F32)
            xwin_ref[b, 8 - (CONV_W - 1):8, :] = buf0_ref[src]

    eye, tril, triu, strict = _chunk_masks(t)
    chains = [(b, h) for b in range(bb) for h in range(R_HEADS)]
    g_col = [-jnp.exp(alog_ref[0:1, h:h + 1])
             * jax.nn.softplus(sm_ref[b, :, SM_GA + h:SM_GA + h + 1] + dtb_ref[0:1, h:h + 1])
             for b, h in chains]
    g_row = [_to_row(x, eye) for x in g_col]
    gc_col = [jnp.sum(jnp.where(tril, x, 0.0), axis=1, keepdims=True) for x in g_row]
    gc_row = [jnp.sum(jnp.where(triu, x, 0.0), axis=0, keepdims=True) for x in g_col]
    acts = []
    for b in range(bb):
        xwin_ref[b, 8:8 + t, :] = x_ref[b]
        acts.append(_conv_silu_rows(xwin_ref.at[b], 0, t, cw_ref))
        tail = xwin_ref[b, 8 + t - (CONV_W - 1):8 + t, :]
        xwin_ref[b, 8 - (CONV_W - 1):8, :] = tail
        buf_ref[b] = tail
    gq = [acts[b][:, h * R_DH:(h + 1) * R_DH] for b, h in chains]
    gk = [acts[b][:, R_WIDTH + h * R_DH:R_WIDTH + (h + 1) * R_DH] for b, h in chains]
    q_ss = [jnp.sum(x * x, axis=1, keepdims=True) for x in gq]
    k_ss = [jnp.sum(x * x, axis=1, keepdims=True) for x in gk]
    pre = []
    for i, (b, h) in enumerate(chains):
        gv = acts[b][:, 2 * R_WIDTH + h * R_DH:2 * R_WIDTH + (h + 1) * R_DH]
        qn = gq[i] * lax.rsqrt(q_ss[i] + EPS) * (R_DH ** -0.5)
        kn = gk[i] * lax.rsqrt(k_ss[i] + EPS)
        beta = jax.nn.sigmoid(sm_ref[b, :, SM_GB + h:SM_GB + h + 1])
        eg = jnp.exp(gc_col[i])
        g_last = gc_col[i][t - 1:t, :]
        pre.append(dict(
            qn=qn, kn=kn, beta=beta, eg=eg, g_last=g_last,
            gam=jnp.exp(jnp.where(tril, gc_col[i] - gc_row[i], -jnp.inf)),
            rhs=jnp.concatenate([beta * gv, (beta * eg) * kn], axis=1),
            kdec=kn * jnp.exp(g_last - gc_col[i])))
    kq = [_dot_nt(_bf(jnp.concatenate([p["kn"], p["qn"]], axis=0)), _bf(p["kn"])) for p in pre]
    a_list = [jnp.where(strict, p["beta"] * x[:t] * p["gam"], 0.0) for p, x in zip(pre, kq)]
    n_inv = _unit_lower_inverse_all(a_list, t)
    sol = [p["rhs"] + _dot(_bf(n), _bf(p["rhs"])) for p, n in zip(pre, n_inv)]
    ws = [_dot(_bf(jnp.concatenate([s[:, R_DH:], p["qn"] * p["eg"]], axis=0)), _bf(s_ref[b, h]))
          for p, s, (b, h) in zip(pre, sol, chains)]
    delta = [_bf(s[:, :R_DH] - x[:t]) for s, x in zip(sol, ws)]
    o2 = [_dot(_bf(x[t:] * p["gam"]), d) for p, x, d in zip(pre, kq, delta)]
    upd = [_dot_tn(_bf(p["kdec"]), d) for p, d in zip(pre, delta)]
    o = [x[t:] + o2_c for x, o2_c in zip(ws, o2)]
    ms = [jnp.mean(x * x, axis=-1, keepdims=True) for x in o]
    for (b, h), p, o_c, ms_c, upd_c in zip(chains, pre, o, ms, upd):
        hs = slice(h * R_DH, (h + 1) * R_DH)
        s_ref[b, h] = jnp.exp(p["g_last"]) * s_ref[b, h] + upd_c
        out_ref[b, :, hs] = _bf(o_c * lax.rsqrt(ms_c + EPS) * gn_ref[...] * _silu(z_ref[b, :, hs]))


def _gdn(proj3, lw, state, bb, t, shared_init):
    nb, seq, _ = proj3.shape
    s0, buf0 = state
    sb = 1 if shared_init else bb
    st = (lambda i, c: (0, 0, 0, 0)) if shared_init else (lambda i, c: (i, 0, 0, 0))
    st3 = (lambda i, c: (0, 0, 0)) if shared_init else (lambda i, c: (i, 0, 0))
    const = lambda i, c: (0, 0)
    in_specs = [
        pl.BlockSpec((bb, t, 3 * R_WIDTH), lambda i, c: (i, c, OFF_GQKV // (3 * R_WIDTH))),
        pl.BlockSpec((bb, t, R_WIDTH), lambda i, c: (i, c, OFF_ZG // R_WIDTH)),
        pl.BlockSpec((bb, t, LANES), lambda i, c: (i, c, OFF_SMALL // LANES)),
        pl.BlockSpec((CONV_W, 3 * R_WIDTH), const),
        pl.BlockSpec((1, R_HEADS), const),
        pl.BlockSpec((1, R_HEADS), const),
        pl.BlockSpec((1, R_DH), const),
        pl.BlockSpec((sb, R_HEADS, R_DH, R_DH), st),
        pl.BlockSpec((sb, CONV_W - 1, 3 * R_WIDTH), st3),
    ]
    args = [proj3, proj3, proj3, lw["g_conv_w"], lw["g_a_log"], lw["g_dt_bias"], lw["g_norm"], s0, buf0]
    out_specs = [
        pl.BlockSpec((bb, t, R_WIDTH), lambda i, c: (i, c, 0)),
        pl.BlockSpec((bb, R_HEADS, R_DH, R_DH), lambda i, c: (i, 0, 0, 0)),
        pl.BlockSpec((bb, CONV_W - 1, 3 * R_WIDTH), lambda i, c: (i, 0, 0)),
    ]
    out_shape = [
        jax.ShapeDtypeStruct((nb, seq, R_WIDTH), BF16),
        jax.ShapeDtypeStruct((nb, R_HEADS, R_DH, R_DH), F32),
        jax.ShapeDtypeStruct((nb, CONV_W - 1, 3 * R_WIDTH), F32),
    ]
    out, s1, buf1 = pl.pallas_call(
        functools.partial(_gdn_kernel, bb=bb, t=t, shared_init=shared_init),
        grid=(nb // bb, seq // t),
        in_specs=in_specs,
        out_specs=out_specs,
        out_shape=out_shape,
        scratch_shapes=[pltpu.VMEM((bb, 8 + t, 3 * R_WIDTH), F32)],
        compiler_params=_cparams(("arbitrary", "arbitrary")),
        name="gdn",
    )(*args)
    return out, (s1, buf1)


def _outproj_kernel(*refs, final):
    if final:
        oa_ref, za_ref, mm_ref, mg_ref, x_ref, w_ref, fn_ref, y_ref = refs
    else:
        oa_ref, za_ref, mm_ref, mg_ref, x_ref, w_ref, y_ref = refs
    ma = oa_ref[...].astype(F32) * _silu(za_ref[...])
    acc = (_dot(_bf(ma), w_ref[0:R_WIDTH, :])
           + _dot(mm_ref[...], w_ref[R_WIDTH:2 * R_WIDTH, :])
           + _dot(mg_ref[...], w_ref[2 * R_WIDTH:3 * R_WIDTH, :]))
    hnew = x_ref[...] + acc
    y_ref[...] = _rms(hnew, fn_ref[...]) if final else hnew


def _outproj(oa, proj2, mm, mg, x2d, w_bf, final_norm):
    rows, d = x2d.shape
    tm = min(rows, 512)
    final = final_norm is not None
    in_specs = [
        pl.BlockSpec((tm, R_WIDTH), lambda i: (i, 0)),
        pl.BlockSpec((tm, R_WIDTH), lambda i: (i, OFF_ZA // R_WIDTH)),
        pl.BlockSpec((tm, R_WIDTH), lambda i: (i, 0)),
        pl.BlockSpec((tm, R_WIDTH), lambda i: (i, 0)),
        pl.BlockSpec((tm, d), lambda i: (i, 0)),
        pl.BlockSpec((3 * R_WIDTH, d), lambda i: (0, 0)),
    ]
    args = [oa, proj2, mm, mg, x2d, w_bf]
    if final:
        in_specs.append(pl.BlockSpec((1, d), lambda i: (0, 0)))
        args.append(final_norm.reshape(1, d))
    return pl.pallas_call(
        functools.partial(_outproj_kernel, final=final),
        grid=(rows // tm,),
        in_specs=in_specs,
        out_specs=pl.BlockSpec((tm, d), lambda i: (i, 0)),
        out_shape=jax.ShapeDtypeStruct((rows, d), F32),
        compiler_params=_cparams(("arbitrary",)),
        name="outproj",
    )(*args)


def _permute_w_in(w):
    d = w.shape[0]
    c_q, c_kv, k_r, z_a = w[:, 0:384], w[:, 384:640], w[:, 640:672], w[:, 672:1184]
    m_qkv, m_i, m_f = w[:, 1184:2720], w[:, 2720:2724], w[:, 2724:2728]
    m_oz = w[:, 2728:3752]
    g_qkv, g_a, g_b, z_g = w[:, 3752:5288], w[:, 5288:5292], w[:, 5292:5296], w[:, 5296:5808]
    small = jnp.concatenate([k_r, m_i, m_f, g_a, g_b, jnp.zeros((d, LANES - 48), w.dtype)], axis=1)
    return jnp.concatenate([g_qkv, m_qkv, m_oz, z_a, z_g, c_kv, c_q, small], axis=1).astype(BF16)


def _rope_tables(pos0, n):
    half = MLA_ROPE // 2
    freq = ROPE_BASE ** (-np.arange(half, dtype=np.float64) / half)
    ang = (pos0 + np.arange(n)).astype(np.float64)[:, None] * freq[None, :]
    cos, sin = np.cos(ang).astype(np.float32), np.sin(ang).astype(np.float32)
    one_lo = np.ones((n, MLA_NOPE), np.float32)
    one_hi = np.ones((n, LANES - MLA_NOPE - MLA_ROPE), np.float32)
    zero_lo = np.zeros((n, MLA_NOPE), np.float32)
    zero_hi = np.zeros((n, LANES - MLA_NOPE - MLA_ROPE), np.float32)
    zero_h = np.zeros((n, half), np.float32)
    tc = np.concatenate([one_lo, cos, cos, one_hi], axis=1)
    ts1 = np.concatenate([zero_lo, zero_h, sin, zero_hi], axis=1)
    ts2 = np.concatenate([zero_lo, -sin, zero_h, zero_hi], axis=1)
    return jnp.asarray(tc), jnp.asarray(ts1), jnp.asarray(ts2)


def _layer_weights(l, norm_w, w_in, mla_q_norm, mla_w_uq, mla_kv_norm, mla_w_uk, mla_w_uv, mlstm_gate_b,
                   mlstm_norm, gdn_conv_w, gdn_a_log, gdn_dt_bias, gdn_norm, w_out):
    pad = HEAD_PAD - (MLA_NOPE + MLA_ROPE)
    w_uq = mla_w_uq[l].reshape(MLA_Q_LORA, N_HEADS_MLA, MLA_NOPE + MLA_ROPE)
    w_uq = jnp.pad(w_uq, ((0, 0), (0, 0), (0, pad))).reshape(MLA_Q_LORA, N_HEADS_MLA * HEAD_PAD)
    w_uk = jnp.pad(mla_w_uk[l], ((0, 0), (0, 0), (0, HEAD_PAD - MLA_NOPE)))
    w_uvt = jnp.pad(jnp.transpose(mla_w_uv[l], (1, 2, 0)), ((0, 0), (0, VT_ROWS - MLA_V), (0, 0)))
    v_one = jnp.zeros((N_HEADS_MLA, VT_ROWS, 1), F32).at[:, MLA_V, :].set(1.0)
    return {
        "w_uvt": w_uvt.reshape(N_HEADS_MLA * VT_ROWS, MLA_KV_LORA).astype(BF16),
        "v_one": v_one.reshape(N_HEADS_MLA * VT_ROWS, 1),
        "norm_w": norm_w[l],
        "w_in": _permute_w_in(w_in[l]),
        "q_norm": mla_q_norm[l].reshape(1, -1),
        "w_uq": w_uq.astype(BF16),
        "kv_norm": mla_kv_norm[l].reshape(1, -1),
        "w_uk": w_uk.reshape(MLA_KV_LORA, N_HEADS_MLA * HEAD_PAD).astype(BF16),
        "w_uv": mla_w_uv[l].reshape(MLA_KV_LORA, N_HEADS_MLA * MLA_V).astype(BF16),
        "m_gate_b": mlstm_gate_b[l],
        "m_norm": mlstm_norm[l].reshape(1, -1),
        "g_conv_w": gdn_conv_w[l],
        "g_a_log": gdn_a_log[l].reshape(1, -1),
        "g_dt_bias": gdn_dt_bias[l].reshape(1, -1),
        "g_norm": gdn_norm[l].reshape(1, -1),
        "w_out": w_out[l].astype(BF16),
    }


def _recurrent_groups(proj3, lw, m_state, g_state, bb, t, shared_init):
    mm, m_state = _mlstm(proj3, lw, m_state, bb, t, shared_init)
    mg, g_state = _gdn(proj3, lw, g_state, bb, t, shared_init)
    return mm, mg, m_state, g_state


def kernel(x_prompt, x_sample, cache_mla_latent, cache_mla_krope, state_mlstm_C, state_mlstm_n, state_mlstm_m, state_gdn_S, state_gdn_conv, meta_tokens, norm_w, w_in, mla_q_norm, mla_w_uq, mla_kv_norm, mla_w_uk, mla_w_uv, mlstm_gate_b, mlstm_norm, gdn_conv_w, gdn_a_log, gdn_dt_bias, gdn_norm, w_out, final_norm):
    nb, seq, d = x_prompt.shape
    ns, dseq, _ = x_sample.shape
    n_meta = meta_tokens.shape[0]
    n_cache = cache_mla_latent.shape[2]
    depth = norm_w.shape[0]
    assert seq % 256 == 0 and n_meta % 8 == 0 and dseq % 8 == 0 and n_meta <= CHUNK and dseq <= CHUNK

    tabs_m = _rope_tables(0, n_meta)
    tabs_p = _rope_tables(n_meta, seq)
    tabs_s = _rope_tables(n_cache, dseq)

    h_m = meta_tokens.astype(F32)
    h_p = x_prompt.reshape(nb * seq, d)
    h_s = x_sample.reshape(ns * dseq, d)
    zero_m = (jnp.zeros((1, R_HEADS, R_DH, R_DH), F32), jnp.zeros((1, R_HEADS, R_DH), F32),
              jnp.zeros((1, R_HEADS), F32))
    zero_g = (jnp.zeros((1, R_HEADS, R_DH, R_DH), F32), jnp.zeros((1, CONV_W - 1, 3 * R_WIDTH), F32))
    bb_p = 2 if nb % 2 == 0 else 1
    bb_s = 4 if ns % 4 == 0 else 1
    p_rows, s_rows = [], []
    for l in range(depth):
        lw = _layer_weights(l, norm_w, w_in, mla_q_norm, mla_w_uq, mla_kv_norm, mla_w_uk, mla_w_uv,
                            mlstm_gate_b, mlstm_norm, gdn_conv_w, gdn_a_log, gdn_dt_bias, gdn_norm, w_out)
        last = l == depth - 1

        proj_m, q_m, k_m, v_m, c_m, kr_m = _inproj_mla(h_m, 1, tabs_m, lw)
        proj_m3 = proj_m.reshape(1, n_meta, NP)
        oa_m = _attention(q_m, k_m, v_m, None, n_meta, n_meta)
        mm_m, mg_m, mst, gst = _recurrent_groups(proj_m3, lw, zero_m, zero_g, 1, n_meta, True)
        if not last:
            h_m = _outproj(oa_m.reshape(n_meta, -1), proj_m, mm_m.reshape(n_meta, -1),
                           mg_m.reshape(n_meta, -1), h_m, lw["w_out"], None)

        proj_p, q_p, k_p, v_p, c_p, kr_p = _inproj_mla(h_p, nb, tabs_p, lw)
        proj_p3 = proj_p.reshape(nb, seq, NP)
        oa_p = _attention(q_p, k_p, v_p, (k_m, v_m), 256, 256)
        mm_p, mg_p, mst, gst = _recurrent_groups(proj_p3, lw, mst, gst, bb_p, R_CHUNK, True)
        h_p = _outproj(oa_p.reshape(nb * seq, -1), proj_p, mm_p.reshape(nb * seq, -1),
                       mg_p.reshape(nb * seq, -1), h_p, lw["w_out"], final_norm if last else None)
        p_rows.append((
            jnp.concatenate([jnp.broadcast_to(c_m, (nb,) + c_m.shape[1:]), c_p], axis=1),
            jnp.concatenate([jnp.broadcast_to(kr_m, (nb,) + kr_m.shape[1:]), kr_p], axis=1),
            mst[0], mst[1], mst[2], gst[0], gst[1]))

        proj_s = _inproj(h_s, lw["norm_w"], lw["w_in"])
        proj_s3 = proj_s.reshape(ns, dseq, NP)
        oa_s, c_s, kr_s = _sample_attention(proj_s3, cache_mla_latent, cache_mla_krope, l, tabs_s, lw)
        mm_s, mg_s, sm_st, sg_st = _recurrent_groups(
            proj_s3, lw, (state_mlstm_C[l], state_mlstm_n[l], state_mlstm_m[l]),
            (state_gdn_S[l], state_gdn_conv[l]), bb_s, dseq, False)
        h_s = _outproj(oa_s.reshape(ns * dseq, -1), proj_s, mm_s.reshape(ns * dseq, -1),
                       mg_s.reshape(ns * dseq, -1), h_s, lw["w_out"], final_norm if last else None)
        s_rows.append((c_s, kr_s, sm_st[0], sm_st[1], sm_st[2], sg_st[0], sg_st[1]))

    y_prompt = h_p.reshape(nb, seq, d)
    y_sample = h_s.reshape(ns, dseq, d)
    stack = lambda rows, i: jnp.stack([r[i] for r in rows])
    return ((y_prompt, y_sample) + tuple(stack(p_rows, i) for i in range(7))
            + tuple(stack(s_rows, i) for i in range(7)))
```

```python
import functools
import math

import jax
import jax.numpy as jnp
import numpy as np
from jax import lax
from jax.experimental import pallas as pl
from jax.experimental.pallas import tpu as pltpu

F32 = jnp.float32
BF16 = jnp.bfloat16

EPS = 1e-6
ROPE_BASE = 10000.0
CHUNK = 64
N_HEADS_MLA = 8
MLA_NOPE, MLA_ROPE, MLA_V = 64, 32, 64
MLA_Q_LORA, MLA_KV_LORA = 384, 256
R_HEADS, R_DH = 4, 128
R_WIDTH = R_HEADS * R_DH
CONV_W = 4
TRI_BLOCK = 16
R_CHUNK = 128
LANES = 128
HEAD_PAD = 128
VT_ROWS = 80
SOFTMAX_SCALE = 1.0 / math.sqrt(MLA_NOPE + MLA_ROPE)
LOG2E = math.log2(math.e)
QK_LOOKAHEAD = 4

OFF_GQKV = 0
OFF_MQKV = 1536
OFF_MOZ = 3072
OFF_ZA = 4096
OFF_ZG = 4608
OFF_CKV = 5120
OFF_CQ = 5376
OFF_SMALL = 5760
NP = 5888
PREP_AFTER = 2048
SM_KR, SM_MI, SM_MF, SM_GA, SM_GB = 0, 32, 36, 40, 44

VMEM_LIMIT = 56 * 1024 * 1024


def _cparams(sem):
    return pltpu.CompilerParams(dimension_semantics=sem, vmem_limit_bytes=VMEM_LIMIT)


def _bf(x):
    return x.astype(BF16)


def _dot(a, b):
    return jnp.dot(a, b, preferred_element_type=F32)


def _dot_nt(a, b):
    return lax.dot_general(a, b, (((1,), (1,)), ((), ())), preferred_element_type=F32)


def _dot_tn(a, b):
    return lax.dot_general(a, b, (((0,), (0,)), ((), ())), preferred_element_type=F32)


def _rms(x, w):
    return x * lax.rsqrt(jnp.mean(x * x, axis=-1, keepdims=True) + EPS) * w


def _silu(x):
    return x * jax.nn.sigmoid(x)


def _rope128(x, tc, ts1, ts2):
    return x * tc + pltpu.roll(x, 16, 1) * ts1 + pltpu.roll(x, LANES - 16, 1) * ts2


def _inproj_chunks(xn, w_ref, o_ref, n0, n_end):
    while n0 < n_end:
        n1 = min(n0 + 512, n_end)
        o_ref[:, n0:n1] = _dot(xn, w_ref[:, n0:n1])
        n0 = n1


def _inproj_kernel(x_ref, nw_ref, w_ref, o_ref):
    _inproj_chunks(_bf(_rms(x_ref[...], nw_ref[...])), w_ref, o_ref, 0, NP)


def _inproj(x2d, norm_w, w_bf):
    rows, d = x2d.shape
    tm = min(rows, 512)
    return pl.pallas_call(
        _inproj_kernel,
        grid=(rows // tm,),
        in_specs=[
            pl.BlockSpec((tm, d), lambda i: (i, 0)),
            pl.BlockSpec((1, d), lambda i: (0, 0)),
            pl.BlockSpec((d, NP), lambda i: (0, 0), pipeline_mode=pl.Buffered(1)),
        ],
        out_specs=pl.BlockSpec((tm, NP), lambda i: (i, 0)),
        out_shape=jax.ShapeDtypeStruct((rows, NP), F32),
        compiler_params=_cparams(("arbitrary",)),
        name="inproj",
    )(x2d, norm_w.reshape(1, d), w_bf)


def _inproj_mla_kernel(x_ref, nw_ref, w_ref, tc_ref, ts1_ref, ts2_ref, qn_ref, wuq_ref, kvn_ref,
                       wuk_ref, wuvt_ref, vone_ref, o_ref, q_out, k_out, vt_out, cn_out, kr_out):
    xn = _bf(_rms(x_ref[...], nw_ref[...]))
    _inproj_chunks(xn, w_ref, o_ref, OFF_CKV, NP)
    _inproj_chunks(xn, w_ref, o_ref, 0, PREP_AFTER)
    tc, ts1, ts2 = tc_ref[...], ts1_ref[...], ts2_ref[...]
    cq = o_ref[:, OFF_CQ:OFF_CQ + MLA_Q_LORA]
    q = _dot(_bf(_rms(cq, qn_ref[...])), wuq_ref[...])
    cn = _rms(o_ref[:, OFF_CKV:OFF_CKV + MLA_KV_LORA], kvn_ref[...])
    cn_out[...] = cn
    cnb = _bf(cn)
    kn = _dot(cnb, wuk_ref[...])
    vt_out[0] = _bf(_dot_nt(wuvt_ref[...], cnb) + vone_ref[...])
    sm = o_ref[:, OFF_SMALL:OFF_SMALL + LANES]
    lane = lax.broadcasted_iota(jnp.int32, sm.shape, 1)
    kr = pltpu.roll(jnp.where(lane < MLA_ROPE, sm, 0.0), MLA_NOPE, 1)
    kr = _rope128(kr, tc, ts1, ts2)
    kr_out[...] = pltpu.roll(kr, LANES - MLA_NOPE, 1)[:, :MLA_ROPE]
    for h in range(N_HEADS_MLA):
        sl = slice(h * HEAD_PAD, (h + 1) * HEAD_PAD)
        q_out[:, sl] = _bf(_rope128(q[:, sl], tc, ts1, ts2) * (SOFTMAX_SCALE * LOG2E))
        k_out[:, sl] = _bf(kn[:, sl] + kr)
    _inproj_chunks(xn, w_ref, o_ref, PREP_AFTER, OFF_CKV)


def _inproj_mla(x2d, nb, tabs, lw):
    rows, d = x2d.shape
    seq = rows // nb
    tm = min(seq, 512)
    tpb = seq // tm
    hq = N_HEADS_MLA * HEAD_PAD
    hvt = N_HEADS_MLA * VT_ROWS
    const = lambda i: (0, 0)
    row_tile = lambda i: (i, 0)
    tab_tile = lambda i: (i % tpb, 0)
    outs = pl.pallas_call(
        _inproj_mla_kernel,
        grid=(rows // tm,),
        in_specs=[
            pl.BlockSpec((tm, d), row_tile),
            pl.BlockSpec((1, d), const),
            pl.BlockSpec((d, NP), const, pipeline_mode=pl.Buffered(1)),
            pl.BlockSpec((tm, LANES), tab_tile),
            pl.BlockSpec((tm, LANES), tab_tile),
            pl.BlockSpec((tm, LANES), tab_tile),
            pl.BlockSpec((1, MLA_Q_LORA), const),
            pl.BlockSpec((MLA_Q_LORA, hq), const),
            pl.BlockSpec((1, MLA_KV_LORA), const),
            pl.BlockSpec((MLA_KV_LORA, hq), const),
            pl.BlockSpec((hvt, MLA_KV_LORA), const),
            pl.BlockSpec((hvt, 1), const),
        ],
        out_specs=[
            pl.BlockSpec((tm, NP), row_tile),
            pl.BlockSpec((tm, hq), row_tile),
            pl.BlockSpec((tm, hq), row_tile),
            pl.BlockSpec((1, hvt, tm), lambda i: (i // tpb, 0, i % tpb)),
            pl.BlockSpec((tm, MLA_KV_LORA), row_tile),
            pl.BlockSpec((tm, MLA_ROPE), row_tile),
        ],
        out_shape=[
            jax.ShapeDtypeStruct((rows, NP), F32),
            jax.ShapeDtypeStruct((rows, hq), BF16),
            jax.ShapeDtypeStruct((rows, hq), BF16),
            jax.ShapeDtypeStruct((nb, hvt, seq), BF16),
            jax.ShapeDtypeStruct((rows, MLA_KV_LORA), F32),
            jax.ShapeDtypeStruct((rows, MLA_ROPE), F32),
        ],
        compiler_params=_cparams(("arbitrary",)),
        name="inproj_mla",
    )(x2d, lw["norm_w"].reshape(1, d), lw["w_in"], *tabs, lw["q_norm"], lw["w_uq"], lw["kv_norm"],
      lw["w_uk"], lw["w_uvt"], lw["v_one"])
    proj, q, k, vt, cn, kr = outs
    return (proj, q.reshape(nb, seq, hq), k.reshape(nb, seq, hq), vt,
            cn.reshape(nb, seq, MLA_KV_LORA), kr.reshape(nb, seq, MLA_ROPE))


def _attn_kernel(*refs, tq, tk, has_prefix, single_tile):
    if has_prefix:
        q_ref, k_ref, vt_ref, kp_ref, vtp_ref, o_ref, m_s, acc_s, st_s = refs
    else:
        q_ref, k_ref, vt_ref, o_ref, m_s, acc_s, st_s = refs
    i = pl.program_id(1)
    q0 = i * tq
    nfull = q0 // tk
    hsl = [slice(h * HEAD_PAD, (h + 1) * HEAD_PAD) for h in range(N_HEADS_MLA)]
    rsl = [slice(h * VT_ROWS, (h + 1) * VT_ROWS) for h in range(N_HEADS_MLA)]

    def scores(h, kt):
        return _dot_nt(kt, q_ref[0, :, hsl[h]])

    def softmax_pv(h, st, vt, mask):
        if mask is not None:
            st = jnp.where(mask, st, -jnp.inf)
        m_old = m_s[h]
        m_new = jnp.maximum(m_old, jnp.max(st, axis=0, keepdims=True))
        p = jnp.exp2(st - m_new)
        acc_s[h] = jnp.exp2(m_old - m_new) * acc_s[h] + _dot(vt, _bf(p))
        m_s[h] = m_new

    def tile(r0, mask=None, r0_next=None):
        st = [None] * N_HEADS_MLA
        for h in range(N_HEADS_MLA):
            ahead = h + QK_LOOKAHEAD
            if ahead < N_HEADS_MLA:
                st[ahead] = scores(ahead, k_ref[0, pl.ds(r0, tk), hsl[ahead]])
            cur = st_s[h] if h < QK_LOOKAHEAD else st[h]
            if ahead >= N_HEADS_MLA and r0_next is not None:
                hn = ahead - N_HEADS_MLA
                st_s[hn] = scores(hn, k_ref[0, pl.ds(r0_next, tk), hsl[hn]])
            softmax_pv(h, cur, vt_ref[0, rsl[h], pl.ds(r0, tk)], mask)

    for h in range(N_HEADS_MLA):
        m_s[h] = jnp.full((1, tq), -jnp.inf, F32)
        acc_s[h] = jnp.zeros((VT_ROWS, tq), F32)
    if has_prefix:
        st_pre = [scores(h, kp_ref[0, :, hsl[h]]) for h in range(N_HEADS_MLA)]
    for h in range(QK_LOOKAHEAD):
        st_s[h] = scores(h, k_ref[0, 0:tk, hsl[h]])
    if has_prefix:
        for h in range(N_HEADS_MLA):
            softmax_pv(h, st_pre[h], vtp_ref[0, rsl[h], :], None)

    def full_tile(j):
        tile(pl.multiple_of(j * tk, tk), None, pl.multiple_of((j + 1) * tk, tk))

    def body(j, c):
        full_tile(2 * j)
        full_tile(2 * j + 1)
        return c

    if single_tile:
        r0 = 0
    else:
        lax.fori_loop(0, nfull // 2, body, 0)

        @pl.when(nfull % 2 == 1)
        def _():
            full_tile(nfull - 1)

        r0 = pl.multiple_of(nfull * tk, tk)
    k_chunk = (nfull * tk + lax.broadcasted_iota(jnp.int32, (tk, tq), 0)) // CHUNK
    q_chunk = (q0 + lax.broadcasted_iota(jnp.int32, (tk, tq), 1)) // CHUNK
    tile(r0, k_chunk <= q_chunk)
    for hp in range(N_HEADS_MLA // 2):
        pair = []
        for h in (2 * hp, 2 * hp + 1):
            acc = acc_s[h]
            pair.append(acc[0:MLA_V] / acc[MLA_V:MLA_V + 1])
        o_ref[0, :, hp * LANES:(hp + 1) * LANES] = _bf(jnp.concatenate(pair, axis=0).T)


def _attention(q, k, vt, prefix, tq, tk):
    nb, seq, hq = q.shape
    hvt = vt.shape[1]
    in_specs = [
        pl.BlockSpec((1, tq, hq), lambda b, i: (b, i, 0)),
        pl.BlockSpec((1, seq, hq), lambda b, i: (b, 0, 0)),
        pl.BlockSpec((1, hvt, seq), lambda b, i: (b, 0, 0)),
    ]
    args = [q, k, vt]
    if prefix is not None:
        kp, vtp = prefix
        npre = kp.shape[1]
        in_specs += [pl.BlockSpec((1, npre, hq), lambda b, i: (0, 0, 0)),
                     pl.BlockSpec((1, hvt, npre), lambda b, i: (0, 0, 0))]
        args += [kp, vtp]
    return pl.pallas_call(
        functools.partial(_attn_kernel, tq=tq, tk=tk, has_prefix=prefix is not None,
                          single_tile=seq == tk),
        grid=(nb, seq // tq),
        in_specs=in_specs,
        out_specs=pl.BlockSpec((1, tq, N_HEADS_MLA * MLA_V), lambda b, i: (b, i, 0)),
        out_shape=jax.ShapeDtypeStruct((nb, seq, N_HEADS_MLA * MLA_V), BF16),
        scratch_shapes=[pltpu.VMEM((N_HEADS_MLA, 1, tq), F32),
                        pltpu.VMEM((N_HEADS_MLA, VT_ROWS, tq), F32),
                        pltpu.VMEM((QK_LOOKAHEAD, tk, tq), F32)],
        compiler_params=_cparams(("arbitrary", "arbitrary")),
        name="attn",
    )(*args)


def _sample_attn_kernel(cq_ref, ckv_ref, sm_ref, cc_ref, kc_ref, tc_ref, ts1_ref, ts2_ref, qn_ref,
                        wuq_ref, kvn_ref, wuk_ref, wuv_ref, o_ref, cn_out, kr_out, *, n_cache, tk, bb):
    tc, ts1, ts2 = tc_ref[...], ts1_ref[...], ts2_ref[...]
    t = cq_ref.shape[1]
    streams = range(bb)
    q = [_dot(_bf(_rms(cq_ref[b], qn_ref[...])), wuq_ref[...]) for b in streams]
    cn = [_rms(ckv_ref[b], kvn_ref[...]) for b in streams]
    krn = []
    for b in streams:
        cn_out[b] = cn[b]
        sm = sm_ref[b]
        lane = lax.broadcasted_iota(jnp.int32, sm.shape, 1)
        kr = pltpu.roll(jnp.where(lane < MLA_ROPE, sm, 0.0), MLA_NOPE, 1)
        kr = _rope128(kr, tc, ts1, ts2)
        krn.append(pltpu.roll(kr, LANES - MLA_NOPE, 1)[:, :MLA_ROPE])
        kr_out[b] = krn[b]
    qh = [[_rope128(q[b][:, h * HEAD_PAD:(h + 1) * HEAD_PAD], tc, ts1, ts2) for h in range(N_HEADS_MLA)]
          for b in streams]
    ql = [[_dot_nt(_bf(qh[b][h]), wuk_ref[:, h * HEAD_PAD:(h + 1) * HEAD_PAD]) for h in range(N_HEADS_MLA)]
          for b in streams]
    qlat = [_bf(jnp.concatenate(ql[b], axis=0) * (SOFTMAX_SCALE * LOG2E)) for b in streams]
    qrope = [_bf(jnp.concatenate([pltpu.roll(x, LANES - MLA_NOPE, 1)[:, :MLA_ROPE] for x in qh[b]], axis=0)
                 * (SOFTMAX_SCALE * LOG2E)) for b in streams]

    bounds = [(r0, min(r0 + tk, n_cache)) for r0 in range(0, n_cache, tk)]
    c_bf = [[_bf(cc_ref[b, r0:r1, :]) for r0, r1 in bounds] + [_bf(cn[b])] for b in streams]
    s = [[_dot_nt(qlat[b], c) + _dot(qrope[b], _bf(kc_ref[b, :, r0:r1]))
          for c, (r0, r1) in zip(c_bf[b], bounds)]
         + [_dot_nt(qlat[b], c_bf[b][-1]) + _dot_nt(qrope[b], _bf(krn[b]))] for b in streams]

    def lane_blocks(x):
        w = x.shape[1]
        return [x[:, i:i + LANES] for i in range(0, w, LANES)] if w % LANES == 0 else None

    def folded(xs, op):
        wide = [blk for x in xs if lane_blocks(x) for blk in lane_blocks(x)]
        narrow = [x for x in xs if lane_blocks(x) is None]
        return ([functools.reduce(op, wide)] if wide else []) + narrow

    m = [functools.reduce(jnp.maximum, [jnp.max(x, axis=1, keepdims=True) for x in folded(s[b], jnp.maximum)])
         for b in streams]
    p = [[jnp.exp2(x - m[b]) for x in s[b]] for b in streams]
    l = [functools.reduce(jnp.add, [jnp.sum(x, axis=1, keepdims=True) for x in folded(p[b], jnp.add)])
         for b in streams]
    acc = [functools.reduce(jnp.add, [_dot(_bf(x), c) for x, c in zip(p[b], c_bf[b])]) for b in streams]
    out_lane = lax.broadcasted_iota(jnp.int32, (t, N_HEADS_MLA * MLA_V), 1) // MLA_V
    for b in streams:
        olat = _bf(acc[b] / l[b])
        o = jnp.zeros((t, N_HEADS_MLA * MLA_V), F32)
        for h in range(N_HEADS_MLA):
            o = jnp.where(out_lane == h, _dot(olat[h * t:(h + 1) * t], wuv_ref[...]), o)
        o_ref[b] = _bf(o)


def _sample_attention(proj3, cache_c, cache_kr, layer, tabs, lw):
    nb, t, _ = proj3.shape
    n_cache = cache_c.shape[2]
    hq = N_HEADS_MLA * HEAD_PAD
    hv = N_HEADS_MLA * MLA_V
    bb = 2 if nb % 2 == 0 else 1
    const = lambda b: (0, 0)
    layer_blk = layer * nb // bb
    return pl.pallas_call(
        functools.partial(_sample_attn_kernel, n_cache=n_cache, tk=min(512, n_cache), bb=bb),
        grid=(nb // bb,),
        in_specs=[
            pl.BlockSpec((bb, t, MLA_Q_LORA), lambda b: (b, 0, OFF_CQ // MLA_Q_LORA)),
            pl.BlockSpec((bb, t, MLA_KV_LORA), lambda b: (b, 0, OFF_CKV // MLA_KV_LORA)),
            pl.BlockSpec((bb, t, LANES), lambda b: (b, 0, OFF_SMALL // LANES)),
            pl.BlockSpec((bb, n_cache, MLA_KV_LORA), lambda b: (layer_blk + b, 0, 0)),
            pl.BlockSpec((bb, MLA_ROPE, n_cache), lambda b: (layer_blk + b, 0, 0)),
            pl.BlockSpec((t, LANES), const),
            pl.BlockSpec((t, LANES), const),
            pl.BlockSpec((t, LANES), const),
            pl.BlockSpec((1, MLA_Q_LORA), const),
            pl.BlockSpec((MLA_Q_LORA, hq), const),
            pl.BlockSpec((1, MLA_KV_LORA), const),
            pl.BlockSpec((MLA_KV_LORA, hq), const),
            pl.BlockSpec((MLA_KV_LORA, hv), const),
        ],
        out_specs=[
            pl.BlockSpec((bb, t, hv), lambda b: (b, 0, 0)),
            pl.BlockSpec((bb, t, MLA_KV_LORA), lambda b: (b, 0, 0)),
            pl.BlockSpec((bb, t, MLA_ROPE), lambda b: (b, 0, 0)),
        ],
        out_shape=[
            jax.ShapeDtypeStruct((nb, t, hv), BF16),
            jax.ShapeDtypeStruct((nb, t, MLA_KV_LORA), F32),
            jax.ShapeDtypeStruct((nb, t, MLA_ROPE), F32),
        ],
        compiler_params=_cparams(("arbitrary",)),
        name="sample_attn",
    )(proj3, proj3, proj3, cache_c.reshape(-1, n_cache, MLA_KV_LORA),
      jnp.swapaxes(cache_kr, 2, 3).reshape(-1, MLA_ROPE, n_cache), *tabs, lw["q_norm"], lw["w_uq"],
      lw["kv_norm"],
      lw["w_uk"], lw["w_uv"])


def _chunk_masks(t):
    row = lax.broadcasted_iota(jnp.int32, (t, t), 0)
    col = lax.broadcasted_iota(jnp.int32, (t, t), 1)
    return row == col, col <= row, row <= col, col < row


def _to_row(col_vec, eye):
    return jnp.sum(jnp.where(eye, col_vec, 0.0), axis=0, keepdims=True)


def _mlstm_kernel(qkv_ref, oz_ref, sm_ref, gb_ref, nrm_ref, c0_ref, n0_ref, m0_ref,
                  out_ref, c_ref, n_ref, m_ref, *, bb, t, shared_init):
    @pl.when(pl.program_id(1) == 0)
    def _():
        for b in range(bb):
            src = 0 if shared_init else b
            c_ref[b] = c0_ref[src]
            n_ref[b] = n0_ref[src]
            m_ref[0, b:b + 1, :] = m0_ref[0, src:src + 1, :]

    eye, tril, triu, _ = _chunk_masks(t)
    chains = [(b, h) for b in range(bb) for h in range(R_HEADS)]
    i_col = [sm_ref[b, :, SM_MI + h:SM_MI + h + 1] + gb_ref[0:1, h:h + 1] for b, h in chains]
    f_col = [jax.nn.log_sigmoid(sm_ref[b, :, SM_MF + h:SM_MF + h + 1] + gb_ref[1:2, h:h + 1])
             for b, h in chains]
    i_row = [_to_row(x, eye) for x in i_col]
    f_row = [_to_row(x, eye) for x in f_col]
    b_col = [jnp.sum(jnp.where(tril, x, 0.0), axis=1, keepdims=True) for x in f_row]
    b_row = [jnp.sum(jnp.where(triu, x, 0.0), axis=0, keepdims=True) for x in f_col]
    dmat = [jnp.where(tril, bc - br + ir, -jnp.inf) for bc, br, ir in zip(b_col, b_row, i_row)]
    dmax = [jnp.max(x, axis=1, keepdims=True) for x in dmat]
    pre = []
    for (b, h), ic, bc, dm, dx in zip(chains, i_col, b_col, dmat, dmax):
        hs = slice(h * R_DH, (h + 1) * R_DH)
        q = qkv_ref[b, :, hs]
        k = qkv_ref[b, :, R_WIDTH + h * R_DH:R_WIDTH + (h + 1) * R_DH] * (R_DH ** -0.5)
        v = qkv_ref[b, :, 2 * R_WIDTH + h * R_DH:2 * R_WIDTH + (h + 1) * R_DH]
        m_prev = m_ref[0, b:b + 1, h:h + 1]
        inter = bc + m_prev
        m_t = jnp.maximum(inter, dx)
        m_new = m_t[t - 1:t, :]
        b_last = bc[t - 1:t, :]
        pre.append(dict(
            q=q, k=k, v=v, qb=_bf(q), kb=_bf(k), vb=_bf(v), m_t=m_t, m_new=m_new,
            w_inter=jnp.exp(inter - m_t), e=jnp.exp(dm - m_t),
            g_state=jnp.exp(b_last + m_prev - m_new),
            g_tok=jnp.exp(b_last - bc + ic - m_new)))
    qk_raw = [_dot_nt(p["qb"], p["kb"]) for p in pre]
    qc = [_dot_nt(p["qb"], _bf(c_ref[b, h])) for p, (b, h) in zip(pre, chains)]
    qk = [r * p["e"] for r, p in zip(qk_raw, pre)]
    pv = [_dot(_bf(x), p["vb"]) for x, p in zip(qk, pre)]
    upd = [_dot_tn(_bf(p["g_tok"] * p["v"]), p["kb"]) for p in pre]
    nvec = [n_ref[b, h:h + 1, :] for b, h in chains]
    qn = [jnp.sum(p["q"] * nv, axis=1, keepdims=True) for p, nv in zip(pre, nvec)]
    qks = [jnp.sum(x, axis=1, keepdims=True) for x in qk]
    hm = []
    for (b, h), p, qn_c, qks_c, qc_c, pv_c in zip(chains, pre, qn, qks, qc, pv):
        num = p["w_inter"] * qc_c + pv_c
        den = p["w_inter"] * qn_c + qks_c
        hh = num / jnp.maximum(jnp.abs(den), jnp.exp(-p["m_t"]))
        hm.append(jax.nn.sigmoid(oz_ref[b, :, h * R_DH:(h + 1) * R_DH]) * hh)
    ms = [jnp.mean(x * x, axis=-1, keepdims=True) for x in hm]
    for (b, h), p, nv, upd_c, hm_c, ms_c in zip(chains, pre, nvec, upd, hm, ms):
        hs = slice(h * R_DH, (h + 1) * R_DH)
        c_ref[b, h] = p["g_state"] * c_ref[b, h] + upd_c
        n_ref[b, h:h + 1, :] = p["g_state"] * nv + jnp.sum(p["g_tok"] * p["k"], axis=0, keepdims=True)
        m_ref[0, b:b + 1, h:h + 1] = p["m_new"]
        om = hm_c * lax.rsqrt(ms_c + EPS) * nrm_ref[0:1, hs]
        out_ref[b, :, hs] = _bf(om * _silu(oz_ref[b, :, R_WIDTH + h * R_DH:R_WIDTH + (h + 1) * R_DH]))


def _mlstm(proj3, lw, state, bb, t, shared_init):
    nb, seq, _ = proj3.shape
    c0, n0, m0 = state
    sb = 1 if shared_init else bb
    m0 = m0.reshape(-1, sb, R_HEADS)
    st = (lambda i, c: (0, 0, 0, 0)) if shared_init else (lambda i, c: (i, 0, 0, 0))
    st3 = (lambda i, c: (0, 0, 0)) if shared_init else (lambda i, c: (i, 0, 0))
    out, c1, n1, m1 = pl.pallas_call(
        functools.partial(_mlstm_kernel, bb=bb, t=t, shared_init=shared_init),
        grid=(nb // bb, seq // t),
        in_specs=[
            pl.BlockSpec((bb, t, 3 * R_WIDTH), lambda i, c: (i, c, OFF_MQKV // (3 * R_WIDTH))),
            pl.BlockSpec((bb, t, 2 * R_WIDTH), lambda i, c: (i, c, OFF_MOZ // (2 * R_WIDTH))),
            pl.BlockSpec((bb, t, LANES), lambda i, c: (i, c, OFF_SMALL // LANES)),
            pl.BlockSpec((2, R_HEADS), lambda i, c: (0, 0)),
            pl.BlockSpec((1, R_WIDTH), lambda i, c: (0, 0)),
            pl.BlockSpec((sb, R_HEADS, R_DH, R_DH), st),
            pl.BlockSpec((sb, R_HEADS, R_DH), st3),
            pl.BlockSpec((1, sb, R_HEADS), st3),
        ],
        out_specs=[
            pl.BlockSpec((bb, t, R_WIDTH), lambda i, c: (i, c, 0)),
            pl.BlockSpec((bb, R_HEADS, R_DH, R_DH), lambda i, c: (i, 0, 0, 0)),
            pl.BlockSpec((bb, R_HEADS, R_DH), lambda i, c: (i, 0, 0)),
            pl.BlockSpec((1, bb, R_HEADS), lambda i, c: (i, 0, 0)),
        ],
        out_shape=[
            jax.ShapeDtypeStruct((nb, seq, R_WIDTH), BF16),
            jax.ShapeDtypeStruct((nb, R_HEADS, R_DH, R_DH), F32),
            jax.ShapeDtypeStruct((nb, R_HEADS, R_DH), F32),
            jax.ShapeDtypeStruct((nb // bb, bb, R_HEADS), F32),
        ],
        compiler_params=_cparams(("arbitrary", "arbitrary")),
        name="mlstm",
    )(proj3, proj3, proj3, lw["m_gate_b"], lw["m_norm"], c0, n0, m0)
    return out, (c1, n1, m1.reshape(nb, R_HEADS))


def _neumann_all(a_list, t, nil):
    levels = int(math.log2(nil)) - 1
    n_acc = [-a for a in a_list]
    pw = [_dot(_bf(a), _bf(a)) for a in a_list]
    for _ in range(levels - 1):
        r = [_dot(_bf(jnp.concatenate([p, n], axis=0)), _bf(p)) for p, n in zip(pw, n_acc)]
        n_acc = [n + p + x[t:] for n, p, x in zip(n_acc, pw, r)]
        pw = [x[:t] for x in r]
    r = [_dot(_bf(n), _bf(p)) for p, n in zip(pw, n_acc)]
    return [n + p + x for n, p, x in zip(n_acc, pw, r)]


def _unit_lower_inverse_all(a_list, t):
    blk = min(t, TRI_BLOCK)
    row = lax.broadcasted_iota(jnp.int32, (t, t), 0)
    col = lax.broadcasted_iota(jnp.int32, (t, t), 1)
    shift = int(math.log2(blk))
    same = jnp.right_shift(row, shift) == jnp.right_shift(col, shift)
    n_list = _neumann_all([jnp.where(same, a, 0.0) for a in a_list], t, blk)
    if blk == t:
        return n_list
    eye = (row == col).astype(F32)
    w_list = [n + eye for n in n_list]
    while blk < t:
        inner = ((jnp.right_shift(row, shift + 1) == jnp.right_shift(col, shift + 1))
                 & (jnp.right_shift(row, shift) > jnp.right_shift(col, shift)))
        z = [_dot(_bf(jnp.where(inner, a, 0.0)), _bf(w)) for a, w in zip(a_list, w_list)]
        w_list = [w - _dot(_bf(w), _bf(x)) for w, x in zip(w_list, z)]
        blk *= 2
        shift += 1
    return [w - eye for w in w_list]


def _conv_silu_rows(xwin, r0, rb, cw_ref):
    xw = xwin[r0:r0 + rb + 8]
    conv = xw[8:8 + rb] * cw_ref[CONV_W - 1:CONV_W, :]
    for j in range(1, CONV_W):
        conv = conv + pltpu.roll(xw, j, 0)[8:8 + rb] * cw_ref[CONV_W - 1 - j:CONV_W - j, :]
    return _silu(conv)


def _gdn_kernel(x_ref, z_ref, sm_ref, cw_ref, alog_ref, dtb_ref, gn_ref, s0_ref, buf0_ref,
                out_ref, s_ref, buf_ref, xwin_ref, *, bb, t, shared_init):
    c = pl.program_id(1)

    @pl.when(c == 0)
    def _():
        for b in range(bb):
            src = 0 if shared_init else b
            s_ref[b] = s0_ref[src]
            xwin_ref[b, 0:8, :] = jnp.zeros((8, 3 * R_WIDTH), F32)
            xwin_ref[b, 8 - (CONV_W - 1):8, :] = buf0_ref[src]

    eye, tril, triu, strict = _chunk_masks(t)
    chains = [(b, h) for b in range(bb) for h in range(R_HEADS)]
    g_col = [-jnp.exp(alog_ref[0:1, h:h + 1])
             * jax.nn.softplus(sm_ref[b, :, SM_GA + h:SM_GA + h + 1] + dtb_ref[0:1, h:h + 1])
             for b, h in chains]
    g_row = [_to_row(x, eye) for x in g_col]
    gc_col = [jnp.sum(jnp.where(tril, x, 0.0), axis=1, keepdims=True) for x in g_row]
    gc_row = [jnp.sum(jnp.where(triu, x, 0.0), axis=0, keepdims=True) for x in g_col]
    acts = []
    for b in range(bb):
        xwin_ref[b, 8:8 + t, :] = x_ref[b]
        acts.append(_conv_silu_rows(xwin_ref.at[b], 0, t, cw_ref))
        tail = xwin_ref[b, 8 + t - (CONV_W - 1):8 + t, :]
        xwin_ref[b, 8 - (CONV_W - 1):8, :] = tail
        buf_ref[b] = tail
    gq = [acts[b][:, h * R_DH:(h + 1) * R_DH] for b, h in chains]
    gk = [acts[b][:, R_WIDTH + h * R_DH:R_WIDTH + (h + 1) * R_DH] for b, h in chains]
    q_ss = [jnp.sum(x * x, axis=1, keepdims=True) for x in gq]
    k_ss = [jnp.sum(x * x, axis=1, keepdims=True) for x in gk]
    pre = []
    for i, (b, h) in enumerate(chains):
        gv = acts[b][:, 2 * R_WIDTH + h * R_DH:2 * R_WIDTH + (h + 1) * R_DH]
        qn = gq[i] * lax.rsqrt(q_ss[i] + EPS) * (R_DH ** -0.5)
        kn = gk[i] * lax.rsqrt(k_ss[i] + EPS)
        beta = jax.nn.sigmoid(sm_ref[b, :, SM_GB + h:SM_GB + h + 1])
        eg = jnp.exp(gc_col[i])
        g_last = gc_col[i][t - 1:t, :]
        pre.append(dict(
            qn=qn, kn=kn, beta=beta, eg=eg, g_last=g_last,
            gam=jnp.exp(jnp.where(tril, gc_col[i] - gc_row[i], -jnp.inf)),
            rhs=jnp.concatenate([beta * gv, (beta * eg) * kn], axis=1),
            kdec=kn * jnp.exp(g_last - gc_col[i])))
    kq = [_dot_nt(_bf(jnp.concatenate([p["kn"], p["qn"]], axis=0)), _bf(p["kn"])) for p in pre]
    a_list = [jnp.where(strict, p["beta"] * x[:t] * p["gam"], 0.0) for p, x in zip(pre, kq)]
    n_inv = _unit_lower_inverse_all(a_list, t)
    sol = [p["rhs"] + _dot(_bf(n), _bf(p["rhs"])) for p, n in zip(pre, n_inv)]
    ws = [_dot(_bf(jnp.concatenate([s[:, R_DH:], p["qn"] * p["eg"]], axis=0)), _bf(s_ref[b, h]))
          for p, s, (b, h) in zip(pre, sol, chains)]
    delta = [_bf(s[:, :R_DH] - x[:t]) for s, x in zip(sol, ws)]
    o2 = [_dot(_bf(x[t:] * p["gam"]), d) for p, x, d in zip(pre, kq, delta)]
    upd = [_dot_tn(_bf(p["kdec"]), d) for p, d in zip(pre, delta)]
    o = [x[t:] + o2_c for x, o2_c in zip(ws, o2)]
    ms = [jnp.mean(x * x, axis=-1, keepdims=True) for x in o]
    for (b, h), p, o_c, ms_c, upd_c in zip(chains, pre, o, ms, upd):
        hs = slice(h * R_DH, (h + 1) * R_DH)
        s_ref[b, h] = jnp.exp(p["g_last"]) * s_ref[b, h] + upd_c
        out_ref[b, :, hs] = _bf(o_c * lax.rsqrt(ms_c + EPS) * gn_ref[...] * _silu(z_ref[b, :, hs]))


def _gdn(proj3, lw, state, bb, t, shared_init):
    nb, seq, _ = proj3.shape
    s0, buf0 = state
    sb = 1 if shared_init else bb
    st = (lambda i, c: (0, 0, 0, 0)) if shared_init else (lambda i, c: (i, 0, 0, 0))
    st3 = (lambda i, c: (0, 0, 0)) if shared_init else (lambda i, c: (i, 0, 0))
    const = lambda i, c: (0, 0)
    in_specs = [
        pl.BlockSpec((bb, t, 3 * R_WIDTH), lambda i, c: (i, c, OFF_GQKV // (3 * R_WIDTH))),
        pl.BlockSpec((bb, t, R_WIDTH), lambda i, c: (i, c, OFF_ZG // R_WIDTH)),
        pl.BlockSpec((bb, t, LANES), lambda i, c: (i, c, OFF_SMALL // LANES)),
        pl.BlockSpec((CONV_W, 3 * R_WIDTH), const),
        pl.BlockSpec((1, R_HEADS), const),
        pl.BlockSpec((1, R_HEADS), const),
        pl.BlockSpec((1, R_DH), const),
        pl.BlockSpec((sb, R_HEADS, R_DH, R_DH), st),
        pl.BlockSpec((sb, CONV_W - 1, 3 * R_WIDTH), st3),
    ]
    args = [proj3, proj3, proj3, lw["g_conv_w"], lw["g_a_log"], lw["g_dt_bias"], lw["g_norm"], s0, buf0]
    out_specs = [
        pl.BlockSpec((bb, t, R_WIDTH), lambda i, c: (i, c, 0)),
        pl.BlockSpec((bb, R_HEADS, R_DH, R_DH), lambda i, c: (i, 0, 0, 0)),
        pl.BlockSpec((bb, CONV_W - 1, 3 * R_WIDTH), lambda i, c: (i, 0, 0)),
    ]
    out_shape = [
        jax.ShapeDtypeStruct((nb, seq, R_WIDTH), BF16),
        jax.ShapeDtypeStruct((nb, R_HEADS, R_DH, R_DH), F32),
        jax.ShapeDtypeStruct((nb, CONV_W - 1, 3 * R_WIDTH), F32),
    ]
    out, s1, buf1 = pl.pallas_call(
        functools.partial(_gdn_kernel, bb=bb, t=t, shared_init=shared_init),
        grid=(nb // bb, seq // t),
        in_specs=in_specs,
        out_specs=out_specs,
        out_shape=out_shape,
        scratch_shapes=[pltpu.VMEM((bb, 8 + t, 3 * R_WIDTH), F32)],
        compiler_params=_cparams(("arbitrary", "arbitrary")),
        name="gdn",
    )(*args)
    return out, (s1, buf1)


def _outproj_kernel(*refs, final):
    if final:
        oa_ref, za_ref, mm_ref, mg_ref, x_ref, w_ref, fn_ref, y_ref = refs
    else:
        oa_ref, za_ref, mm_ref, mg_ref, x_ref, w_ref, y_ref = refs
    ma = oa_ref[...].astype(F32) * _silu(za_ref[...])
    acc = (_dot(_bf(ma), w_ref[0:R_WIDTH, :])
           + _dot(mm_ref[...], w_ref[R_WIDTH:2 * R_WIDTH, :])
           + _dot(mg_ref[...], w_ref[2 * R_WIDTH:3 * R_WIDTH, :]))
    hnew = x_ref[...] + acc
    y_ref[...] = _rms(hnew, fn_ref[...]) if final else hnew


def _outproj(oa, proj2, mm, mg, x2d, w_bf, final_norm):
    rows, d = x2d.shape
    tm = min(rows, 512)
    final = final_norm is not None
    in_specs = [
        pl.BlockSpec((tm, R_WIDTH), lambda i: (i, 0)),
        pl.BlockSpec((tm, R_WIDTH), lambda i: (i, OFF_ZA // R_WIDTH)),
        pl.BlockSpec((tm, R_WIDTH), lambda i: (i, 0)),
        pl.BlockSpec((tm, R_WIDTH), lambda i: (i, 0)),
        pl.BlockSpec((tm, d), lambda i: (i, 0)),
        pl.BlockSpec((3 * R_WIDTH, d), lambda i: (0, 0)),
    ]
    args = [oa, proj2, mm, mg, x2d, w_bf]
    if final:
        in_specs.append(pl.BlockSpec((1, d), lambda i: (0, 0)))
        args.append(final_norm.reshape(1, d))
    return pl.pallas_call(
        functools.partial(_outproj_kernel, final=final),
        grid=(rows // tm,),
        in_specs=in_specs,
        out_specs=pl.BlockSpec((tm, d), lambda i: (i, 0)),
        out_shape=jax.ShapeDtypeStruct((rows, d), F32),
        compiler_params=_cparams(("arbitrary",)),
        name="outproj",
    )(*args)


def _permute_w_in(w):
    d = w.shape[0]
    c_q, c_kv, k_r, z_a = w[:, 0:384], w[:, 384:640], w[:, 640:672], w[:, 672:1184]
    m_qkv, m_i, m_f = w[:, 1184:2720], w[:, 2720:2724], w[:, 2724:2728]
    m_oz = w[:, 2728:3752]
    g_qkv, g_a, g_b, z_g = w[:, 3752:5288], w[:, 5288:5292], w[:, 5292:5296], w[:, 5296:5808]
    small = jnp.concatenate([k_r, m_i, m_f, g_a, g_b, jnp.zeros((d, LANES - 48), w.dtype)], axis=1)
    return jnp.concatenate([g_qkv, m_qkv, m_oz, z_a, z_g, c_kv, c_q, small], axis=1).astype(BF16)


def _rope_tables(pos0, n):
    half = MLA_ROPE // 2
    freq = ROPE_BASE ** (-np.arange(half, dtype=np.float64) / half)
    ang = (pos0 + np.arange(n)).astype(np.float64)[:, None] * freq[None, :]
    cos, sin = np.cos(ang).astype(np.float32), np.sin(ang).astype(np.float32)
    one_lo = np.ones((n, MLA_NOPE), np.float32)
    one_hi = np.ones((n, LANES - MLA_NOPE - MLA_ROPE), np.float32)
    zero_lo = np.zeros((n, MLA_NOPE), np.float32)
    zero_hi = np.zeros((n, LANES - MLA_NOPE - MLA_ROPE), np.float32)
    zero_h = np.zeros((n, half), np.float32)
    tc = np.concatenate([one_lo, cos, cos, one_hi], axis=1)
    ts1 = np.concatenate([zero_lo, zero_h, sin, zero_hi], axis=1)
    ts2 = np.concatenate([zero_lo, -sin, zero_h, zero_hi], axis=1)
    return jnp.asarray(tc), jnp.asarray(ts1), jnp.asarray(ts2)


def _layer_weights(l, norm_w, w_in, mla_q_norm, mla_w_uq, mla_kv_norm, mla_w_uk, mla_w_uv, mlstm_gate_b,
                   mlstm_norm, gdn_conv_w, gdn_a_log, gdn_dt_bias, gdn_norm, w_out):
    pad = HEAD_PAD - (MLA_NOPE + MLA_ROPE)
    w_uq = mla_w_uq[l].reshape(MLA_Q_LORA, N_HEADS_MLA, MLA_NOPE + MLA_ROPE)
    w_uq = jnp.pad(w_uq, ((0, 0), (0, 0), (0, pad))).reshape(MLA_Q_LORA, N_HEADS_MLA * HEAD_PAD)
    w_uk = jnp.pad(mla_w_uk[l], ((0, 0), (0, 0), (0, HEAD_PAD - MLA_NOPE)))
    w_uvt = jnp.pad(jnp.transpose(mla_w_uv[l], (1, 2, 0)), ((0, 0), (0, VT_ROWS - MLA_V), (0, 0)))
    v_one = jnp.zeros((N_HEADS_MLA, VT_ROWS, 1), F32).at[:, MLA_V, :].set(1.0)
    return {
        "w_uvt": w_uvt.reshape(N_HEADS_MLA * VT_ROWS, MLA_KV_LORA).astype(BF16),
        "v_one": v_one.reshape(N_HEADS_MLA * VT_ROWS, 1),
        "norm_w": norm_w[l],
        "w_in": _permute_w_in(w_in[l]),
        "q_norm": mla_q_norm[l].reshape(1, -1),
        "w_uq": w_uq.astype(BF16),
        "kv_norm": mla_kv_norm[l].reshape(1, -1),
        "w_uk": w_uk.reshape(MLA_KV_LORA, N_HEADS_MLA * HEAD_PAD).astype(BF16),
        "w_uv": mla_w_uv[l].reshape(MLA_KV_LORA, N_HEADS_MLA * MLA_V).astype(BF16),
        "m_gate_b": mlstm_gate_b[l],
        "m_norm": mlstm_norm[l].reshape(1, -1),
        "g_conv_w": gdn_conv_w[l],
        "g_a_log": gdn_a_log[l].reshape(1, -1),
        "g_dt_bias": gdn_dt_bias[l].reshape(1, -1),
        "g_norm": gdn_norm[l].reshape(1, -1),
        "w_out": w_out[l].astype(BF16),
    }


def _recurrent_groups(proj3, lw, m_state, g_state, bb, t, shared_init):
    mm, m_state = _mlstm(proj3, lw, m_state, bb, t, shared_init)
    mg, g_state = _gdn(proj3, lw, g_state, bb, t, shared_init)
    return mm, mg, m_state, g_state


def kernel(x_prompt, x_sample, cache_mla_latent, cache_mla_krope, state_mlstm_C, state_mlstm_n, state_mlstm_m, state_gdn_S, state_gdn_conv, meta_tokens, norm_w, w_in, mla_q_norm, mla_w_uq, mla_kv_norm, mla_w_uk, mla_w_uv, mlstm_gate_b, mlstm_norm, gdn_conv_w, gdn_a_log, gdn_dt_bias, gdn_norm, w_out, final_norm):
    nb, seq, d = x_prompt.shape
    ns, dseq, _ = x_sample.shape
    n_meta = meta_tokens.shape[0]
    n_cache = cache_mla_latent.shape[2]
    depth = norm_w.shape[0]
    assert seq % 256 == 0 and n_meta % 8 == 0 and dseq % 8 == 0 and n_meta <= CHUNK and dseq <= CHUNK

    tabs_m = _rope_tables(0, n_meta)
    tabs_p = _rope_tables(n_meta, seq)
    tabs_s = _rope_tables(n_cache, dseq)

    h_m = meta_tokens.astype(F32)
    h_p = x_prompt.reshape(nb * seq, d)
    h_s = x_sample.reshape(ns * dseq, d)
    zero_m = (jnp.zeros((1, R_HEADS, R_DH, R_DH), F32), jnp.zeros((1, R_HEADS, R_DH), F32),
              jnp.zeros((1, R_HEADS), F32))
    zero_g = (jnp.zeros((1, R_HEADS, R_DH, R_DH), F32), jnp.zeros((1, CONV_W - 1, 3 * R_WIDTH), F32))
    bb_p = 2 if nb % 2 == 0 else 1
    bb_s = 4 if ns % 4 == 0 else 1
    p_rows, s_rows = [], []
    for l in range(depth):
        lw = _layer_weights(l, norm_w, w_in, mla_q_norm, mla_w_uq, mla_kv_norm, mla_w_uk, mla_w_uv,
                            mlstm_gate_b, mlstm_norm, gdn_conv_w, gdn_a_log, gdn_dt_bias, gdn_norm, w_out)
        last = l == depth - 1

        proj_m, q_m, k_m, v_m, c_m, kr_m = _inproj_mla(h_m, 1, tabs_m, lw)
        proj_m3 = proj_m.reshape(1, n_meta, NP)
        oa_m = _attention(q_m, k_m, v_m, None, n_meta, n_meta)
        mm_m, mg_m, mst, gst = _recurrent_groups(proj_m3, lw, zero_m, zero_g, 1, n_meta, True)
        if not last:
            h_m = _outproj(oa_m.reshape(n_meta, -1), proj_m, mm_m.reshape(n_meta, -1),
                           mg_m.reshape(n_meta, -1), h_m, lw["w_out"], None)

        proj_p, q_p, k_p, v_p, c_p, kr_p = _inproj_mla(h_p, nb, tabs_p, lw)
        proj_p3 = proj_p.reshape(nb, seq, NP)
        oa_p = _attention(q_p, k_p, v_p, (k_m, v_m), 256, 256)
        mm_p, mg_p, mst, gst = _recurrent_groups(proj_p3, lw, mst, gst, bb_p, R_CHUNK, True)
        h_p = _outproj(oa_p.reshape(nb * seq, -1), proj_p, mm_p.reshape(nb * seq, -1),
                       mg_p.reshape(nb * seq, -1), h_p, lw["w_out"], final_norm if last else None)
        p_rows.append((
            jnp.concatenate([jnp.broadcast_to(c_m, (nb,) + c_m.shape[1:]), c_p], axis=1),
            jnp.concatenate([jnp.broadcast_to(kr_m, (nb,) + kr_m.shape[1:]), kr_p], axis=1),
            mst[0], mst[1], mst[2], gst[0], gst[1]))

        proj_s = _inproj(h_s, lw["norm_w"], lw["w_in"])
        proj_s3 = proj_s.reshape(ns, dseq, NP)
        oa_s, c_s, kr_s = _sample_attention(proj_s3, cache_mla_latent, cache_mla_krope, l, tabs_s, lw)
        mm_s, mg_s, sm_st, sg_st = _recurrent_groups(
            proj_s3, lw, (state_mlstm_C[l], state_mlstm_n[l], state_mlstm_m[l]),
            (state_gdn_S[l], state_gdn_conv[l]), bb_s, dseq, False)
        h_s = _outproj(oa_s.reshape(ns * dseq, -1), proj_s, mm_s.reshape(ns * dseq, -1),
                       mg_s.reshape(ns * dseq, -1), h_s, lw["w_out"], final_norm if last else None)
        s_rows.append((c_s, kr_s, sm_st[0], sm_st[1], sm_st[2], sg_st[0], sg_st[1]))

    y_prompt = h_p.reshape(nb, seq, d)
    y_sample = h_s.reshape(ns, dseq, d)
    stack = lambda rows, i: jnp.stack([r[i] for r in rows])
    return ((y_prompt, y_sample) + tuple(stack(p_rows, i) for i in range(7))
            + tuple(stack(s_rows, i) for i in range(7)))
```

```python
import functools
import math

import jax
import jax.numpy as jnp
import numpy as np
from jax import lax
from jax.experimental import pallas as pl
from jax.experimental.pallas import tpu as pltpu

F32 = jnp.float32
BF16 = jnp.bfloat16

EPS = 1e-6
ROPE_BASE = 10000.0
CHUNK = 64
N_HEADS_MLA = 8
MLA_NOPE, MLA_ROPE, MLA_V = 64, 32, 64
MLA_Q_LORA, MLA_KV_LORA = 384, 256
R_HEADS, R_DH = 4, 128
R_WIDTH = R_HEADS * R_DH
CONV_W = 4
TRI_BLOCK = 16
R_CHUNK = 128
LANES = 128
HEAD_PAD = 128
VT_ROWS = 80
SOFTMAX_SCALE = 1.0 / math.sqrt(MLA_NOPE + MLA_ROPE)
LOG2E = math.log2(math.e)
QK_LOOKAHEAD = 5

OFF_GQKV = 0
OFF_MQKV = 1536
OFF_MOZ = 3072
OFF_ZA = 4096
OFF_ZG = 4608
OFF_CKV = 5120
OFF_CQ = 5376
OFF_SMALL = 5760
NP = 5888
PREP_AFTER = 2048
SM_KR, SM_MI, SM_MF, SM_GA, SM_GB = 0, 32, 36, 40, 44

VMEM_LIMIT = 56 * 1024 * 1024


def _cparams(sem):
    return pltpu.CompilerParams(dimension_semantics=sem, vmem_limit_bytes=VMEM_LIMIT)


def _bf(x):
    return x.astype(BF16)


def _dot(a, b):
    return jnp.dot(a, b, preferred_element_type=F32)


def _dot_nt(a, b):
    return lax.dot_general(a, b, (((1,), (1,)), ((), ())), preferred_element_type=F32)


def _dot_tn(a, b):
    return lax.dot_general(a, b, (((0,), (0,)), ((), ())), preferred_element_type=F32)


def _rms(x, w):
    return x * lax.rsqrt(jnp.mean(x * x, axis=-1, keepdims=True) + EPS) * w


def _silu(x):
    return x * jax.nn.sigmoid(x)


def _rope128(x, tc, ts1, ts2):
    return x * tc + pltpu.roll(x, 16, 1) * ts1 + pltpu.roll(x, LANES - 16, 1) * ts2


def _inproj_chunks(xn, w_ref, o_ref, n0, n_end):
    while n0 < n_end:
        n1 = min(n0 + 512, n_end)
        o_ref[:, n0:n1] = _dot(xn, w_ref[:, n0:n1])
        n0 = n1


def _inproj_kernel(x_ref, nw_ref, w_ref, o_ref):
    _inproj_chunks(_bf(_rms(x_ref[...], nw_ref[...])), w_ref, o_ref, 0, NP)


def _inproj(x2d, norm_w, w_bf):
    rows, d = x2d.shape
    tm = min(rows, 512)
    return pl.pallas_call(
        _inproj_kernel,
        grid=(rows // tm,),
        in_specs=[
            pl.BlockSpec((tm, d), lambda i: (i, 0)),
            pl.BlockSpec((1, d), lambda i: (0, 0)),
            pl.BlockSpec((d, NP), lambda i: (0, 0), pipeline_mode=pl.Buffered(1)),
        ],
        out_specs=pl.BlockSpec((tm, NP), lambda i: (i, 0)),
        out_shape=jax.ShapeDtypeStruct((rows, NP), F32),
        compiler_params=_cparams(("arbitrary",)),
        name="inproj",
    )(x2d, norm_w.reshape(1, d), w_bf)


def _inproj_mla_kernel(x_ref, nw_ref, w_ref, tc_ref, ts1_ref, ts2_ref, qn_ref, wuq_ref, kvn_ref,
                       wuk_ref, wuvt_ref, vone_ref, o_ref, q_out, k_out, vt_out, cn_out, kr_out):
    xn = _bf(_rms(x_ref[...], nw_ref[...]))
    _inproj_chunks(xn, w_ref, o_ref, OFF_CKV, NP)
    _inproj_chunks(xn, w_ref, o_ref, 0, PREP_AFTER)
    tc, ts1, ts2 = tc_ref[...], ts1_ref[...], ts2_ref[...]
    cq = o_ref[:, OFF_CQ:OFF_CQ + MLA_Q_LORA]
    q = _dot(_bf(_rms(cq, qn_ref[...])), wuq_ref[...])
    cn = _rms(o_ref[:, OFF_CKV:OFF_CKV + MLA_KV_LORA], kvn_ref[...])
    cn_out[...] = cn
    cnb = _bf(cn)
    kn = _dot(cnb, wuk_ref[...])
    vt_out[0] = _bf(_dot_nt(wuvt_ref[...], cnb) + vone_ref[...])
    sm = o_ref[:, OFF_SMALL:OFF_SMALL + LANES]
    lane = lax.broadcasted_iota(jnp.int32, sm.shape, 1)
    kr = pltpu.roll(jnp.where(lane < MLA_ROPE, sm, 0.0), MLA_NOPE, 1)
    kr = _rope128(kr, tc, ts1, ts2)
    kr_out[...] = pltpu.roll(kr, LANES - MLA_NOPE, 1)[:, :MLA_ROPE]
    for h in range(N_HEADS_MLA):
        sl = slice(h * HEAD_PAD, (h + 1) * HEAD_PAD)
        q_out[:, sl] = _bf(_rope128(q[:, sl], tc, ts1, ts2) * (SOFTMAX_SCALE * LOG2E))
        k_out[:, sl] = _bf(kn[:, sl] + kr)
    _inproj_chunks(xn, w_ref, o_ref, PREP_AFTER, OFF_CKV)


def _inproj_mla(x2d, nb, tabs, lw):
    rows, d = x2d.shape
    seq = rows // nb
    tm = min(seq, 512)
    tpb = seq // tm
    hq = N_HEADS_MLA * HEAD_PAD
    hvt = N_HEADS_MLA * VT_ROWS
    const = lambda i: (0, 0)
    row_tile = lambda i: (i, 0)
    tab_tile = lambda i: (i % tpb, 0)
    outs = pl.pallas_call(
        _inproj_mla_kernel,
        grid=(rows // tm,),
        in_specs=[
            pl.BlockSpec((tm, d), row_tile),
            pl.BlockSpec((1, d), const),
            pl.BlockSpec((d, NP), const, pipeline_mode=pl.Buffered(1)),
            pl.BlockSpec((tm, LANES), tab_tile),
            pl.BlockSpec((tm, LANES), tab_tile),
            pl.BlockSpec((tm, LANES), tab_tile),
            pl.BlockSpec((1, MLA_Q_LORA), const),
            pl.BlockSpec((MLA_Q_LORA, hq), const),
            pl.BlockSpec((1, MLA_KV_LORA), const),
            pl.BlockSpec((MLA_KV_LORA, hq), const),
            pl.BlockSpec((hvt, MLA_KV_LORA), const),
            pl.BlockSpec((hvt, 1), const),
        ],
        out_specs=[
            pl.BlockSpec((tm, NP), row_tile),
            pl.BlockSpec((tm, hq), row_tile),
            pl.BlockSpec((tm, hq), row_tile),
            pl.BlockSpec((1, hvt, tm), lambda i: (i // tpb, 0, i % tpb)),
            pl.BlockSpec((tm, MLA_KV_LORA), row_tile),
            pl.BlockSpec((tm, MLA_ROPE), row_tile),
        ],
        out_shape=[
            jax.ShapeDtypeStruct((rows, NP), F32),
            jax.ShapeDtypeStruct((rows, hq), BF16),
            jax.ShapeDtypeStruct((rows, hq), BF16),
            jax.ShapeDtypeStruct((nb, hvt, seq), BF16),
            jax.ShapeDtypeStruct((rows, MLA_KV_LORA), F32),
            jax.ShapeDtypeStruct((rows, MLA_ROPE), F32),
        ],
        compiler_params=_cparams(("arbitrary",)),
        name="inproj_mla",
    )(x2d, lw["norm_w"].reshape(1, d), lw["w_in"], *tabs, lw["q_norm"], lw["w_uq"], lw["kv_norm"],
      lw["w_uk"], lw["w_uvt"], lw["v_one"])
    proj, q, k, vt, cn, kr = outs
    return (proj, q.reshape(nb, seq, hq), k.reshape(nb, seq, hq), vt,
            cn.reshape(nb, seq, MLA_KV_LORA), kr.reshape(nb, seq, MLA_ROPE))


def _attn_kernel(*refs, tq, tk, has_prefix, single_tile):
    if has_prefix:
        q_ref, k_ref, vt_ref, kp_ref, vtp_ref, o_ref, m_s, acc_s, st_s = refs
    else:
        q_ref, k_ref, vt_ref, o_ref, m_s, acc_s, st_s = refs
    i = pl.program_id(1)
    q0 = i * tq
    nfull = q0 // tk
    hsl = [slice(h * HEAD_PAD, (h + 1) * HEAD_PAD) for h in range(N_HEADS_MLA)]
    rsl = [slice(h * VT_ROWS, (h + 1) * VT_ROWS) for h in range(N_HEADS_MLA)]

    def scores(h, kt):
        return _dot_nt(kt, q_ref[0, :, hsl[h]])

    def softmax_pv(h, st, vt, mask):
        if mask is not None:
            st = jnp.where(mask, st, -jnp.inf)
        m_old = m_s[h]
        m_new = jnp.maximum(m_old, jnp.max(st, axis=0, keepdims=True))
        p = jnp.exp2(st - m_new)
        acc_s[h] = jnp.exp2(m_old - m_new) * acc_s[h] + _dot(vt, _bf(p))
        m_s[h] = m_new

    def tile(r0, mask=None, r0_next=None):
        st = [None] * N_HEADS_MLA
        for h in range(N_HEADS_MLA):
            ahead = h + QK_LOOKAHEAD
            if ahead < N_HEADS_MLA:
                st[ahead] = scores(ahead, k_ref[0, pl.ds(r0, tk), hsl[ahead]])
            cur = st_s[h] if h < QK_LOOKAHEAD else st[h]
            if ahead >= N_HEADS_MLA and r0_next is not None:
                hn = ahead - N_HEADS_MLA
                st_s[hn] = scores(hn, k_ref[0, pl.ds(r0_next, tk), hsl[hn]])
            softmax_pv(h, cur, vt_ref[0, rsl[h], pl.ds(r0, tk)], mask)

    for h in range(N_HEADS_MLA):
        m_s[h] = jnp.full((1, tq), -jnp.inf, F32)
        acc_s[h] = jnp.zeros((VT_ROWS, tq), F32)
    if has_prefix:
        st_pre = [scores(h, kp_ref[0, :, hsl[h]]) for h in range(N_HEADS_MLA)]
    for h in range(QK_LOOKAHEAD):
        st_s[h] = scores(h, k_ref[0, 0:tk, hsl[h]])
    if has_prefix:
        for h in range(N_HEADS_MLA):
            softmax_pv(h, st_pre[h], vtp_ref[0, rsl[h], :], None)

    def full_tile(j):
        tile(pl.multiple_of(j * tk, tk), None, pl.multiple_of((j + 1) * tk, tk))

    def body(j, c):
        full_tile(2 * j)
        full_tile(2 * j + 1)
        return c

    if single_tile:
        r0 = 0
    else:
        lax.fori_loop(0, nfull // 2, body, 0)

        @pl.when(nfull % 2 == 1)
        def _():
            full_tile(nfull - 1)

        r0 = pl.multiple_of(nfull * tk, tk)
    k_chunk = (nfull * tk + lax.broadcasted_iota(jnp.int32, (tk, tq), 0)) // CHUNK
    q_chunk = (q0 + lax.broadcasted_iota(jnp.int32, (tk, tq), 1)) // CHUNK
    tile(r0, k_chunk <= q_chunk)
    for hp in range(N_HEADS_MLA // 2):
        pair = []
        for h in (2 * hp, 2 * hp + 1):
            acc = acc_s[h]
            pair.append(acc[0:MLA_V] / acc[MLA_V:MLA_V + 1])
        o_ref[0, :, hp * LANES:(hp + 1) * LANES] = _bf(jnp.concatenate(pair, axis=0).T)


def _attention(q, k, vt, prefix, tq, tk):
    nb, seq, hq = q.shape
    hvt = vt.shape[1]
    in_specs = [
        pl.BlockSpec((1, tq, hq), lambda b, i: (b, i, 0)),
        pl.BlockSpec((1, seq, hq), lambda b, i: (b, 0, 0)),
        pl.BlockSpec((1, hvt, seq), lambda b, i: (b, 0, 0)),
    ]
    args = [q, k, vt]
    if prefix is not None:
        kp, vtp = prefix
        npre = kp.shape[1]
        in_specs += [pl.BlockSpec((1, npre, hq), lambda b, i: (0, 0, 0)),
                     pl.BlockSpec((1, hvt, npre), lambda b, i: (0, 0, 0))]
        args += [kp, vtp]
    return pl.pallas_call(
        functools.partial(_attn_kernel, tq=tq, tk=tk, has_prefix=prefix is not None,
                          single_tile=seq == tk),
        grid=(nb, seq // tq),
        in_specs=in_specs,
        out_specs=pl.BlockSpec((1, tq, N_HEADS_MLA * MLA_V), lambda b, i: (b, i, 0)),
        out_shape=jax.ShapeDtypeStruct((nb, seq, N_HEADS_MLA * MLA_V), BF16),
        scratch_shapes=[pltpu.VMEM((N_HEADS_MLA, 1, tq), F32),
                        pltpu.VMEM((N_HEADS_MLA, VT_ROWS, tq), F32),
                        pltpu.VMEM((QK_LOOKAHEAD, tk, tq), F32)],
        compiler_params=_cparams(("arbitrary", "arbitrary")),
        name="attn",
    )(*args)


def _sample_attn_kernel(cq_ref, ckv_ref, sm_ref, cc_ref, kc_ref, tc_ref, ts1_ref, ts2_ref, qn_ref,
                        wuq_ref, kvn_ref, wuk_ref, wuv_ref, o_ref, cn_out, kr_out, *, n_cache, tk, bb):
    tc, ts1, ts2 = tc_ref[...], ts1_ref[...], ts2_ref[...]
    t = cq_ref.shape[1]
    streams = range(bb)
    q = [_dot(_bf(_rms(cq_ref[b], qn_ref[...])), wuq_ref[...]) for b in streams]
    cn = [_rms(ckv_ref[b], kvn_ref[...]) for b in streams]
    krn = []
    for b in streams:
        cn_out[b] = cn[b]
        sm = sm_ref[b]
        lane = lax.broadcasted_iota(jnp.int32, sm.shape, 1)
        kr = pltpu.roll(jnp.where(lane < MLA_ROPE, sm, 0.0), MLA_NOPE, 1)
        kr = _rope128(kr, tc, ts1, ts2)
        krn.append(pltpu.roll(kr, LANES - MLA_NOPE, 1)[:, :MLA_ROPE])
        kr_out[b] = krn[b]
    qh = [[_rope128(q[b][:, h * HEAD_PAD:(h + 1) * HEAD_PAD], tc, ts1, ts2) for h in range(N_HEADS_MLA)]
          for b in streams]
    ql = [[_dot_nt(_bf(qh[b][h]), wuk_ref[:, h * HEAD_PAD:(h + 1) * HEAD_PAD]) for h in range(N_HEADS_MLA)]
          for b in streams]
    qlat = [_bf(jnp.concatenate(ql[b], axis=0) * (SOFTMAX_SCALE * LOG2E)) for b in streams]
    qrope = [_bf(jnp.concatenate([pltpu.roll(x, LANES - MLA_NOPE, 1)[:, :MLA_ROPE] for x in qh[b]], axis=0)
                 * (SOFTMAX_SCALE * LOG2E)) for b in streams]

    bounds = [(r0, min(r0 + tk, n_cache)) for r0 in range(0, n_cache, tk)]
    c_bf = [[_bf(cc_ref[b, r0:r1, :]) for r0, r1 in bounds] + [_bf(cn[b])] for b in streams]
    s = [[_dot_nt(qlat[b], c) + _dot(qrope[b], _bf(kc_ref[b, :, r0:r1]))
          for c, (r0, r1) in zip(c_bf[b], bounds)]
         + [_dot_nt(qlat[b], c_bf[b][-1]) + _dot_nt(qrope[b], _bf(krn[b]))] for b in streams]

    def lane_blocks(x):
        w = x.shape[1]
        return [x[:, i:i + LANES] for i in range(0, w, LANES)] if w % LANES == 0 else None

    def folded(xs, op):
        wide = [blk for x in xs if lane_blocks(x) for blk in lane_blocks(x)]
        narrow = [x for x in xs if lane_blocks(x) is None]
        return ([functools.reduce(op, wide)] if wide else []) + narrow

    m = [functools.reduce(jnp.maximum, [jnp.max(x, axis=1, keepdims=True) for x in folded(s[b], jnp.maximum)])
         for b in streams]
    p = [[jnp.exp2(x - m[b]) for x in s[b]] for b in streams]
    l = [functools.reduce(jnp.add, [jnp.sum(x, axis=1, keepdims=True) for x in folded(p[b], jnp.add)])
         for b in streams]
    acc = [functools.reduce(jnp.add, [_dot(_bf(x), c) for x, c in zip(p[b], c_bf[b])]) for b in streams]
    out_lane = lax.broadcasted_iota(jnp.int32, (t, N_HEADS_MLA * MLA_V), 1) // MLA_V
    for b in streams:
        olat = _bf(acc[b] / l[b])
        o = jnp.zeros((t, N_HEADS_MLA * MLA_V), F32)
        for h in range(N_HEADS_MLA):
            o = jnp.where(out_lane == h, _dot(olat[h * t:(h + 1) * t], wuv_ref[...]), o)
        o_ref[b] = _bf(o)


def _sample_attention(proj3, cache_c, cache_kr, layer, tabs, lw):
    nb, t, _ = proj3.shape
    n_cache = cache_c.shape[2]
    hq = N_HEADS_MLA * HEAD_PAD
    hv = N_HEADS_MLA * MLA_V
    bb = 2 if nb % 2 == 0 else 1
    const = lambda b: (0, 0)
    layer_blk = layer * nb // bb
    return pl.pallas_call(
        functools.partial(_sample_attn_kernel, n_cache=n_cache, tk=min(512, n_cache), bb=bb),
        grid=(nb // bb,),
        in_specs=[
            pl.BlockSpec((bb, t, MLA_Q_LORA), lambda b: (b, 0, OFF_CQ // MLA_Q_LORA)),
            pl.BlockSpec((bb, t, MLA_KV_LORA), lambda b: (b, 0, OFF_CKV // MLA_KV_LORA)),
            pl.BlockSpec((bb, t, LANES), lambda b: (b, 0, OFF_SMALL // LANES)),
            pl.BlockSpec((bb, n_cache, MLA_KV_LORA), lambda b: (layer_blk + b, 0, 0)),
            pl.BlockSpec((bb, MLA_ROPE, n_cache), lambda b: (layer_blk + b, 0, 0)),
            pl.BlockSpec((t, LANES), const),
            pl.BlockSpec((t, LANES), const),
            pl.BlockSpec((t, LANES), const),
            pl.BlockSpec((1, MLA_Q_LORA), const),
            pl.BlockSpec((MLA_Q_LORA, hq), const),
            pl.BlockSpec((1, MLA_KV_LORA), const),
            pl.BlockSpec((MLA_KV_LORA, hq), const),
            pl.BlockSpec((MLA_KV_LORA, hv), const),
        ],
        out_specs=[
            pl.BlockSpec((bb, t, hv), lambda b: (b, 0, 0)),
            pl.BlockSpec((bb, t, MLA_KV_LORA), lambda b: (b, 0, 0)),
            pl.BlockSpec((bb, t, MLA_ROPE), lambda b: (b, 0, 0)),
        ],
        out_shape=[
            jax.ShapeDtypeStruct((nb, t, hv), BF16),
            jax.ShapeDtypeStruct((nb, t, MLA_KV_LORA), F32),
            jax.ShapeDtypeStruct((nb, t, MLA_ROPE), F32),
        ],
        compiler_params=_cparams(("arbitrary",)),
        name="sample_attn",
    )(proj3, proj3, proj3, cache_c.reshape(-1, n_cache, MLA_KV_LORA),
      jnp.swapaxes(cache_kr, 2, 3).reshape(-1, MLA_ROPE, n_cache), *tabs, lw["q_norm"], lw["w_uq"],
      lw["kv_norm"],
      lw["w_uk"], lw["w_uv"])


def _chunk_masks(t):
    row = lax.broadcasted_iota(jnp.int32, (t, t), 0)
    col = lax.broadcasted_iota(jnp.int32, (t, t), 1)
    return row == col, col <= row, row <= col, col < row


def _to_row(col_vec, eye):
    return jnp.sum(jnp.where(eye, col_vec, 0.0), axis=0, keepdims=True)


def _mlstm_kernel(qkv_ref, oz_ref, sm_ref, gb_ref, nrm_ref, c0_ref, n0_ref, m0_ref,
                  out_ref, c_ref, n_ref, m_ref, *, bb, t, shared_init):
    @pl.when(pl.program_id(1) == 0)
    def _():
        for b in range(bb):
            src = 0 if shared_init else b
            c_ref[b] = c0_ref[src]
            n_ref[b] = n0_ref[src]
            m_ref[0, b:b + 1, :] = m0_ref[0, src:src + 1, :]

    eye, tril, triu, _ = _chunk_masks(t)
    chains = [(b, h) for b in range(bb) for h in range(R_HEADS)]
    i_col = [sm_ref[b, :, SM_MI + h:SM_MI + h + 1] + gb_ref[0:1, h:h + 1] for b, h in chains]
    f_col = [jax.nn.log_sigmoid(sm_ref[b, :, SM_MF + h:SM_MF + h + 1] + gb_ref[1:2, h:h + 1])
             for b, h in chains]
    i_row = [_to_row(x, eye) for x in i_col]
    f_row = [_to_row(x, eye) for x in f_col]
    b_col = [jnp.sum(jnp.where(tril, x, 0.0), axis=1, keepdims=True) for x in f_row]
    b_row = [jnp.sum(jnp.where(triu, x, 0.0), axis=0, keepdims=True) for x in f_col]
    dmat = [jnp.where(tril, bc - br + ir, -jnp.inf) for bc, br, ir in zip(b_col, b_row, i_row)]
    dmax = [jnp.max(x, axis=1, keepdims=True) for x in dmat]
    pre = []
    for (b, h), ic, bc, dm, dx in zip(chains, i_col, b_col, dmat, dmax):
        hs = slice(h * R_DH, (h + 1) * R_DH)
        q = qkv_ref[b, :, hs]
        k = qkv_ref[b, :, R_WIDTH + h * R_DH:R_WIDTH + (h + 1) * R_DH] * (R_DH ** -0.5)
        v = qkv_ref[b, :, 2 * R_WIDTH + h * R_DH:2 * R_WIDTH + (h + 1) * R_DH]
        m_prev = m_ref[0, b:b + 1, h:h + 1]
        inter = bc + m_prev
        m_t = jnp.maximum(inter, dx)
        m_new = m_t[t - 1:t, :]
        b_last = bc[t - 1:t, :]
        pre.append(dict(
            q=q, k=k, v=v, qb=_bf(q), kb=_bf(k), vb=_bf(v), m_t=m_t, m_new=m_new,
            w_inter=jnp.exp(inter - m_t), e=jnp.exp(dm - m_t),
            g_state=jnp.exp(b_last + m_prev - m_new),
            g_tok=jnp.exp(b_last - bc + ic - m_new)))
    qk_raw = [_dot_nt(p["qb"], p["kb"]) for p in pre]
    qc = [_dot_nt(p["qb"], _bf(c_ref[b, h])) for p, (b, h) in zip(pre, chains)]
    qk = [r * p["e"] for r, p in zip(qk_raw, pre)]
    pv = [_dot(_bf(x), p["vb"]) for x, p in zip(qk, pre)]
    upd = [_dot_tn(_bf(p["g_tok"] * p["v"]), p["kb"]) for p in pre]
    nvec = [n_ref[b, h:h + 1, :] for b, h in chains]
    qn = [jnp.sum(p["q"] * nv, axis=1, keepdims=True) for p, nv in zip(pre, nvec)]
    qks = [jnp.sum(x, axis=1, keepdims=True) for x in qk]
    hm = []
    for (b, h), p, qn_c, qks_c, qc_c, pv_c in zip(chains, pre, qn, qks, qc, pv):
        num = p["w_inter"] * qc_c + pv_c
        den = p["w_inter"] * qn_c + qks_c
        hh = num / jnp.maximum(jnp.abs(den), jnp.exp(-p["m_t"]))
        hm.append(jax.nn.sigmoid(oz_ref[b, :, h * R_DH:(h + 1) * R_DH]) * hh)
    ms = [jnp.mean(x * x, axis=-1, keepdims=True) for x in hm]
    for (b, h), p, nv, upd_c, hm_c, ms_c in zip(chains, pre, nvec, upd, hm, ms):
        hs = slice(h * R_DH, (h + 1) * R_DH)
        c_ref[b, h] = p["g_state"] * c_ref[b, h] + upd_c
        n_ref[b, h:h + 1, :] = p["g_state"] * nv + jnp.sum(p["g_tok"] * p["k"], axis=0, keepdims=True)
        m_ref[0, b:b + 1, h:h + 1] = p["m_new"]
        om = hm_c * lax.rsqrt(ms_c + EPS) * nrm_ref[0:1, hs]
        out_ref[b, :, hs] = _bf(om * _silu(oz_ref[b, :, R_WIDTH + h * R_DH:R_WIDTH + (h + 1) * R_DH]))


def _mlstm(proj3, lw, state, bb, t, shared_init):
    nb, seq, _ = proj3.shape
    c0, n0, m0 = state
    sb = 1 if shared_init else bb
    m0 = m0.reshape(-1, sb, R_HEADS)
    st = (lambda i, c: (0, 0, 0, 0)) if shared_init else (lambda i, c: (i, 0, 0, 0))
    st3 = (lambda i, c: (0, 0, 0)) if shared_init else (lambda i, c: (i, 0, 0))
    out, c1, n1, m1 = pl.pallas_call(
        functools.partial(_mlstm_kernel, bb=bb, t=t, shared_init=shared_init),
        grid=(nb // bb, seq // t),
        in_specs=[
            pl.BlockSpec((bb, t, 3 * R_WIDTH), lambda i, c: (i, c, OFF_MQKV // (3 * R_WIDTH))),
            pl.BlockSpec((bb, t, 2 * R_WIDTH), lambda i, c: (i, c, OFF_MOZ // (2 * R_WIDTH))),
            pl.BlockSpec((bb, t, LANES), lambda i, c: (i, c, OFF_SMALL // LANES)),
            pl.BlockSpec((2, R_HEADS), lambda i, c: (0, 0)),
            pl.BlockSpec((1, R_WIDTH), lambda i, c: (0, 0)),
            pl.BlockSpec((sb, R_HEADS, R_DH, R_DH), st),
            pl.BlockSpec((sb, R_HEADS, R_DH), st3),
            pl.BlockSpec((1, sb, R_HEADS), st3),
        ],
        out_specs=[
            pl.BlockSpec((bb, t, R_WIDTH), lambda i, c: (i, c, 0)),
            pl.BlockSpec((bb, R_HEADS, R_DH, R_DH), lambda i, c: (i, 0, 0, 0)),
            pl.BlockSpec((bb, R_HEADS, R_DH), lambda i, c: (i, 0, 0)),
            pl.BlockSpec((1, bb, R_HEADS), lambda i, c: (i, 0, 0)),
        ],
        out_shape=[
            jax.ShapeDtypeStruct((nb, seq, R_WIDTH), BF16),
            jax.ShapeDtypeStruct((nb, R_HEADS, R_DH, R_DH), F32),
            jax.ShapeDtypeStruct((nb, R_HEADS, R_DH), F32),
            jax.ShapeDtypeStruct((nb // bb, bb, R_HEADS), F32),
        ],
        compiler_params=_cparams(("arbitrary", "arbitrary")),
        name="mlstm",
    )(proj3, proj3, proj3, lw["m_gate_b"], lw["m_norm"], c0, n0, m0)
    return out, (c1, n1, m1.reshape(nb, R_HEADS))


def _neumann_all(a_list, t, nil):
    levels = int(math.log2(nil)) - 1
    n_acc = [-a for a in a_list]
    pw = [_dot(_bf(a), _bf(a)) for a in a_list]
    for _ in range(levels - 1):
        r = [_dot(_bf(jnp.concatenate([p, n], axis=0)), _bf(p)) for p, n in zip(pw, n_acc)]
        n_acc = [n + p + x[t:] for n, p, x in zip(n_acc, pw, r)]
        pw = [x[:t] for x in r]
    r = [_dot(_bf(n), _bf(p)) for p, n in zip(pw, n_acc)]
    return [n + p + x for n, p, x in zip(n_acc, pw, r)]


def _unit_lower_inverse_all(a_list, t):
    blk = min(t, TRI_BLOCK)
    row = lax.broadcasted_iota(jnp.int32, (t, t), 0)
    col = lax.broadcasted_iota(jnp.int32, (t, t), 1)
    shift = int(math.log2(blk))
    same = jnp.right_shift(row, shift) == jnp.right_shift(col, shift)
    n_list = _neumann_all([jnp.where(same, a, 0.0) for a in a_list], t, blk)
    if blk == t:
        return n_list
    eye = (row == col).astype(F32)
    w_list = [n + eye for n in n_list]
    while blk < t:
        inner = ((jnp.right_shift(row, shift + 1) == jnp.right_shift(col, shift + 1))
                 & (jnp.right_shift(row, shift) > jnp.right_shift(col, shift)))
        z = [_dot(_bf(jnp.where(inner, a, 0.0)), _bf(w)) for a, w in zip(a_list, w_list)]
        w_list = [w - _dot(_bf(w), _bf(x)) for w, x in zip(w_list, z)]
        blk *= 2
        shift += 1
    return [w - eye for w in w_list]


def _conv_silu_rows(xwin, r0, rb, cw_ref):
    xw = xwin[r0:r0 + rb + 8]
    conv = xw[8:8 + rb] * cw_ref[CONV_W - 1:CONV_W, :]
    for j in range(1, CONV_W):
        conv = conv + pltpu.roll(xw, j, 0)[8:8 + rb] * cw_ref[CONV_W - 1 - j:CONV_W - j, :]
    return _silu(conv)


def _gdn_kernel(x_ref, z_ref, sm_ref, cw_ref, alog_ref, dtb_ref, gn_ref, s0_ref, buf0_ref,
                out_ref, s_ref, buf_ref, xwin_ref, *, bb, t, shared_init):
    c = pl.program_id(1)

    @pl.when(c == 0)
    def _():
        for b in range(bb):
            src = 0 if shared_init else b
            s_ref[b] = s0_ref[src]
            xwin_ref[b, 0:8, :] = jnp.zeros((8, 3 * R_WIDTH), F32)
            xwin_ref[b, 8 - (CONV_W - 1):8, :] = buf0_ref[src]

    eye, tril, triu, strict = _chunk_masks(t)
    chains = [(b, h) for b in range(bb) for h in range(R_HEADS)]
    g_col = [-jnp.exp(alog_ref[0:1, h:h + 1])
             * jax.nn.softplus(sm_ref[b, :, SM_GA + h:SM_GA + h + 1] + dtb_ref[0:1, h:h + 1])
             for b, h in chains]
    g_row = [_to_row(x, eye) for x in g_col]
    gc_col = [jnp.sum(jnp.where(tril, x, 0.0), axis=1, keepdims=True) for x in g_row]
    gc_row = [jnp.sum(jnp.where(triu, x, 0.0), axis=0, keepdims=True) for x in g_col]
    acts = []
    for b in range(bb):
        xwin_ref[b, 8:8 + t, :] = x_ref[b]
        acts.append(_conv_silu_rows(xwin_ref.at[b], 0, t, cw_ref))
        tail = xwin_ref[b, 8 + t - (CONV_W - 1):8 + t, :]
        xwin_ref[b, 8 - (CONV_W - 1):8, :] = tail
        buf_ref[b] = tail
    gq = [acts[b][:, h * R_DH:(h + 1) * R_DH] for b, h in chains]
    gk = [acts[b][:, R_WIDTH + h * R_DH:R_WIDTH + (h + 1) * R_DH] for b, h in chains]
    q_ss = [jnp.sum(x * x, axis=1, keepdims=True) for x in gq]
    k_ss = [jnp.sum(x * x, axis=1, keepdims=True) for x in gk]
    pre = []
    for i, (b, h) in enumerate(chains):
        gv = acts[b][:, 2 * R_WIDTH + h * R_DH:2 * R_WIDTH + (h + 1) * R_DH]
        qn = gq[i] * lax.rsqrt(q_ss[i] + EPS) * (R_DH ** -0.5)
        kn = gk[i] * lax.rsqrt(k_ss[i] + EPS)
        beta = jax.nn.sigmoid(sm_ref[b, :, SM_GB + h:SM_GB + h + 1])
        eg = jnp.exp(gc_col[i])
        g_last = gc_col[i][t - 1:t, :]
        pre.append(dict(
            qn=qn, kn=kn, beta=beta, eg=eg, g_last=g_last,
            gam=jnp.exp(jnp.where(tril, gc_col[i] - gc_row[i], -jnp.inf)),
            rhs=jnp.concatenate([beta * gv, (beta * eg) * kn], axis=1),
            kdec=kn * jnp.exp(g_last - gc_col[i])))
    kq = [_dot_nt(_bf(jnp.concatenate([p["kn"], p["qn"]], axis=0)), _bf(p["kn"])) for p in pre]
    a_list = [jnp.where(strict, p["beta"] * x[:t] * p["gam"], 0.0) for p, x in zip(pre, kq)]
    n_inv = _unit_lower_inverse_all(a_list, t)
    sol = [p["rhs"] + _dot(_bf(n), _bf(p["rhs"])) for p, n in zip(pre, n_inv)]
    ws = [_dot(_bf(jnp.concatenate([s[:, R_DH:], p["qn"] * p["eg"]], axis=0)), _bf(s_ref[b, h]))
          for p, s, (b, h) in zip(pre, sol, chains)]
    delta = [_bf(s[:, :R_DH] - x[:t]) for s, x in zip(sol, ws)]
    o2 = [_dot(_bf(x[t:] * p["gam"]), d) for p, x, d in zip(pre, kq, delta)]
    upd = [_dot_tn(_bf(p["kdec"]), d) for p, d in zip(pre, delta)]
    o = [x[t:] + o2_c for x, o2_c in zip(ws, o2)]
    ms = [jnp.mean(x * x, axis=-1, keepdims=True) for x in o]
    for (b, h), p, o_c, ms_c, upd_c in zip(chains, pre, o, ms, upd):
        hs = slice(h * R_DH, (h + 1) * R_DH)
        s_ref[b, h] = jnp.exp(p["g_last"]) * s_ref[b, h] + upd_c
        out_ref[b, :, hs] = _bf(o_c * lax.rsqrt(ms_c + EPS) * gn_ref[...] * _silu(z_ref[b, :, hs]))


def _gdn(proj3, lw, state, bb, t, shared_init):
    nb, seq, _ = proj3.shape
    s0, buf0 = state
    sb = 1 if shared_init else bb
    st = (lambda i, c: (0, 0, 0, 0)) if shared_init else (lambda i, c: (i, 0, 0, 0))
    st3 = (lambda i, c: (0, 0, 0)) if shared_init else (lambda i, c: (i, 0, 0))
    const = lambda i, c: (0, 0)
    in_specs = [
        pl.BlockSpec((bb, t, 3 * R_WIDTH), lambda i, c: (i, c, OFF_GQKV // (3 * R_WIDTH))),
        pl.BlockSpec((bb, t, R_WIDTH), lambda i, c: (i, c, OFF_ZG // R_WIDTH)),
        pl.BlockSpec((bb, t, LANES), lambda i, c: (i, c, OFF_SMALL // LANES)),
        pl.BlockSpec((CONV_W, 3 * R_WIDTH), const),
        pl.BlockSpec((1, R_HEADS), const),
        pl.BlockSpec((1, R_HEADS), const),
        pl.BlockSpec((1, R_DH), const),
        pl.BlockSpec((sb, R_HEADS, R_DH, R_DH), st),
        pl.BlockSpec((sb, CONV_W - 1, 3 * R_WIDTH), st3),
    ]
    args = [proj3, proj3, proj3, lw["g_conv_w"], lw["g_a_log"], lw["g_dt_bias"], lw["g_norm"], s0, buf0]
    out_specs = [
        pl.BlockSpec((bb, t, R_WIDTH), lambda i, c: (i, c, 0)),
        pl.BlockSpec((bb, R_HEADS, R_DH, R_DH), lambda i, c: (i, 0, 0, 0)),
        pl.BlockSpec((bb, CONV_W - 1, 3 * R_WIDTH), lambda i, c: (i, 0, 0)),
    ]
    out_shape = [
        jax.ShapeDtypeStruct((nb, seq, R_WIDTH), BF16),
        jax.ShapeDtypeStruct((nb, R_HEADS, R_DH, R_DH), F32),
        jax.ShapeDtypeStruct((nb, CONV_W - 1, 3 * R_WIDTH), F32),
    ]
    out, s1, buf1 = pl.pallas_call(
        functools.partial(_gdn_kernel, bb=bb, t=t, shared_init=shared_init),
        grid=(nb // bb, seq // t),
        in_specs=in_specs,
        out_specs=out_specs,
        out_shape=out_shape,
        scratch_shapes=[pltpu.VMEM((bb, 8 + t, 3 * R_WIDTH), F32)],
        compiler_params=_cparams(("arbitrary", "arbitrary")),
        name="gdn",
    )(*args)
    return out, (s1, buf1)


def _outproj_kernel(*refs, final):
    if final:
        oa_ref, za_ref, mm_ref, mg_ref, x_ref, w_ref, fn_ref, y_ref = refs
    else:
        oa_ref, za_ref, mm_ref, mg_ref, x_ref, w_ref, y_ref = refs
    ma = oa_ref[...].astype(F32) * _silu(za_ref[...])
    acc = (_dot(_bf(ma), w_ref[0:R_WIDTH, :])
           + _dot(mm_ref[...], w_ref[R_WIDTH:2 * R_WIDTH, :])
           + _dot(mg_ref[...], w_ref[2 * R_WIDTH:3 * R_WIDTH, :]))
    hnew = x_ref[...] + acc
    y_ref[...] = _rms(hnew, fn_ref[...]) if final else hnew


def _outproj(oa, proj2, mm, mg, x2d, w_bf, final_norm):
    rows, d = x2d.shape
    tm = min(rows, 512)
    final = final_norm is not None
    in_specs = [
        pl.BlockSpec((tm, R_WIDTH), lambda i: (i, 0)),
        pl.BlockSpec((tm, R_WIDTH), lambda i: (i, OFF_ZA // R_WIDTH)),
        pl.BlockSpec((tm, R_WIDTH), lambda i: (i, 0)),
        pl.BlockSpec((tm, R_WIDTH), lambda i: (i, 0)),
        pl.BlockSpec((tm, d), lambda i: (i, 0)),
        pl.BlockSpec((3 * R_WIDTH, d), lambda i: (0, 0)),
    ]
    args = [oa, proj2, mm, mg, x2d, w_bf]
    if final:
        in_specs.append(pl.BlockSpec((1, d), lambda i: (0, 0)))
        args.append(final_norm.reshape(1, d))
    return pl.pallas_call(
        functools.partial(_outproj_kernel, final=final),
        grid=(rows // tm,),
        in_specs=in_specs,
        out_specs=pl.BlockSpec((tm, d), lambda i: (i, 0)),
        out_shape=jax.ShapeDtypeStruct((rows, d), F32),
        compiler_params=_cparams(("arbitrary",)),
        name="outproj",
    )(*args)


def _permute_w_in(w):
    d = w.shape[0]
    c_q, c_kv, k_r, z_a = w[:, 0:384], w[:, 384:640], w[:, 640:672], w[:, 672:1184]
    m_qkv, m_i, m_f = w[:, 1184:2720], w[:, 2720:2724], w[:, 2724:2728]
    m_oz = w[:, 2728:3752]
    g_qkv, g_a, g_b, z_g = w[:, 3752:5288], w[:, 5288:5292], w[:, 5292:5296], w[:, 5296:5808]
    small = jnp.concatenate([k_r, m_i, m_f, g_a, g_b, jnp.zeros((d, LANES - 48), w.dtype)], axis=1)
    return jnp.concatenate([g_qkv, m_qkv, m_oz, z_a, z_g, c_kv, c_q, small], axis=1).astype(BF16)


def _rope_tables(pos0, n):
    half = MLA_ROPE // 2
    freq = ROPE_BASE ** (-np.arange(half, dtype=np.float64) / half)
    ang = (pos0 + np.arange(n)).astype(np.float64)[:, None] * freq[None, :]
    cos, sin = np.cos(ang).astype(np.float32), np.sin(ang).astype(np.float32)
    one_lo = np.ones((n, MLA_NOPE), np.float32)
    one_hi = np.ones((n, LANES - MLA_NOPE - MLA_ROPE), np.float32)
    zero_lo = np.zeros((n, MLA_NOPE), np.float32)
    zero_hi = np.zeros((n, LANES - MLA_NOPE - MLA_ROPE), np.float32)
    zero_h = np.zeros((n, half), np.float32)
    tc = np.concatenate([one_lo, cos, cos, one_hi], axis=1)
    ts1 = np.concatenate([zero_lo, zero_h, sin, zero_hi], axis=1)
    ts2 = np.concatenate([zero_lo, -sin, zero_h, zero_hi], axis=1)
    return jnp.asarray(tc), jnp.asarray(ts1), jnp.asarray(ts2)


def _layer_weights(l, norm_w, w_in, mla_q_norm, mla_w_uq, mla_kv_norm, mla_w_uk, mla_w_uv, mlstm_gate_b,
                   mlstm_norm, gdn_conv_w, gdn_a_log, gdn_dt_bias, gdn_norm, w_out):
    pad = HEAD_PAD - (MLA_NOPE + MLA_ROPE)
    w_uq = mla_w_uq[l].reshape(MLA_Q_LORA, N_HEADS_MLA, MLA_NOPE + MLA_ROPE)
    w_uq = jnp.pad(w_uq, ((0, 0), (0, 0), (0, pad))).reshape(MLA_Q_LORA, N_HEADS_MLA * HEAD_PAD)
    w_uk = jnp.pad(mla_w_uk[l], ((0, 0), (0, 0), (0, HEAD_PAD - MLA_NOPE)))
    w_uvt = jnp.pad(jnp.transpose(mla_w_uv[l], (1, 2, 0)), ((0, 0), (0, VT_ROWS - MLA_V), (0, 0)))
    v_one = jnp.zeros((N_HEADS_MLA, VT_ROWS, 1), F32).at[:, MLA_V, :].set(1.0)
    return {
        "w_uvt": w_uvt.reshape(N_HEADS_MLA * VT_ROWS, MLA_KV_LORA).astype(BF16),
        "v_one": v_one.reshape(N_HEADS_MLA * VT_ROWS, 1),
        "norm_w": norm_w[l],
        "w_in": _permute_w_in(w_in[l]),
        "q_norm": mla_q_norm[l].reshape(1, -1),
        "w_uq": w_uq.astype(BF16),
        "kv_norm": mla_kv_norm[l].reshape(1, -1),
        "w_uk": w_uk.reshape(MLA_KV_LORA, N_HEADS_MLA * HEAD_PAD).astype(BF16),
        "w_uv": mla_w_uv[l].reshape(MLA_KV_LORA, N_HEADS_MLA * MLA_V).astype(BF16),
        "m_gate_b": mlstm_gate_b[l],
        "m_norm": mlstm_norm[l].reshape(1, -1),
        "g_conv_w": gdn_conv_w[l],
        "g_a_log": gdn_a_log[l].reshape(1, -1),
        "g_dt_bias": gdn_dt_bias[l].reshape(1, -1),
        "g_norm": gdn_norm[l].reshape(1, -1),
        "w_out": w_out[l].astype(BF16),
    }


def _recurrent_groups(proj3, lw, m_state, g_state, bb, t, shared_init):
    mm, m_state = _mlstm(proj3, lw, m_state, bb, t, shared_init)
    bb_g = 2 * bb if t == R_CHUNK and proj3.shape[0] % (2 * bb) == 0 else bb
    mg, g_state = _gdn(proj3, lw, g_state, bb_g, t, shared_init)
    return mm, mg, m_state, g_state


def kernel(x_prompt, x_sample, cache_mla_latent, cache_mla_krope, state_mlstm_C, state_mlstm_n, state_mlstm_m, state_gdn_S, state_gdn_conv, meta_tokens, norm_w, w_in, mla_q_norm, mla_w_uq, mla_kv_norm, mla_w_uk, mla_w_uv, mlstm_gate_b, mlstm_norm, gdn_conv_w, gdn_a_log, gdn_dt_bias, gdn_norm, w_out, final_norm):
    nb, seq, d = x_prompt.shape
    ns, dseq, _ = x_sample.shape
    n_meta = meta_tokens.shape[0]
    n_cache = cache_mla_latent.shape[2]
    depth = norm_w.shape[0]
    assert seq % 256 == 0 and n_meta % 8 == 0 and dseq % 8 == 0 and n_meta <= CHUNK and dseq <= CHUNK

    tabs_m = _rope_tables(0, n_meta)
    tabs_p = _rope_tables(n_meta, seq)
    tabs_s = _rope_tables(n_cache, dseq)

    h_m = meta_tokens.astype(F32)
    h_p = x_prompt.reshape(nb * seq, d)
    h_s = x_sample.reshape(ns * dseq, d)
    zero_m = (jnp.zeros((1, R_HEADS, R_DH, R_DH), F32), jnp.zeros((1, R_HEADS, R_DH), F32),
              jnp.zeros((1, R_HEADS), F32))
    zero_g = (jnp.zeros((1, R_HEADS, R_DH, R_DH), F32), jnp.zeros((1, CONV_W - 1, 3 * R_WIDTH), F32))
    bb_p = 2 if nb % 2 == 0 else 1
    bb_s = 4 if ns % 4 == 0 else 1
    p_rows, s_rows = [], []
    for l in range(depth):
        lw = _layer_weights(l, norm_w, w_in, mla_q_norm, mla_w_uq, mla_kv_norm, mla_w_uk, mla_w_uv,
                            mlstm_gate_b, mlstm_norm, gdn_conv_w, gdn_a_log, gdn_dt_bias, gdn_norm, w_out)
        last = l == depth - 1

        proj_m, q_m, k_m, v_m, c_m, kr_m = _inproj_mla(h_m, 1, tabs_m, lw)
        proj_m3 = proj_m.reshape(1, n_meta, NP)
        oa_m = _attention(q_m, k_m, v_m, None, n_meta, n_meta)
        mm_m, mg_m, mst, gst = _recurrent_groups(proj_m3, lw, zero_m, zero_g, 1, n_meta, True)
        if not last:
            h_m = _outproj(oa_m.reshape(n_meta, -1), proj_m, mm_m.reshape(n_meta, -1),
                           mg_m.reshape(n_meta, -1), h_m, lw["w_out"], None)

        proj_p, q_p, k_p, v_p, c_p, kr_p = _inproj_mla(h_p, nb, tabs_p, lw)
        proj_p3 = proj_p.reshape(nb, seq, NP)
        oa_p = _attention(q_p, k_p, v_p, (k_m, v_m), 256, 256)
        mm_p, mg_p, mst, gst = _recurrent_groups(proj_p3, lw, mst, gst, bb_p, R_CHUNK, True)
        h_p = _outproj(oa_p.reshape(nb * seq, -1), proj_p, mm_p.reshape(nb * seq, -1),
                       mg_p.reshape(nb * seq, -1), h_p, lw["w_out"], final_norm if last else None)
        p_rows.append((
            jnp.concatenate([jnp.broadcast_to(c_m, (nb,) + c_m.shape[1:]), c_p], axis=1),
            jnp.concatenate([jnp.broadcast_to(kr_m, (nb,) + kr_m.shape[1:]), kr_p], axis=1),
            mst[0], mst[1], mst[2], gst[0], gst[1]))

        proj_s = _inproj(h_s, lw["norm_w"], lw["w_in"])
        proj_s3 = proj_s.reshape(ns, dseq, NP)
        oa_s, c_s, kr_s = _sample_attention(proj_s3, cache_mla_latent, cache_mla_krope, l, tabs_s, lw)
        mm_s, mg_s, sm_st, sg_st = _recurrent_groups(
            proj_s3, lw, (state_mlstm_C[l], state_mlstm_n[l], state_mlstm_m[l]),
            (state_gdn_S[l], state_gdn_conv[l]), bb_s, dseq, False)
        h_s = _outproj(oa_s.reshape(ns * dseq, -1), proj_s, mm_s.reshape(ns * dseq, -1),
                       mg_s.reshape(ns * dseq, -1), h_s, lw["w_out"], final_norm if last else None)
        s_rows.append((c_s, kr_s, sm_st[0], sm_st[1], sm_st[2], sg_st[0], sg_st[1]))

    y_prompt = h_p.reshape(nb, seq, d)
    y_sample = h_s.reshape(ns, dseq, d)
    stack = lambda rows, i: jnp.stack([r[i] for r in rows])
    return ((y_prompt, y_sample) + tuple(stack(p_rows, i) for i in range(7))
            + tuple(stack(s_rows, i) for i in range(7)))
```
